```python
import math
import jax, jax.numpy as jnp
from jax import lax
import numpy as np

D_MODEL = 2048
BATCH = 8
SEQ = 2048
DEPTH = 2

D_RNN = 1024
LRU_BLOCKS = 8
LRU_CONV = 4
LRU_C = 8.0
SSM_D_INNER = 1024
SSM_HEAD_DIM = 64
SSM_HEADS = SSM_D_INNER // SSM_HEAD_DIM
SSM_GROUPS = 2
SSM_D_STATE = 128
SSM_CONV = 4
SSM_CHUNK = 128
SSM_XBC = SSM_D_INNER + 2 * SSM_GROUPS * SSM_D_STATE
ATT_HEAD_DIM = 128
ATT_KV_HEADS = 8
ATT_PATTERNS = ((128, 1), (512, 4), (2048, 16))
ATT_GROUPS = len(ATT_PATTERNS)
ATT_Q_HEADS = ATT_GROUPS * ATT_KV_HEADS
ATT_BLOCK = 128
ATT_D_OUT = ATT_KV_HEADS * ATT_HEAD_DIM
N_BRANCH = 3
IN_SIZES = (N_BRANCH * D_MODEL,
            D_RNN, D_RNN,
            SSM_D_INNER, SSM_XBC, SSM_HEADS,
            ATT_Q_HEADS * ATT_HEAD_DIM, ATT_D_OUT, ATT_D_OUT)
D_IN = sum(IN_SIZES)
N_EXPERTS = 64
EXPERT_DIM = 512
TOP_K = 8
N_EXPERT_GROUPS = 8
TOPK_GROUPS = 4
ROUTED_SCALE = 2.5
MOE_BLOCK = 128
DEEPNORM_ALPHA = (2 * DEPTH) ** 0.25
DEEPNORM_BETA = (8 * DEPTH) ** -0.25
LN_EPS = 1e-5
RMS_EPS = 1e-6

kernel_name = "hybrid_rglru_ssd_dilated_attn_moe"

F32 = jnp.float32


def layer_norm(x, g, b):
    xf = x.astype(F32)
    mu = xf.mean(-1, keepdims=True)
    xc = xf - mu
    var = (xc * xc).mean(-1, keepdims=True)
    return (xc * lax.rsqrt(var + LN_EPS) * g + b).astype(x.dtype)


def causal_dwconv(x, w, b):
    K = w.shape[0]
    T = x.shape[1]
    xp = jnp.pad(x, ((0, 0), (K - 1, 0), (0, 0)))
    return sum(xp[:, k:k + T] * w[k] for k in range(K)) + b


def rglru_branch(xr, gate, conv_w, conv_b, wr, br, wi, bi, lam):
    B, T, _ = xr.shape
    xc = causal_dwconv(xr, conv_w, conv_b)
    xb = xc.reshape(B, T, LRU_BLOCKS, D_RNN // LRU_BLOCKS)
    r = jax.nn.sigmoid((jnp.einsum("btnc,ncd->btnd", xb, wr).reshape(B, T, D_RNN) + br).astype(F32))
    i = jax.nn.sigmoid((jnp.einsum("btnc,ncd->btnd", xb, wi).reshape(B, T, D_RNN) + bi).astype(F32))
    log_a = -LRU_C * r * jax.nn.softplus(-lam.astype(F32))
    a = jnp.exp(log_a)
    u = jnp.sqrt(-jnp.expm1(2.0 * log_a)) * i * xc.astype(F32)

    def combine(left, right):
        a1, u1 = left
        a2, u2 = right
        return a1 * a2, a2 * u1 + u2

    _, h = lax.associative_scan(combine, (a, u), axis=1)
    return (h * jax.nn.gelu(gate.astype(F32))).astype(xr.dtype)


def ssd_chunked(xh, dt, a, bm, cm):
    B, T, H, P = xh.shape
    G, N = bm.shape[2], bm.shape[3]
    HG = H // G
    L = SSM_CHUNK
    NC = T // L
    xdt = (xh * dt[..., None]).reshape(B, NC, L, G, HG, P)
    adt = (dt * a).reshape(B, NC, L, G, HG).transpose(0, 1, 3, 4, 2)
    bc = bm.reshape(B, NC, L, G, N)
    cc = cm.reshape(B, NC, L, G, N)
    a_cum = jnp.cumsum(adt, axis=-1)
    causal = jnp.tril(jnp.ones((L, L), bool))
    seg = a_cum[..., :, None] - a_cum[..., None, :]
    decay = jnp.where(causal, jnp.exp(jnp.where(causal, seg, 0.0)), 0.0)
    cb = jnp.einsum("bclgn,bcsgn->bcgls", cc, bc)
    y_diag = jnp.einsum("bcghls,bcsghp->bclghp", cb[:, :, :, None] * decay, xdt)
    decay_states = jnp.exp(a_cum[..., -1:] - a_cum)
    states = jnp.einsum("bclgn,bcghl,bclghp->bcghpn", bc, decay_states, xdt)
    a_last = a_cum[..., -1]
    cl = jnp.cumsum(a_last, axis=1)
    cl_ex = cl - a_last
    chunk_causal = jnp.tril(jnp.ones((NC, NC), bool), k=-1)[None, :, :, None, None]
    seg_c = cl_ex[:, :, None] - cl[:, None, :]
    decay_c = jnp.where(chunk_causal, jnp.exp(jnp.where(chunk_causal, seg_c, 0.0)), 0.0)
    entering = jnp.einsum("bzcgh,bcghpn->bzghpn", decay_c, states)
    y_off = jnp.einsum("bclgn,bcghpn,bcghl->bclghp", cc, entering, jnp.exp(a_cum))
    return (y_diag + y_off).reshape(B, T, H, P)


def mamba2_branch(z, xbc, dt_raw, conv_w, conv_b, dt_bias, a_log, d_skip, norm_g):
    B, T, _ = z.shape
    xbc = jax.nn.silu(causal_dwconv(xbc, conv_w, conv_b))
    xs = xbc[..., :SSM_D_INNER]
    bm = xbc[..., SSM_D_INNER:SSM_D_INNER + SSM_GROUPS * SSM_D_STATE]
    cm = xbc[..., SSM_D_INNER + SSM_GROUPS * SSM_D_STATE:]
    xh = xs.reshape(B, T, SSM_HEADS, SSM_HEAD_DIM).astype(F32)
    bm = bm.reshape(B, T, SSM_GROUPS, SSM_D_STATE).astype(F32)
    cm = cm.reshape(B, T, SSM_GROUPS, SSM_D_STATE).astype(F32)
    dt = jax.nn.softplus(dt_raw.astype(F32) + dt_bias.astype(F32))
    a = -jnp.exp(a_log.astype(F32))
    y = ssd_chunked(xh, dt, a, bm, cm) + d_skip.astype(F32)[:, None] * xh
    y = y.reshape(B, T, SSM_D_INNER) * jax.nn.silu(z.astype(F32))
    yg = y.reshape(B, T, SSM_GROUPS, SSM_D_INNER // SSM_GROUPS)
    yg = yg * lax.rsqrt((yg * yg).mean(-1, keepdims=True) + RMS_EPS)
    return (yg.reshape(B, T, SSM_D_INNER) * norm_g).astype(z.dtype)


def alibi_slopes(n):
    return 2.0 ** (-8.0 * jnp.arange(1, n + 1, dtype=F32) / n)


def dilated_window_attention(q, k, v, slopes, window, dilation):
    B, T, H, Dh = q.shape
    U = T // dilation
    W = window // dilation
    blk = ATT_BLOCK
    nblk = -(-U // blk)
    U_pad = nblk * blk
    Z = B * dilation

    def to_sub(t):
        t = t.reshape(B, U, dilation, H, Dh).transpose(0, 2, 1, 3, 4).reshape(Z, U, H, Dh)
        return jnp.pad(t, ((0, 0), (0, U_pad - U), (0, 0), (0, 0)))

    def band(t):
        tp = jnp.pad(t, ((0, 0), (blk, 0), (0, 0), (0, 0))).reshape(Z, nblk + 1, blk, H, Dh)
        return jnp.concatenate([tp[:, :-1], tp[:, 1:]], axis=2)

    qb = to_sub(q).reshape(Z, nblk, blk, H, Dh)
    kb = band(to_sub(k))
    vb = band(to_sub(v))
    s = jnp.einsum("znqhd,znkhd->znhqk", qb, kb).astype(F32) * (Dh ** -0.5)
    dist = blk + jnp.arange(blk)[:, None] - jnp.arange(2 * blk)[None, :]
    key_pos = jnp.arange(nblk)[:, None] * blk - blk + jnp.arange(2 * blk)[None, :]
    valid = ((dist >= 0) & (dist <= W))[None] & (key_pos >= 0)[:, None, :]
    s = s - slopes[:, None, None] * (dilation * dist).astype(F32)
    s = jnp.where(valid[None, :, None], s, -jnp.inf)
    m = s.max(-1, keepdims=True)
    p = jnp.exp(s - m)
    l = p.sum(-1, keepdims=True)
    o = jnp.einsum("znhqk,znkhd->znqhd", (p / l).astype(v.dtype), vb)
    lse = (m + jnp.log(l))[..., 0]
    o = o.reshape(B, dilation, U_pad, H, Dh)[:, :, :U].transpose(0, 2, 1, 3, 4).reshape(B, T, H, Dh)
    lse = lse.transpose(0, 1, 3, 2).reshape(B, dilation, U_pad, H)[:, :, :U]
    lse = lse.transpose(0, 2, 1, 3).reshape(B, T, H)
    return o, lse


def attention_branch(q, k, v):
    B, T, _ = q.shape
    q = q.reshape(B, T, ATT_GROUPS, ATT_KV_HEADS, ATT_HEAD_DIM)
    k = k.reshape(B, T, ATT_KV_HEADS, ATT_HEAD_DIM)
    v = v.reshape(B, T, ATT_KV_HEADS, ATT_HEAD_DIM)
    slopes = alibi_slopes(ATT_Q_HEADS).reshape(ATT_GROUPS, ATT_KV_HEADS)
    outs, lses = [], []
    for gi, (window, dilation) in enumerate(ATT_PATTERNS):
        o, lse = dilated_window_attention(q[:, :, gi], k, v, slopes[gi], window, dilation)
        outs.append(o)
        lses.append(lse)
    wts = jax.nn.softmax(jnp.stack(lses), axis=0)
    o = (wts[..., None] * jnp.stack(outs).astype(F32)).sum(0)
    return o.reshape(B, T, ATT_D_OUT).astype(q.dtype)


def hybrid_mixer(x, w_in, b_gate, lru_conv_w, lru_conv_b, lru_wr, lru_br, lru_wi, lru_bi,
                 lru_lambda, ssm_conv_w, ssm_conv_b, ssm_dt_bias, ssm_a_log, ssm_d, ssm_norm_g,
                 w_proj_lru, w_proj_ssm, w_proj_att, w_out):
    B, T, _ = x.shape
    proj = x @ w_in
    splits = np.cumsum(IN_SIZES)[:-1].tolist()
    g_raw, lru_x, lru_g, ssm_z, ssm_xbc, ssm_dt, att_q, att_k, att_v = jnp.split(proj, splits, axis=-1)
    gates = jax.nn.sigmoid((g_raw + b_gate).astype(F32)).reshape(B, T, N_BRANCH, D_MODEL).astype(x.dtype)
    y_lru = rglru_branch(lru_x, lru_g, lru_conv_w, lru_conv_b, lru_wr, lru_br, lru_wi, lru_bi, lru_lambda)
    y_ssm = mamba2_branch(ssm_z, ssm_xbc, ssm_dt, ssm_conv_w, ssm_conv_b, ssm_dt_bias, ssm_a_log, ssm_d, ssm_norm_g)
    y_att = attention_branch(att_q, att_k, att_v)
    merged = (gates[:, :, 0] * (y_lru @ w_proj_lru)
              + gates[:, :, 1] * (y_ssm @ w_proj_ssm)
              + gates[:, :, 2] * (y_att @ w_proj_att))
    return merged @ w_out


def moe_ffn(x2d, router_w, router_bias, w1, w3, w2, ws1, ws3, ws2):
    N, D = x2d.shape
    E, K = N_EXPERTS, TOP_K
    A = N * K
    scores = jax.nn.sigmoid(x2d.astype(F32) @ router_w.astype(F32))
    choice = scores + router_bias.astype(F32)
    grp = choice.reshape(N, N_EXPERT_GROUPS, E // N_EXPERT_GROUPS)
    grp_score = lax.top_k(grp, 2)[0].sum(-1)
    _, grp_idx = lax.top_k(grp_score, TOPK_GROUPS)
    grp_keep = jax.nn.one_hot(grp_idx, N_EXPERT_GROUPS, dtype=F32).sum(1) > 0
    expert_keep = jnp.repeat(grp_keep, E // N_EXPERT_GROUPS, axis=1)
    _, idx = lax.top_k(jnp.where(expert_keep, choice, -jnp.inf), K)
    wts = jnp.take_along_axis(scores, idx, axis=1)
    wts = wts / wts.sum(-1, keepdims=True) * ROUTED_SCALE
    blk = MOE_BLOCK
    nb = (A + E * (blk - 1) + blk - 1) // blk
    P = nb * blk
    e_flat = idx.reshape(A)
    order = jnp.argsort(e_flat)
    e_sorted = e_flat[order]
    counts = jnp.bincount(e_flat, length=E)
    padded = (counts + blk - 1) // blk * blk
    pad_end = jnp.cumsum(padded)
    pad_start = pad_end - padded
    start = jnp.cumsum(counts) - counts
    dest = pad_start[e_sorted] + jnp.arange(A) - start[e_sorted]
    row_tok = jnp.full((P,), N, jnp.int32).at[dest].set((order // K).astype(jnp.int32))
    row_w = jnp.zeros((P,), F32).at[dest].set(wts.reshape(A)[order])
    block_e = jnp.minimum(jnp.searchsorted(pad_end, jnp.arange(nb) * blk, side="right"), E - 1)
    x_pad = jnp.concatenate([x2d, jnp.zeros((1, D), x2d.dtype)], axis=0)

    def expert_block(args):
        tok, wb, e = args
        xb = x_pad[tok]
        hb = jax.nn.silu(xb @ w1[e]) * (xb @ w3[e])
        return (hb @ w2[e]) * wb[:, None].astype(x2d.dtype)

    rows = lax.map(expert_block, (row_tok.reshape(nb, blk), row_w.reshape(nb, blk), block_e))
    routed = jax.ops.segment_sum(rows.reshape(P, D), row_tok, num_segments=N + 1)[:N]
    shared = (jax.nn.silu(x2d @ ws1) * (x2d @ ws3)) @ ws2
    return routed + shared


def setup_inputs(seed: int = 0) -> dict:
    key = jax.random.key(seed)
    ks = iter(jax.random.split(key, 40))
    L, D = DEPTH, D_MODEL
    bs = D_RNN // LRU_BLOCKS

    def nrm(shape, scale):
        return jax.random.normal(next(ks), shape, F32) * scale

    x = nrm((BATCH, SEQ, D), 1.0)
    emb_ln_g = 1.0 + nrm((D,), 0.05)
    emb_ln_b = nrm((D,), 0.02)
    w_in = nrm((L, D, D_IN), D ** -0.5)
    b_gate = nrm((L, N_BRANCH * D), 0.1)
    lru_conv_w = nrm((L, LRU_CONV, D_RNN), LRU_CONV ** -0.5)
    lru_conv_b = nrm((L, D_RNN), 0.02)
    lru_wr = nrm((L, LRU_BLOCKS, bs, bs), bs ** -0.5)
    lru_br = nrm((L, D_RNN), 0.1)
    lru_wi = nrm((L, LRU_BLOCKS, bs, bs), bs ** -0.5)
    lru_bi = nrm((L, D_RNN), 0.1)
    a0 = jax.random.uniform(next(ks), (L, D_RNN), F32, 0.9, 0.999) ** (1.0 / LRU_C)
    lru_lambda = jnp.log(a0) - jnp.log1p(-a0)
    ssm_conv_w = nrm((L, SSM_CONV, SSM_XBC), SSM_CONV ** -0.5)
    ssm_conv_b = nrm((L, SSM_XBC), 0.02)
    dt0 = jnp.exp(jax.random.uniform(next(ks), (L, SSM_HEADS), F32, math.log(1e-3), math.log(1e-1)))
    ssm_dt_bias = dt0 + jnp.log(-jnp.expm1(-dt0))
    ssm_a_log = jnp.log(jax.random.uniform(next(ks), (L, SSM_HEADS), F32, 1.0, 16.0))
    ssm_d = 1.0 + nrm((L, SSM_HEADS), 0.1)
    ssm_norm_g = 1.0 + nrm((L, SSM_D_INNER), 0.05)
    w_proj_lru = nrm((L, D_RNN, D), D_RNN ** -0.5)
    w_proj_ssm = nrm((L, SSM_D_INNER, D), SSM_D_INNER ** -0.5)
    w_proj_att = nrm((L, ATT_D_OUT, D), ATT_D_OUT ** -0.5)
    w_out = nrm((L, D, D), D ** -0.5 * DEEPNORM_BETA)
    ln1_g = 1.0 + nrm((L, D), 0.05)
    ln1_b = nrm((L, D), 0.02)
    router_w = nrm((L, D, N_EXPERTS), D ** -0.5)
    router_bias = nrm((L, N_EXPERTS), 0.01)
    w1 = nrm((L, N_EXPERTS, D, EXPERT_DIM), D ** -0.5)
    w3 = nrm((L, N_EXPERTS, D, EXPERT_DIM), D ** -0.5)
    w2 = nrm((L, N_EXPERTS, EXPERT_DIM, D), EXPERT_DIM ** -0.5 * DEEPNORM_BETA)
    ws1 = nrm((L, D, EXPERT_DIM), D ** -0.5)
    ws3 = nrm((L, D, EXPERT_DIM), D ** -0.5)
    ws2 = nrm((L, EXPERT_DIM, D), EXPERT_DIM ** -0.5 * DEEPNORM_BETA)
    ln2_g = 1.0 + nrm((L, D), 0.05)
    ln2_b = nrm((L, D), 0.02)
    return {"x": x, "emb_ln_g": emb_ln_g, "emb_ln_b": emb_ln_b, "w_in": w_in, "b_gate": b_gate,
            "lru_conv_w": lru_conv_w, "lru_conv_b": lru_conv_b, "lru_wr": lru_wr, "lru_br": lru_br,
            "lru_wi": lru_wi, "lru_bi": lru_bi, "lru_lambda": lru_lambda,
            "ssm_conv_w": ssm_conv_w, "ssm_conv_b": ssm_conv_b, "ssm_dt_bias": ssm_dt_bias,
            "ssm_a_log": ssm_a_log, "ssm_d": ssm_d, "ssm_norm_g": ssm_norm_g,
            "w_proj_lru": w_proj_lru, "w_proj_ssm": w_proj_ssm, "w_proj_att": w_proj_att, "w_out": w_out,
            "ln1_g": ln1_g, "ln1_b": ln1_b, "router_w": router_w, "router_bias": router_bias,
            "w1": w1, "w3": w3, "w2": w2, "ws1": ws1, "ws3": ws3, "ws2": ws2,
            "ln2_g": ln2_g, "ln2_b": ln2_b}


def reference(x, emb_ln_g, emb_ln_b, w_in, b_gate, lru_conv_w, lru_conv_b, lru_wr, lru_br,
              lru_wi, lru_bi, lru_lambda, ssm_conv_w, ssm_conv_b, ssm_dt_bias, ssm_a_log, ssm_d,
              ssm_norm_g, w_proj_lru, w_proj_ssm, w_proj_att, w_out, ln1_g, ln1_b, router_w,
              router_bias, w1, w3, w2, ws1, ws3, ws2, ln2_g, ln2_b):
    B, T, D = x.shape
    h = layer_norm(x, emb_ln_g, emb_ln_b)
    for l in range(DEPTH):
        mix = hybrid_mixer(h, w_in[l], b_gate[l], lru_conv_w[l], lru_conv_b[l], lru_wr[l], lru_br[l],
                           lru_wi[l], lru_bi[l], lru_lambda[l], ssm_conv_w[l], ssm_conv_b[l],
                           ssm_dt_bias[l], ssm_a_log[l], ssm_d[l], ssm_norm_g[l],
                           w_proj_lru[l], w_proj_ssm[l], w_proj_att[l], w_out[l])
        h = layer_norm(DEEPNORM_ALPHA * h + mix, ln1_g[l], ln1_b[l])
        ffn = moe_ffn(h.reshape(B * T, D), router_w[l], router_bias[l], w1[l], w3[l], w2[l],
                      ws1[l], ws3[l], ws2[l]).reshape(B, T, D)
        h = layer_norm(DEEPNORM_ALPHA * h + ffn, ln2_g[l], ln2_b[l])
    return h
```

```python
import functools
import math

import jax
import jax.numpy as jnp
from jax import lax
from jax.experimental import pallas as pl
from jax.experimental.pallas import tpu as pltpu

F32 = jnp.float32
BF16 = jnp.bfloat16
HIGHEST = lax.Precision.HIGHEST

D_MODEL = 2048
DEPTH = 2
D_RNN = 1024
LRU_BLOCKS = 8
LRU_CONV = 4
LRU_C = 8.0
SSM_D_INNER = 1024
SSM_HEAD_DIM = 64
SSM_HEADS = SSM_D_INNER // SSM_HEAD_DIM
SSM_GROUPS = 2
SSM_D_STATE = 128
SSM_CONV = 4
SSM_CHUNK = 128
SSM_XBC = SSM_D_INNER + 2 * SSM_GROUPS * SSM_D_STATE
ATT_HEAD_DIM = 128
ATT_KV_HEADS = 8
ATT_PATTERNS = ((128, 1), (512, 4), (2048, 16))
ATT_GROUPS = len(ATT_PATTERNS)
ATT_Q_HEADS = ATT_GROUPS * ATT_KV_HEADS
ATT_BLOCK = 128
ATT_D_OUT = ATT_KV_HEADS * ATT_HEAD_DIM
N_BRANCH = 3
N_EXPERTS = 64
EXPERT_DIM = 512
TOP_K = 8
N_EXPERT_GROUPS = 8
TOPK_GROUPS = 4
ROUTED_SCALE = 2.5
DEEPNORM_ALPHA = (2 * DEPTH) ** 0.25
LN_EPS = 1e-5
RMS_EPS = 1e-6

LANES = 128
SUBLANES = 8
VMEM_LIMIT_BYTES = 56 * 1024 * 1024

DT_PAD = 512 - SSM_HEADS
MAIN_COLS = N_BRANCH * D_MODEL + 2 * D_RNN + SSM_XBC + SSM_HEADS + DT_PAD + SSM_D_INNER
COL_LRU_X = N_BRANCH * D_MODEL
COL_LRU_G = COL_LRU_X + D_RNN
COL_XBCDT = COL_LRU_G + D_RNN
XBCDT_W = SSM_XBC + SSM_HEADS + DT_PAD
COL_SSM_Z = COL_XBCDT + XBCDT_W

MOE_ROWS = 256


def _cparams(sem):
    return pltpu.CompilerParams(dimension_semantics=sem, vmem_limit_bytes=VMEM_LIMIT_BYTES)


def _layer_norm(x, g, b):
    mu = jnp.mean(x, axis=-1, keepdims=True)
    xc = x - mu
    var = jnp.mean(xc * xc, axis=-1, keepdims=True)
    return xc * lax.rsqrt(var + LN_EPS) * g + b


def _ln_kernel(x_ref, g_ref, b_ref, h_ref, hb_ref):
    y = _layer_norm(x_ref[...], g_ref[...], b_ref[...])
    h_ref[...] = y
    hb_ref[...] = y.astype(BF16)


def _embed_ln(x2d, g, b):
    n, d = x2d.shape
    tm = 512
    return pl.pallas_call(
        _ln_kernel,
        grid=(n // tm,),
        in_specs=[pl.BlockSpec((tm, d), lambda i: (i, 0)),
                  pl.BlockSpec((1, d), lambda i: (0, 0)),
                  pl.BlockSpec((1, d), lambda i: (0, 0))],
        out_specs=[pl.BlockSpec((tm, d), lambda i: (i, 0)),
                   pl.BlockSpec((tm, d), lambda i: (i, 0))],
        out_shape=[jax.ShapeDtypeStruct((n, d), F32), jax.ShapeDtypeStruct((n, d), BF16)],
        compiler_params=_cparams(("parallel",)),
        name="embed_ln",
    )(x2d, g.reshape(1, d), b.reshape(1, d))


def _mm_kernel(a_ref, b_ref, o_ref):
    o_ref[...] = jnp.dot(a_ref[...], b_ref[...], preferred_element_type=F32).astype(o_ref.dtype)


def _matmul(a, b, tm, tn, name):
    m, k = a.shape
    _, n = b.shape
    return pl.pallas_call(
        _mm_kernel,
        grid=(m // tm, n // tn),
        in_specs=[pl.BlockSpec((tm, k), lambda i, j: (i, 0)),
                  pl.BlockSpec((k, tn), lambda i, j: (0, j))],
        out_specs=pl.BlockSpec((tm, tn), lambda i, j: (i, j)),
        out_shape=jax.ShapeDtypeStruct((m, n), BF16),
        compiler_params=_cparams(("parallel", "parallel")),
        name=name,
    )(a, b)


def _store_dilated(acc_ref, dst_ref, d):
    nc, t, _ = acc_ref.shape
    u = t // d
    for c in range(nc):
        cols = slice(c * LANES, (c + 1) * LANES)
        if d == 1:
            dst_ref[:, cols] = acc_ref[c].astype(dst_ref.dtype)
            continue
        for r in range(d):
            dst_ref[r * u:(r + 1) * u, cols] = acc_ref[c, pl.ds(r, u, stride=d), :].astype(dst_ref.dtype)


def _dot_to_lane_tiles(a_ref, w_ref, acc_ref):
    res = jnp.dot(a_ref[...], w_ref[...], preferred_element_type=F32)
    for c in range(acc_ref.shape[0]):
        acc_ref[c] = res[:, c * LANES:(c + 1) * LANES]


def _q_proj_kernel(a_ref, w_ref, o_ref, acc_ref, *, tiles_per_group):
    j = pl.program_id(1)
    _dot_to_lane_tiles(a_ref, w_ref, acc_ref)
    for gi, (_, d) in enumerate(ATT_PATTERNS):
        @pl.when(j // tiles_per_group == gi)
        def _(d=d):
            _store_dilated(acc_ref, o_ref, d)


def _kv_proj_kernel(a_ref, w_ref, o_ref, acc_ref):
    _dot_to_lane_tiles(a_ref, w_ref, acc_ref)
    for gi, (_, d) in enumerate(ATT_PATTERNS):
        _store_dilated(acc_ref, o_ref.at[gi], d)


def _qkv_proj(hb, w_q, w_kv, batch, seq):
    n, k = hb.shape
    tn = 512
    nq = w_q.shape[1]
    q = pl.pallas_call(
        functools.partial(_q_proj_kernel, tiles_per_group=(nq // ATT_GROUPS) // tn),
        grid=(batch, nq // tn),
        in_specs=[pl.BlockSpec((seq, k), lambda b, j: (b, 0)),
                  pl.BlockSpec((k, tn), lambda b, j: (0, j))],
        out_specs=pl.BlockSpec((seq, tn), lambda b, j: (b, j)),
        out_shape=jax.ShapeDtypeStruct((n, nq), BF16),
        scratch_shapes=[pltpu.VMEM((tn // LANES, seq, LANES), F32)],
        compiler_params=_cparams(("parallel", "parallel")),
        name="q_proj",
    )(hb, w_q)
    nkv = w_kv.shape[1]
    kv = pl.pallas_call(
        _kv_proj_kernel,
        grid=(batch, nkv // tn),
        in_specs=[pl.BlockSpec((seq, k), lambda b, j: (b, 0)),
                  pl.BlockSpec((k, tn), lambda b, j: (0, j))],
        out_specs=pl.BlockSpec((ATT_GROUPS, seq, tn), lambda b, j: (0, b, j)),
        out_shape=jax.ShapeDtypeStruct((ATT_GROUPS, n, nkv), BF16),
        scratch_shapes=[pltpu.VMEM((tn // LANES, seq, LANES), F32)],
        compiler_params=_cparams(("parallel", "parallel")),
        name="kv_proj",
    )(hb, w_kv)
    return q, kv


def _attn_kernel(slopes_ref, q0_ref, q1_ref, q2_ref, k0_ref, k1_ref, k2_ref,
                 v0_ref, v1_ref, v2_ref, o_ref, acc_scr, m_scr, l_scr):
    h = pl.program_id(1)
    t = o_ref.shape[0]
    blk = ATT_BLOCK
    scale = ATT_HEAD_DIM ** -0.5
    qi = lax.broadcasted_iota(jnp.int32, (blk, 2 * blk), 0)
    kj = lax.broadcasted_iota(jnp.int32, (blk, 2 * blk), 1)
    dist = blk + qi - kj
    q_refs = (q0_ref, q1_ref, q2_ref)
    k_refs = (k0_ref, k1_ref, k2_ref)
    v_refs = (v0_ref, v1_ref, v2_ref)
    nt = (((1,), (1,)), ((), ()))
    for g, (window, d) in enumerate(ATT_PATTERNS):
        reach = window // d
        assert reach <= blk
        valid = (dist >= 0) & (dist <= reach)
        slope = slopes_ref[g, h]
        bias = jnp.where(valid, -(slope * d) * dist.astype(F32), -jnp.inf)
        bias_cur = bias[:, blk:]
        nb = (t // d) // blk
        q_ref, k_ref, v_ref = q_refs[g], k_refs[g], v_refs[g]
        for f in range(t // blk):
            r, i = divmod(f, nb)
            qb = q_ref[f * blk:(f + 1) * blk, :]
            if i > 0:
                kc = k_ref[(f - 1) * blk:(f + 1) * blk, :]
                vc = v_ref[(f - 1) * blk:(f + 1) * blk, :]
                s = lax.dot_general(qb, kc, nt, preferred_element_type=F32) * scale + bias
            else:
                kc = k_ref[f * blk:(f + 1) * blk, :]
                vc = v_ref[f * blk:(f + 1) * blk, :]
                s = lax.dot_general(qb, kc, nt, preferred_element_type=F32) * scale + bias_cur
            m = jnp.max(s, axis=-1, keepdims=True)
            p = jnp.exp(s - m)
            l = jnp.sum(p, axis=-1, keepdims=True)
            o = jnp.dot(p.astype(BF16), vc, preferred_element_type=F32)
            if d == 1:
                rows = slice(f * blk, (f + 1) * blk)
            else:
                rows = pl.ds(r + d * blk * i, blk, stride=d)
            acc_scr[g, rows, :] = o
            m_scr[g, rows, :] = jnp.broadcast_to(m, (blk, LANES))
            l_scr[g, rows, :] = jnp.broadcast_to(l, (blk, LANES))
    ch = 256
    for c in range(t // ch):
        rows = slice(c * ch, (c + 1) * ch)
        ms = [m_scr[g, rows, :] for g in range(ATT_GROUPS)]
        mx = jnp.maximum(jnp.maximum(ms[0], ms[1]), ms[2])
        num = jnp.zeros((ch, LANES), F32)
        den = jnp.zeros((ch, LANES), F32)
        for g in range(ATT_GROUPS):
            w = jnp.exp(ms[g] - mx)
            num = num + w * acc_scr[g, rows, :]
            den = den + w * l_scr[g, rows, :]
        o_ref[rows, :] = (num / den).astype(o_ref.dtype)


def _attention(q, kv, slopes, batch, seq):
    n = q.shape[0]
    hd = ATT_HEAD_DIM
    nh = ATT_KV_HEADS
    q_specs = [pl.BlockSpec((seq, hd), lambda b, h, g=g: (b, g * nh + h)) for g in range(ATT_GROUPS)]
    k_specs = [pl.BlockSpec((None, seq, hd), lambda b, h, g=g: (g, b, h)) for g in range(ATT_GROUPS)]
    v_specs = [pl.BlockSpec((None, seq, hd), lambda b, h, g=g: (g, b, nh + h)) for g in range(ATT_GROUPS)]
    return pl.pallas_call(
        _attn_kernel,
        grid=(batch, nh),
        in_specs=[pl.BlockSpec(memory_space=pltpu.SMEM)] + q_specs + k_specs + v_specs,
        out_specs=pl.BlockSpec((seq, hd), lambda b, h: (b, h)),
        out_shape=jax.ShapeDtypeStruct((n, nh * hd), BF16),
        scratch_shapes=[pltpu.VMEM((ATT_GROUPS, seq, hd), F32),
                        pltpu.VMEM((ATT_GROUPS, seq, LANES), F32),
                        pltpu.VMEM((ATT_GROUPS, seq, LANES), F32)],
        compiler_params=_cparams(("parallel", "parallel")),
        name="dilated_attention",
    )(slopes, q, q, q, kv, kv, kv, kv, kv, kv)


def _scan8(a, u, carry, row):
    for s in (1, 2, 4):
        a_sh = pltpu.roll(a, s, axis=0)
        u_sh = pltpu.roll(u, s, axis=0)
        m = row >= s
        u = jnp.where(m, a * u_sh + u, u)
        a = jnp.where(m, a * a_sh, a)
    return u + a * carry


def _lru_kernel(x_ref, g_ref, cw_ref, cb_ref, wr_ref, br_ref, wi_ref, bi_ref, lam_ref, o_ref,
                xpad_scr, a_scr, u_scr):
    t, c = o_ref.shape
    nb = LRU_BLOCKS
    bs = c // nb
    ch = 256
    pad = SUBLANES
    xpad_scr[0:pad, :] = jnp.zeros((pad, c), F32)
    for k in range(t // ch):
        xpad_scr[pad + k * ch:pad + (k + 1) * ch, :] = x_ref[k * ch:(k + 1) * ch, :].astype(F32)
    neg_lam = -lam_ref[...]
    sp = jnp.maximum(neg_lam, 0.0) + jnp.log1p(jnp.exp(-jnp.abs(neg_lam)))
    for k in range(t // ch):
        base = k * ch
        xc = cb_ref[...] + sum(
            cw_ref[j:j + 1, :] * xpad_scr[base + pad - (LRU_CONV - 1) + j:base + pad - (LRU_CONV - 1) + j + ch, :]
            for j in range(LRU_CONV))
        xcb = xc.astype(BF16)
        for n in range(nb):
            cols = slice(n * bs, (n + 1) * bs)
            xn = xcb[:, cols]
            r = jax.nn.sigmoid(jnp.dot(xn, wr_ref[n], preferred_element_type=F32) + br_ref[:, cols])
            ig = jax.nn.sigmoid(jnp.dot(xn, wi_ref[n], preferred_element_type=F32) + bi_ref[:, cols])
            log_a = -LRU_C * r * sp[:, cols]
            th = jnp.tanh(log_a)
            a_scr[base:base + ch, cols] = jnp.exp(log_a)
            u_scr[base:base + ch, cols] = jnp.sqrt(-2.0 * th / (1.0 - th)) * ig * xc[:, cols]
    row = lax.broadcasted_iota(jnp.int32, (SUBLANES, c), 0)

    def step(j, carry):
        rows = pl.ds(pl.multiple_of(j * SUBLANES, SUBLANES), SUBLANES)
        h8 = _scan8(a_scr[rows, :], u_scr[rows, :], carry, row)
        u_scr[rows, :] = h8
        return jnp.broadcast_to(h8[SUBLANES - 1:SUBLANES, :], (SUBLANES, c))

    lax.fori_loop(0, t // SUBLANES, step, jnp.zeros((SUBLANES, c), F32))
    for k in range(t // ch):
        rows = slice(k * ch, (k + 1) * ch)
        o_ref[rows, :] = (u_scr[rows, :] * jax.nn.gelu(g_ref[rows, :].astype(F32))).astype(o_ref.dtype)


def _rglru(proj, cw, cb, wr, br, wi, bi, lam, batch, seq):
    n = proj.shape[0]
    c = D_RNN
    bs = c // LRU_BLOCKS
    row = lambda b: (0, 0)
    return pl.pallas_call(
        _lru_kernel,
        grid=(batch,),
        in_specs=[pl.BlockSpec((seq, c), lambda b: (b, COL_LRU_X // c)),
                  pl.BlockSpec((seq, c), lambda b: (b, COL_LRU_G // c)),
                  pl.BlockSpec((LRU_CONV, c), row),
                  pl.BlockSpec((1, c), row),
                  pl.BlockSpec((LRU_BLOCKS, bs, bs), lambda b: (0, 0, 0)),
                  pl.BlockSpec((1, c), row),
                  pl.BlockSpec((LRU_BLOCKS, bs, bs), lambda b: (0, 0, 0)),
                  pl.BlockSpec((1, c), row),
                  pl.BlockSpec((1, c), row)],
        out_specs=pl.BlockSpec((seq, c), lambda b: (b, 0)),
        out_shape=jax.ShapeDtypeStruct((n, c), BF16),
        scratch_shapes=[pltpu.VMEM((seq + SUBLANES, c), F32),
                        pltpu.VMEM((seq, c), F32),
                        pltpu.VMEM((seq, c), F32)],
        compiler_params=_cparams(("parallel",)),
        name="rglru",
    )(proj, proj, cw, cb.reshape(1, c), wr, br.reshape(1, c), wi, bi.reshape(1, c), lam.reshape(1, c))


def _ssd_kernel(z_ref, xd_ref, cw_ref, cb_ref, dtb_ref, alog_ref, dskip_ref, ng_ref, o_ref,
                xpad_scr, st_scr):
    L = o_ref.shape[0]
    di = SSM_D_INNER
    ns = SSM_D_STATE
    pad = SUBLANES
    c = pl.program_id(1)

    @pl.when(c == 0)
    def _():
        xpad_scr[0:pad, :] = jnp.zeros((pad, SSM_XBC), F32)
        st_scr[...] = jnp.zeros(st_scr.shape, F32)

    xpad_scr[pad:pad + L, :] = xd_ref[:, :SSM_XBC].astype(F32)
    xc = cb_ref[...] + sum(
        cw_ref[j:j + 1, :] * xpad_scr[pad - (SSM_CONV - 1) + j:pad - (SSM_CONV - 1) + j + L, :]
        for j in range(SSM_CONV))
    xpad_scr[0:pad, :] = xpad_scr[L:L + pad, :]
    xc = xc * jax.nn.sigmoid(xc)
    xs = xc[:, :di]

    dt_in = xd_ref[:, SSM_XBC:SSM_XBC + LANES].astype(F32) + dtb_ref[...]
    dt = jnp.maximum(dt_in, 0.0) + jnp.log1p(jnp.exp(-jnp.abs(dt_in)))
    adt = dt * (-jnp.exp(alog_ref[...]))
    ri = lax.broadcasted_iota(jnp.int32, (L, L), 0)
    ci = lax.broadcasted_iota(jnp.int32, (L, L), 1)
    causal = ri >= ci
    acum = jnp.dot(causal.astype(F32), adt, precision=HIGHEST, preferred_element_type=F32)
    acum_t = acum.T
    a_last = acum[L - 1:L, :]
    hl = lax.broadcasted_iota(jnp.int32, (LANES, di), 0)
    cl = lax.broadcasted_iota(jnp.int32, (LANES, di), 1)
    expand = (cl // SSM_HEAD_DIM == hl).astype(F32)
    dt_c = jnp.dot(dt, expand, precision=HIGHEST, preferred_element_type=F32)
    ea_c = jnp.dot(jnp.exp(acum), expand, precision=HIGHEST, preferred_element_type=F32)
    ds_c = jnp.dot(jnp.exp(a_last - acum), expand, precision=HIGHEST, preferred_element_type=F32)
    xdt = xs * dt_c
    xdt_b = xdt.astype(BF16)
    xw_b = (xdt * ds_c).astype(BF16)
    lane = lax.broadcasted_iota(jnp.int32, (L, LANES), 1)
    lo = lane < SSM_HEAD_DIM
    nt = (((1,), (1,)), ((), ()))
    heads_per_group = SSM_HEADS // SSM_GROUPS
    ys = []
    for g in range(SSM_GROUPS):
        bm = xc[:, di + g * ns:di + (g + 1) * ns]
        cm = xc[:, di + SSM_GROUPS * ns + g * ns:di + SSM_GROUPS * ns + (g + 1) * ns]
        bm_b = bm.astype(BF16)
        cm_b = cm.astype(BF16)
        bm_t = bm.T.astype(BF16)
        cb = lax.dot_general(cm_b, bm_b, nt, preferred_element_type=F32)
        for jp in range(heads_per_group // 2):
            j = g * (heads_per_group // 2) + jp
            cols = slice(j * LANES, (j + 1) * LANES)
            ms = []
            for hh in (2 * j, 2 * j + 1):
                seg = acum[:, hh:hh + 1] - acum_t[hh:hh + 1, :]
                decay = jnp.exp(jnp.where(causal, seg, -jnp.inf))
                ms.append((cb * decay).astype(BF16))
            mcat = jnp.concatenate(ms, axis=1)
            xp = xdt_b[:, cols]
            zero = jnp.zeros_like(xp)
            xcat = jnp.concatenate([jnp.where(lo, xp, zero), jnp.where(lo, zero, xp)], axis=0)
            y_diag = jnp.dot(mcat, xcat, preferred_element_type=F32)
            ent = st_scr[j]
            y_off = jnp.dot(cm_b, ent.astype(BF16), preferred_element_type=F32) * ea_c[:, cols]
            st_new = jnp.dot(bm_t, xw_b[:, cols], preferred_element_type=F32)
            st_scr[j] = st_new + ea_c[L - 1:L, cols] * ent
            ys.append(y_diag + y_off)
    y = jnp.concatenate(ys, axis=1) + dskip_ref[...] * xs
    zf = z_ref[...].astype(F32)
    y = y * (zf * jax.nn.sigmoid(zf))
    gw = di // SSM_GROUPS
    outs = []
    for g in range(SSM_GROUPS):
        yg = y[:, g * gw:(g + 1) * gw]
        outs.append(yg * lax.rsqrt(jnp.mean(yg * yg, axis=-1, keepdims=True) + RMS_EPS))
    o_ref[...] = (jnp.concatenate(outs, axis=1) * ng_ref[...]).astype(o_ref.dtype)


def _ssd(proj, cw, cb, dt_bias, a_log, d_skip, norm_g, batch, seq):
    n = proj.shape[0]
    L = SSM_CHUNK
    nc = seq // L
    di = SSM_D_INNER
    pad_h = LANES - SSM_HEADS
    dtb = jnp.pad(dt_bias, (0, pad_h)).reshape(1, LANES)
    alog = jnp.pad(a_log, (0, pad_h)).reshape(1, LANES)
    dskip = jnp.repeat(d_skip, SSM_HEAD_DIM).reshape(1, di)
    const = lambda b, c: (0, 0)
    return pl.pallas_call(
        _ssd_kernel,
        grid=(batch, nc),
        in_specs=[pl.BlockSpec((L, di), lambda b, c: (b * nc + c, COL_SSM_Z // di)),
                  pl.BlockSpec((L, XBCDT_W), lambda b, c: (b * nc + c, COL_XBCDT // XBCDT_W)),
                  pl.BlockSpec((SSM_CONV, SSM_XBC), const),
                  pl.BlockSpec((1, SSM_XBC), const),
                  pl.BlockSpec((1, LANES), const),
                  pl.BlockSpec((1, LANES), const),
                  pl.BlockSpec((1, di), const),
                  pl.BlockSpec((1, di), const)],
        out_specs=pl.BlockSpec((L, di), lambda b, c: (b * nc + c, 0)),
        out_shape=jax.ShapeDtypeStruct((n, di), BF16),
        scratch_shapes=[pltpu.VMEM((L + SUBLANES, SSM_XBC), F32),
                        pltpu.VMEM((SSM_HEADS // 2, SSM_D_STATE, LANES), F32)],
        compiler_params=_cparams(("parallel", "arbitrary")),
        name="ssd",
    )(proj, proj, cw, cb.reshape(1, SSM_XBC), dtb, alog, dskip, norm_g.reshape(1, di))


def _merge_kernel(g_ref, bg_ref, yl_ref, ys_ref, ya_ref, wl_ref, ws_ref, wa_ref, wo_ref,
                  h_ref, lg_ref, lb_ref, hn_ref, hb_ref):
    d = h_ref.shape[1]
    merged = None
    for i, (y_ref, w_ref) in enumerate(((yl_ref, wl_ref), (ys_ref, ws_ref), (ya_ref, wa_ref))):
        gate = jax.nn.sigmoid(g_ref[:, i * d:(i + 1) * d].astype(F32) + bg_ref[:, i * d:(i + 1) * d])
        term = gate * jnp.dot(y_ref[...], w_ref[...], preferred_element_type=F32)
        merged = term if merged is None else merged + term
    mix = jnp.dot(merged.astype(BF16), wo_ref[...], preferred_element_type=F32)
    hn = _layer_norm(DEEPNORM_ALPHA * h_ref[...] + mix, lg_ref[...], lb_ref[...])
    hn_ref[...] = hn
    hb_ref[...] = hn.astype(BF16)


def _merge_outproj_ln(proj, b_gate, y_lru, y_ssm, y_att, wl, ws, wa, wo, h, ln_g, ln_b):
    n, d = h.shape
    tm = 256
    gw = N_BRANCH * d
    kb = y_lru.shape[1]
    const = lambda i: (0, 0)
    once = pl.Buffered(1)
    return pl.pallas_call(
        _merge_kernel,
        grid=(n // tm,),
        in_specs=[pl.BlockSpec((tm, gw), lambda i: (i, 0)),
                  pl.BlockSpec((1, gw), const),
                  pl.BlockSpec((tm, kb), lambda i: (i, 0)),
                  pl.BlockSpec((tm, kb), lambda i: (i, 0)),
                  pl.BlockSpec((tm, kb), lambda i: (i, 0)),
                  pl.BlockSpec((kb, d), const, pipeline_mode=once),
                  pl.BlockSpec((kb, d), const, pipeline_mode=once),
                  pl.BlockSpec((kb, d), const, pipeline_mode=once),
                  pl.BlockSpec((d, d), const, pipeline_mode=once),
                  pl.BlockSpec((tm, d), lambda i: (i, 0)),
                  pl.BlockSpec((1, d), const),
                  pl.BlockSpec((1, d), const)],
        out_specs=[pl.BlockSpec((tm, d), lambda i: (i, 0)),
                   pl.BlockSpec((tm, d), lambda i: (i, 0))],
        out_shape=[jax.ShapeDtypeStruct((n, d), F32), jax.ShapeDtypeStruct((n, d), BF16)],
        compiler_params=_cparams(("parallel",)),
        name="merge_outproj_ln",
    )(proj, b_gate.reshape(1, gw), y_lru, y_ssm, y_att, wl, ws, wa, wo, h,
      ln_g.reshape(1, d), ln_b.reshape(1, d))


def _seg_reduce(v, lane, op):
    for s in (1, 2, 4):
        up = pltpu.roll(v, LANES - s, axis=1)
        dn = pltpu.roll(v, s, axis=1)
        v = op(v, jnp.where((lane & s) == 0, up, dn))
    return v


def _router_kernel(h_ref, w_ref, b_ref, idx_ref, wt_ref):
    tm = h_ref.shape[0]
    logits = jnp.dot(h_ref[...], w_ref[...], precision=HIGHEST, preferred_element_type=F32)
    scores = jax.nn.sigmoid(logits)
    lane = lax.broadcasted_iota(jnp.int32, (tm, LANES), 1)
    lane_f = lane.astype(F32)
    real = lane < N_EXPERTS
    neg = -jnp.inf
    choice = jnp.where(real, scores + b_ref[...], neg)
    per_group = N_EXPERTS // N_EXPERT_GROUPS
    assert per_group == 8
    m1 = _seg_reduce(choice, lane, jnp.maximum)
    first = _seg_reduce(jnp.where(choice == m1, lane_f, float(LANES)), lane, jnp.minimum)
    m2 = _seg_reduce(jnp.where(lane_f == first, neg, choice), lane, jnp.maximum)
    gs = m1 + m2
    gidx = lane // per_group
    n_slots = LANES // per_group
    beaten = jnp.zeros((tm, LANES), jnp.int32)
    for k in range(1, n_slots):
        other = pltpu.roll(gs, per_group * k, axis=1)
        og = (gidx - k) & (n_slots - 1)
        wins = (other > gs) | ((other == gs) & (og < gidx))
        beaten = beaten + wins.astype(jnp.int32)
    masked = jnp.where((beaten < TOPK_GROUPS) & real, choice, neg)
    sel_i = jnp.zeros((tm, LANES), F32)
    sel_w = jnp.zeros((tm, LANES), F32)
    for k in range(TOP_K):
        m = jnp.max(masked, axis=1, keepdims=True)
        am = jnp.min(jnp.where(masked == m, lane_f, float(LANES)), axis=1, keepdims=True)
        hit = lane_f == am
        wk = jnp.sum(jnp.where(hit, scores, 0.0), axis=1, keepdims=True)
        sel_i = jnp.where(lane == k, am, sel_i)
        sel_w = jnp.where(lane == k, wk, sel_w)
        masked = jnp.where(hit, neg, masked)
    wsum = jnp.sum(sel_w, axis=1, keepdims=True)
    idx_ref[...] = sel_i.astype(jnp.int32)
    wt_ref[...] = sel_w / wsum * ROUTED_SCALE


def _router(h, router_w, router_bias):
    n, d = h.shape
    tm = 512
    pad_e = LANES - N_EXPERTS
    w = jnp.pad(router_w, ((0, 0), (0, pad_e)))
    b = jnp.pad(router_bias, (0, pad_e)).reshape(1, LANES)
    return pl.pallas_call(
        _router_kernel,
        grid=(n // tm,),
        in_specs=[pl.BlockSpec((tm, d), lambda i: (i, 0)),
                  pl.BlockSpec((d, LANES), lambda i: (0, 0)),
                  pl.BlockSpec((1, LANES), lambda i: (0, 0))],
        out_specs=[pl.BlockSpec((tm, LANES), lambda i: (i, 0)),
                   pl.BlockSpec((tm, LANES), lambda i: (i, 0))],
        out_shape=[jax.ShapeDtypeStruct((n, LANES), jnp.int32), jax.ShapeDtypeStruct((n, LANES), F32)],
        compiler_params=_cparams(("parallel",)),
        name="moe_router",
    )(h, w, b)


def _pack_bf16_pair(lo, hi):
    lo_bits = lax.bitcast_convert_type(lo.astype(BF16).astype(F32), jnp.uint32)
    hi_bits = lax.bitcast_convert_type(hi.astype(BF16).astype(F32), jnp.uint32)
    return (hi_bits & jnp.uint32(0xFFFF0000)) | (lo_bits >> 16)


def _unpack_bf16_pair(w):
    lo = lax.bitcast_convert_type(w << 16, F32)
    hi = lax.bitcast_convert_type(w & jnp.uint32(0xFFFF0000), F32)
    return lo, hi


def _expert_kernel(be_ref, bv_ref, src_ref, dst_ref, rw_ref, h_hbm, w1_ref, w3_ref, w2_ref,
                   slots_hbm, xbuf, ybuf, gsem, ssem):
    i = pl.program_id(0)
    rows = xbuf.shape[0]
    half = ybuf.shape[1]

    def gather_copy(r):
        return pltpu.make_async_copy(h_hbm.at[pl.ds(src_ref[0, 0, r], 1), :], xbuf.at[pl.ds(r, 1), :], gsem)

    def scatter_copy(r):
        return pltpu.make_async_copy(ybuf.at[pl.ds(r, 1), :], slots_hbm.at[pl.ds(dst_ref[0, 0, r], 1), :], ssem)

    @pl.when(bv_ref[i] > 0)
    def _():
        def g_start(r, c):
            gather_copy(r).start()
            return c

        def g_wait(r, c):
            gather_copy(r).wait()
            return c

        lax.fori_loop(0, rows, g_start, 0)
        lax.fori_loop(0, rows, g_wait, 0)
        x = xbuf[...].astype(BF16)
        a = jnp.dot(x, w1_ref[...], preferred_element_type=F32)
        b = jnp.dot(x, w3_ref[...], preferred_element_type=F32)
        hb = (a * jax.nn.sigmoid(a) * b).astype(BF16)
        y = jnp.dot(hb, w2_ref[...], preferred_element_type=F32) * rw_ref[...]
        ybuf[...] = _pack_bf16_pair(y[:, :half], y[:, half:])

        def s_start(r, c):
            @pl.when(dst_ref[0, 0, r] >= 0)
            def _():
                scatter_copy(r).start()
            return c

        def s_wait(r, c):
            @pl.when(dst_ref[0, 0, r] >= 0)
            def _():
                scatter_copy(r).wait()
            return c

        lax.fori_loop(0, rows, s_start, 0)
        lax.fori_loop(0, rows, s_wait, 0)


def _routed_experts(h, block_e, block_v, row_src, row_dst, row_w, w1, w3, w2):
    n, d = h.shape
    nblk = block_e.shape[0]
    rows = MOE_ROWS
    f = w1.shape[2]
    half = d // 2
    grid_spec = pltpu.PrefetchScalarGridSpec(
        num_scalar_prefetch=2,
        grid=(nblk,),
        in_specs=[pl.BlockSpec((1, 1, rows), lambda i, be, bv: (i, 0, 0), memory_space=pltpu.SMEM),
                  pl.BlockSpec((1, 1, rows), lambda i, be, bv: (i, 0, 0), memory_space=pltpu.SMEM),
                  pl.BlockSpec((rows, 1), lambda i, be, bv: (i, 0)),
                  pl.BlockSpec(memory_space=pl.ANY),
                  pl.BlockSpec((None, d, f), lambda i, be, bv: (be[i], 0, 0)),
                  pl.BlockSpec((None, d, f), lambda i, be, bv: (be[i], 0, 0)),
                  pl.BlockSpec((None, f, d), lambda i, be, bv: (be[i], 0, 0))],
        out_specs=pl.BlockSpec(memory_space=pl.ANY),
        scratch_shapes=[pltpu.VMEM((rows, d), F32),
                        pltpu.VMEM((rows, half), jnp.uint32),
                        pltpu.SemaphoreType.DMA(()),
                        pltpu.SemaphoreType.DMA(())],
    )
    return pl.pallas_call(
        _expert_kernel,
        grid_spec=grid_spec,
        out_shape=jax.ShapeDtypeStruct((TOP_K * n, half), jnp.uint32),
        compiler_params=_cparams(("arbitrary",)),
        name="routed_experts",
    )(block_e, block_v, row_src.reshape(nblk, 1, rows), row_dst.reshape(nblk, 1, rows),
      row_w.reshape(nblk * rows, 1), h, w1, w3, w2)


def _dispatch_plan(idx, wts, n):
    rows = MOE_ROWS
    e = N_EXPERTS
    a_total = n * TOP_K
    nblk = (a_total + e * (rows - 1)) // rows
    e_flat = idx.reshape(a_total)
    order = jnp.argsort(e_flat).astype(jnp.int32)
    counts = jnp.sum((e_flat[:, None] == jnp.arange(e, dtype=jnp.int32)[None, :]).astype(jnp.int32), axis=0)
    blocks_e = (counts + rows - 1) // rows
    blk_end = jnp.cumsum(blocks_e)
    blk_start = blk_end - blocks_e
    start = jnp.cumsum(counts) - counts
    bi = jnp.arange(nblk, dtype=jnp.int32)
    block_e = jnp.minimum(jnp.searchsorted(blk_end, bi, side="right"), e - 1).astype(jnp.int32)
    block_v = (bi < blk_end[-1]).astype(jnp.int32)
    r = jnp.arange(rows, dtype=jnp.int32)[None, :]
    j = (bi - blk_start[block_e])[:, None] * rows + r
    valid = (j < counts[block_e][:, None]) & (block_v[:, None] > 0)
    a = order[jnp.clip(start[block_e][:, None] + j, 0, a_total - 1)]
    tok = a // TOP_K
    slot = a % TOP_K
    row_src = jnp.where(valid, tok, 0).astype(jnp.int32)
    row_dst = jnp.where(valid, slot * n + tok, -1).astype(jnp.int32)
    row_w = jnp.where(valid, wts.reshape(a_total)[a], 0.0).astype(F32)
    return block_e, block_v, row_src, row_dst, row_w


def _combine_kernel(*refs):
    slot_refs = refs[:TOP_K]
    hb_ref, h_ref, w1_ref, w3_ref, w2_ref, lg_ref, lb_ref, hn_ref, hbn_ref = refs[TOP_K:]
    lo = None
    hi = None
    for s_ref in slot_refs:
        l, u = _unpack_bf16_pair(s_ref[...])
        lo = l if lo is None else lo + l
        hi = u if hi is None else hi + u
    routed = jnp.concatenate([lo, hi], axis=1)
    x = hb_ref[...]
    a = jnp.dot(x, w1_ref[...], preferred_element_type=F32)
    b = jnp.dot(x, w3_ref[...], preferred_element_type=F32)
    shared = jnp.dot((a * jax.nn.sigmoid(a) * b).astype(BF16), w2_ref[...], preferred_element_type=F32)
    hn = _layer_norm(DEEPNORM_ALPHA * h_ref[...] + routed + shared, lg_ref[...], lb_ref[...])
    hn_ref[...] = hn
    hbn_ref[...] = hn.astype(BF16)


def _combine_shared_ln(slots, hb, h, ws1, ws3, ws2, ln_g, ln_b):
    n, d = h.shape
    tm = 256
    half = d // 2
    f = ws1.shape[1]
    nt = n // tm
    const = lambda i: (0, 0)
    slot_specs = [pl.BlockSpec((tm, half), lambda i, k=k: (k * nt + i, 0)) for k in range(TOP_K)]
    return pl.pallas_call(
        _combine_kernel,
        grid=(nt,),
        in_specs=slot_specs + [pl.BlockSpec((tm, d), lambda i: (i, 0)),
                               pl.BlockSpec((tm, d), lambda i: (i, 0)),
                               pl.BlockSpec((d, f), const),
                               pl.BlockSpec((d, f), const),
                               pl.BlockSpec((f, d), const),
                               pl.BlockSpec((1, d), const),
                               pl.BlockSpec((1, d), const)],
        out_specs=[pl.BlockSpec((tm, d), lambda i: (i, 0)),
                   pl.BlockSpec((tm, d), lambda i: (i, 0))],
        out_shape=[jax.ShapeDtypeStruct((n, d), F32), jax.ShapeDtypeStruct((n, d), BF16)],
        compiler_params=_cparams(("parallel",)),
        name="combine_shared_ln",
    )(*([slots] * TOP_K), hb, h, ws1, ws3, ws2, ln_g.reshape(1, d), ln_b.reshape(1, d))


def _split_w_in(w):
    sizes = (N_BRANCH * D_MODEL, D_RNN, D_RNN, SSM_D_INNER, SSM_XBC, SSM_HEADS,
             ATT_Q_HEADS * ATT_HEAD_DIM, ATT_D_OUT, ATT_D_OUT)
    offs = [0]
    for s in sizes:
        offs.append(offs[-1] + s)
    g, lx, lg, sz, xbc, dt, q, k, v = (w[:, offs[i]:offs[i + 1]] for i in range(len(sizes)))
    w_main = jnp.concatenate([g, lx, lg, xbc, dt, jnp.zeros((w.shape[0], DT_PAD), w.dtype), sz], axis=1)
    return w_main.astype(BF16), q.astype(BF16), jnp.concatenate([k, v], axis=1).astype(BF16)


def kernel(x, emb_ln_g, emb_ln_b, w_in, b_gate, lru_conv_w, lru_conv_b, lru_wr, lru_br, lru_wi, lru_bi, lru_lambda, ssm_conv_w, ssm_conv_b, ssm_dt_bias, ssm_a_log, ssm_d, ssm_norm_g, w_proj_lru, w_proj_ssm, w_proj_att, w_out, ln1_g, ln1_b, router_w, router_bias, w1, w3, w2, ws1, ws3, ws2, ln2_g, ln2_b):
    batch, seq, d = x.shape
    n = batch * seq
    slopes = (2.0 ** (-8.0 * jnp.arange(1, ATT_Q_HEADS + 1, dtype=F32) / ATT_Q_HEADS)).reshape(ATT_GROUPS, ATT_KV_HEADS)
    h, hb = _embed_ln(x.reshape(n, d), emb_ln_g, emb_ln_b)
    for l in range(DEPTH):
        w_main, w_q, w_kv = _split_w_in(w_in[l])
        proj = _matmul(hb, w_main, 1024, 1024, "in_proj")
        q, kv = _qkv_proj(hb, w_q, w_kv, batch, seq)
        y_lru = _rglru(proj, lru_conv_w[l], lru_conv_b[l], lru_wr[l].astype(BF16), lru_br[l],
                       lru_wi[l].astype(BF16), lru_bi[l], lru_lambda[l], batch, seq)
        y_ssm = _ssd(proj, ssm_conv_w[l], ssm_conv_b[l], ssm_dt_bias[l], ssm_a_log[l], ssm_d[l],
                     ssm_norm_g[l], batch, seq)
        y_att = _attention(q, kv, slopes, batch, seq)
        h, hb = _merge_outproj_ln(proj, b_gate[l], y_lru, y_ssm, y_att,
                                  w_proj_lru[l].astype(BF16), w_proj_ssm[l].astype(BF16),
                                  w_proj_att[l].astype(BF16), w_out[l].astype(BF16), h, ln1_g[l], ln1_b[l])
        idx, wts = _router(h, router_w[l], router_bias[l])
        plan = _dispatch_plan(idx[:, :TOP_K], wts[:, :TOP_K], n)
        slots = _routed_experts(h, *plan, w1[l].astype(BF16), w3[l].astype(BF16), w2[l].astype(BF16))
        h, hb = _combine_shared_ln(slots, hb, h, ws1[l].astype(BF16), ws3[l].astype(BF16),
                                   ws2[l].astype(BF16), ln2_g[l], ln2_b[l])
    return h.reshape(batch, seq, d)
```

```python
import functools
import math

import jax
import jax.numpy as jnp
from jax import lax
from jax.experimental import pallas as pl
from jax.experimental.pallas import tpu as pltpu

F32 = jnp.float32
BF16 = jnp.bfloat16
HIGHEST = lax.Precision.HIGHEST

D_MODEL = 2048
DEPTH = 2
D_RNN = 1024
LRU_BLOCKS = 8
LRU_CONV = 4
LRU_C = 8.0
SSM_D_INNER = 1024
SSM_HEAD_DIM = 64
SSM_HEADS = SSM_D_INNER // SSM_HEAD_DIM
SSM_GROUPS = 2
SSM_D_STATE = 128
SSM_CONV = 4
SSM_CHUNK = 128
SSM_XBC = SSM_D_INNER + 2 * SSM_GROUPS * SSM_D_STATE
ATT_HEAD_DIM = 128
ATT_KV_HEADS = 8
ATT_PATTERNS = ((128, 1), (512, 4), (2048, 16))
ATT_GROUPS = len(ATT_PATTERNS)
ATT_Q_HEADS = ATT_GROUPS * ATT_KV_HEADS
ATT_BLOCK = 128
ATT_D_OUT = ATT_KV_HEADS * ATT_HEAD_DIM
N_BRANCH = 3
N_EXPERTS = 64
EXPERT_DIM = 512
TOP_K = 8
N_EXPERT_GROUPS = 8
TOPK_GROUPS = 4
ROUTED_SCALE = 2.5
DEEPNORM_ALPHA = (2 * DEPTH) ** 0.25
LN_EPS = 1e-5
RMS_EPS = 1e-6

LANES = 128
SUBLANES = 8
VMEM_LIMIT_BYTES = 56 * 1024 * 1024

DT_PAD = 512 - SSM_HEADS
MAIN_COLS = N_BRANCH * D_MODEL + 2 * D_RNN + SSM_XBC + SSM_HEADS + DT_PAD + SSM_D_INNER
COL_LRU_X = N_BRANCH * D_MODEL
COL_LRU_G = COL_LRU_X + D_RNN
COL_XBCDT = COL_LRU_G + D_RNN
XBCDT_W = SSM_XBC + SSM_HEADS + DT_PAD
COL_SSM_Z = COL_XBCDT + XBCDT_W

MOE_ROWS = 256


def _cparams(sem):
    return pltpu.CompilerParams(dimension_semantics=sem, vmem_limit_bytes=VMEM_LIMIT_BYTES)


def _layer_norm(x, g, b):
    mu = jnp.mean(x, axis=-1, keepdims=True)
    xc = x - mu
    var = jnp.mean(xc * xc, axis=-1, keepdims=True)
    return xc * lax.rsqrt(var + LN_EPS) * g + b


def _store_token_major(ref, val):
    rows, w = val.shape
    k = w // LANES
    for j in range(k):
        ref[pl.ds(j, rows, stride=k), :] = val[:, j * LANES:(j + 1) * LANES]


def _load_token_major(ref, rows, k):
    return jnp.concatenate([ref[pl.ds(j, rows, stride=k), :] for j in range(k)], axis=1)


def _ln_kernel(x_ref, g_ref, b_ref, h_ref, hb_ref):
    y = _layer_norm(x_ref[...], g_ref[...], b_ref[...])
    h_ref[...] = y
    hb_ref[...] = y.astype(BF16)


def _embed_ln(x2d, g, b):
    n, d = x2d.shape
    tm = 512
    return pl.pallas_call(
        _ln_kernel,
        grid=(n // tm,),
        in_specs=[pl.BlockSpec((tm, d), lambda i: (i, 0)),
                  pl.BlockSpec((1, d), lambda i: (0, 0)),
                  pl.BlockSpec((1, d), lambda i: (0, 0))],
        out_specs=[pl.BlockSpec((tm, d), lambda i: (i, 0)),
                   pl.BlockSpec((tm, d), lambda i: (i, 0))],
        out_shape=[jax.ShapeDtypeStruct((n, d), F32), jax.ShapeDtypeStruct((n, d), BF16)],
        compiler_params=_cparams(("parallel",)),
        name="embed_ln",
    )(x2d, g.reshape(1, d), b.reshape(1, d))


def _mm_kernel(a_ref, b_ref, o_ref):
    o_ref[...] = jnp.dot(a_ref[...], b_ref[...], preferred_element_type=F32).astype(o_ref.dtype)


def _matmul(a, b, tm, tn, name):
    m, k = a.shape
    _, n = b.shape
    return pl.pallas_call(
        _mm_kernel,
        grid=(m // tm, n // tn),
        in_specs=[pl.BlockSpec((tm, k), lambda i, j: (i, 0)),
                  pl.BlockSpec((k, tn), lambda i, j: (0, j))],
        out_specs=pl.BlockSpec((tm, tn), lambda i, j: (i, j)),
        out_shape=jax.ShapeDtypeStruct((m, n), BF16),
        compiler_params=_cparams(("parallel", "parallel")),
        name=name,
    )(a, b)


def _store_dilated(acc_ref, dst_ref, d):
    nc, t, _ = acc_ref.shape
    u = t // d
    for c in range(nc):
        cols = slice(c * LANES, (c + 1) * LANES)
        if d == 1:
            dst_ref[:, cols] = acc_ref[c].astype(dst_ref.dtype)
            continue
        for r in range(d):
            dst_ref[r * u:(r + 1) * u, cols] = acc_ref[c, pl.ds(r, u, stride=d), :].astype(dst_ref.dtype)


def _dot_to_lane_tiles(a_ref, w_ref, acc_ref):
    res = jnp.dot(a_ref[...], w_ref[...], preferred_element_type=F32)
    for c in range(acc_ref.shape[0]):
        acc_ref[c] = res[:, c * LANES:(c + 1) * LANES]


def _q_proj_kernel(a_ref, w_ref, o_ref, acc_ref, *, tiles_per_group):
    j = pl.program_id(1)
    _dot_to_lane_tiles(a_ref, w_ref, acc_ref)
    for gi, (_, d) in enumerate(ATT_PATTERNS):
        @pl.when(j // tiles_per_group == gi)
        def _(d=d):
            _store_dilated(acc_ref, o_ref, d)


def _kv_proj_kernel(a_ref, w_ref, o_ref, acc_ref):
    _dot_to_lane_tiles(a_ref, w_ref, acc_ref)
    for gi, (_, d) in enumerate(ATT_PATTERNS):
        _store_dilated(acc_ref, o_ref.at[gi], d)


def _qkv_proj(hb, w_q, w_kv, batch, seq):
    n, k = hb.shape
    tn = 512
    nq = w_q.shape[1]
    q = pl.pallas_call(
        functools.partial(_q_proj_kernel, tiles_per_group=(nq // ATT_GROUPS) // tn),
        grid=(batch, nq // tn),
        in_specs=[pl.BlockSpec((seq, k), lambda b, j: (b, 0)),
                  pl.BlockSpec((k, tn), lambda b, j: (0, j))],
        out_specs=pl.BlockSpec((seq, tn), lambda b, j: (b, j)),
        out_shape=jax.ShapeDtypeStruct((n, nq), BF16),
        scratch_shapes=[pltpu.VMEM((tn // LANES, seq, LANES), F32)],
        compiler_params=_cparams(("parallel", "parallel")),
        name="q_proj",
    )(hb, w_q)
    nkv = w_kv.shape[1]
    kv = pl.pallas_call(
        _kv_proj_kernel,
        grid=(batch, nkv // tn),
        in_specs=[pl.BlockSpec((seq, k), lambda b, j: (b, 0)),
                  pl.BlockSpec((k, tn), lambda b, j: (0, j))],
        out_specs=pl.BlockSpec((ATT_GROUPS, seq, tn), lambda b, j: (0, b, j)),
        out_shape=jax.ShapeDtypeStruct((ATT_GROUPS, n, nkv), BF16),
        scratch_shapes=[pltpu.VMEM((tn // LANES, seq, LANES), F32)],
        compiler_params=_cparams(("parallel", "parallel")),
        name="kv_proj",
    )(hb, w_kv)
    return q, kv


def _attn_kernel(slopes_ref, q0_ref, q1_ref, q2_ref, k0_ref, k1_ref, k2_ref,
                 v0_ref, v1_ref, v2_ref, o_ref, acc_scr, m_scr, l_scr):
    h = pl.program_id(1)
    t = o_ref.shape[0]
    blk = ATT_BLOCK
    scale = ATT_HEAD_DIM ** -0.5
    qi = lax.broadcasted_iota(jnp.int32, (blk, 2 * blk), 0)
    kj = lax.broadcasted_iota(jnp.int32, (blk, 2 * blk), 1)
    dist = blk + qi - kj
    q_refs = (q0_ref, q1_ref, q2_ref)
    k_refs = (k0_ref, k1_ref, k2_ref)
    v_refs = (v0_ref, v1_ref, v2_ref)
    nt = (((1,), (1,)), ((), ()))
    for g, (window, d) in enumerate(ATT_PATTERNS):
        reach = window // d
        assert reach <= blk
        valid = (dist >= 0) & (dist <= reach)
        slope = slopes_ref[g, h]
        bias = jnp.where(valid, -(slope * d) * dist.astype(F32), -jnp.inf)
        bias_cur = bias[:, blk:]
        nb = (t // d) // blk
        q_ref, k_ref, v_ref = q_refs[g], k_refs[g], v_refs[g]
        for f in range(t // blk):
            r, i = divmod(f, nb)
            qb = q_ref[f * blk:(f + 1) * blk, :]
            if i > 0:
                kc = k_ref[(f - 1) * blk:(f + 1) * blk, :]
                vc = v_ref[(f - 1) * blk:(f + 1) * blk, :]
                s = lax.dot_general(qb, kc, nt, preferred_element_type=F32) * scale + bias
            else:
                kc = k_ref[f * blk:(f + 1) * blk, :]
                vc = v_ref[f * blk:(f + 1) * blk, :]
                s = lax.dot_general(qb, kc, nt, preferred_element_type=F32) * scale + bias_cur
            m = jnp.max(s, axis=-1, keepdims=True)
            p = jnp.exp(s - m)
            l = jnp.sum(p, axis=-1, keepdims=True)
            o = jnp.dot(p.astype(BF16), vc, preferred_element_type=F32)
            if d == 1:
                rows = slice(f * blk, (f + 1) * blk)
            else:
                rows = pl.ds(r + d * blk * i, blk, stride=d)
            acc_scr[g, rows, :] = o
            m_scr[g, rows, :] = jnp.broadcast_to(m, (blk, LANES))
            l_scr[g, rows, :] = jnp.broadcast_to(l, (blk, LANES))
    ch = 256
    for c in range(t // ch):
        rows = slice(c * ch, (c + 1) * ch)
        ms = [m_scr[g, rows, :] for g in range(ATT_GROUPS)]
        mx = jnp.maximum(jnp.maximum(ms[0], ms[1]), ms[2])
        num = jnp.zeros((ch, LANES), F32)
        den = jnp.zeros((ch, LANES), F32)
        for g in range(ATT_GROUPS):
            w = jnp.exp(ms[g] - mx)
            num = num + w * acc_scr[g, rows, :]
            den = den + w * l_scr[g, rows, :]
        o_ref[rows, :] = (num / den).astype(o_ref.dtype)


def _attention(q, kv, slopes, batch, seq):
    n = q.shape[0]
    hd = ATT_HEAD_DIM
    nh = ATT_KV_HEADS
    q_specs = [pl.BlockSpec((seq, hd), lambda b, h, g=g: (b, g * nh + h)) for g in range(ATT_GROUPS)]
    k_specs = [pl.BlockSpec((None, seq, hd), lambda b, h, g=g: (g, b, h)) for g in range(ATT_GROUPS)]
    v_specs = [pl.BlockSpec((None, seq, hd), lambda b, h, g=g: (g, b, nh + h)) for g in range(ATT_GROUPS)]
    return pl.pallas_call(
        _attn_kernel,
        grid=(batch, nh),
        in_specs=[pl.BlockSpec(memory_space=pltpu.SMEM)] + q_specs + k_specs + v_specs,
        out_specs=pl.BlockSpec((seq, hd), lambda b, h: (b, h)),
        out_shape=jax.ShapeDtypeStruct((n, nh * hd), BF16),
        scratch_shapes=[pltpu.VMEM((ATT_GROUPS, seq, hd), F32),
                        pltpu.VMEM((ATT_GROUPS, seq, LANES), F32),
                        pltpu.VMEM((ATT_GROUPS, seq, LANES), F32)],
        compiler_params=_cparams(("parallel", "parallel")),
        name="dilated_attention",
    )(slopes, q, q, q, kv, kv, kv, kv, kv, kv)


def _scan8(a, u, carry, row):
    for s in (1, 2, 4):
        a_sh = pltpu.roll(a, s, axis=0)
        u_sh = pltpu.roll(u, s, axis=0)
        m = row >= s
        u = jnp.where(m, a * u_sh + u, u)
        a = jnp.where(m, a * a_sh, a)
    return u + a * carry


def _lru_kernel(x_ref, g_ref, cw_ref, cb_ref, wr_ref, br_ref, wi_ref, bi_ref, lam_ref, o_ref,
                xpad_scr, a_scr, u_scr):
    t, c = o_ref.shape
    nb = LRU_BLOCKS
    bs = c // nb
    ch = 256
    pad = SUBLANES
    xpad_scr[0:pad, :] = jnp.zeros((pad, c), F32)
    for k in range(t // ch):
        xpad_scr[pad + k * ch:pad + (k + 1) * ch, :] = x_ref[k * ch:(k + 1) * ch, :].astype(F32)
    neg_lam = -lam_ref[...]
    sp = jnp.maximum(neg_lam, 0.0) + jnp.log1p(jnp.exp(-jnp.abs(neg_lam)))
    for k in range(t // ch):
        base = k * ch
        xc = cb_ref[...] + sum(
            cw_ref[j:j + 1, :] * xpad_scr[base + pad - (LRU_CONV - 1) + j:base + pad - (LRU_CONV - 1) + j + ch, :]
            for j in range(LRU_CONV))
        xcb = xc.astype(BF16)
        for n in range(nb):
            cols = slice(n * bs, (n + 1) * bs)
            xn = xcb[:, cols]
            r = jax.nn.sigmoid(jnp.dot(xn, wr_ref[n], preferred_element_type=F32) + br_ref[:, cols])
            ig = jax.nn.sigmoid(jnp.dot(xn, wi_ref[n], preferred_element_type=F32) + bi_ref[:, cols])
            log_a = -LRU_C * r * sp[:, cols]
            th = jnp.tanh(log_a)
            a_scr[base:base + ch, cols] = jnp.exp(log_a)
            u_scr[base:base + ch, cols] = jnp.sqrt(-2.0 * th / (1.0 - th)) * ig * xc[:, cols]
    row = lax.broadcasted_iota(jnp.int32, (SUBLANES, c), 0)

    def step(j, carry):
        rows = pl.ds(pl.multiple_of(j * SUBLANES, SUBLANES), SUBLANES)
        h8 = _scan8(a_scr[rows, :], u_scr[rows, :], carry, row)
        u_scr[rows, :] = h8
        return jnp.broadcast_to(h8[SUBLANES - 1:SUBLANES, :], (SUBLANES, c))

    lax.fori_loop(0, t // SUBLANES, step, jnp.zeros((SUBLANES, c), F32))
    for k in range(t // ch):
        rows = slice(k * ch, (k + 1) * ch)
        o_ref[rows, :] = (u_scr[rows, :] * jax.nn.gelu(g_ref[rows, :].astype(F32))).astype(o_ref.dtype)


def _rglru(proj, cw, cb, wr, br, wi, bi, lam, batch, seq):
    n = proj.shape[0]
    c = D_RNN
    bs = c // LRU_BLOCKS
    row = lambda b: (0, 0)
    return pl.pallas_call(
        _lru_kernel,
        grid=(batch,),
        in_specs=[pl.BlockSpec((seq, c), lambda b: (b, COL_LRU_X // c)),
                  pl.BlockSpec((seq, c), lambda b: (b, COL_LRU_G // c)),
                  pl.BlockSpec((LRU_CONV, c), row),
                  pl.BlockSpec((1, c), row),
                  pl.BlockSpec((LRU_BLOCKS, bs, bs), lambda b: (0, 0, 0)),
                  pl.BlockSpec((1, c), row),
                  pl.BlockSpec((LRU_BLOCKS, bs, bs), lambda b: (0, 0, 0)),
                  pl.BlockSpec((1, c), row),
                  pl.BlockSpec((1, c), row)],
        out_specs=pl.BlockSpec((seq, c), lambda b: (b, 0)),
        out_shape=jax.ShapeDtypeStruct((n, c), BF16),
        scratch_shapes=[pltpu.VMEM((seq + SUBLANES, c), F32),
                        pltpu.VMEM((seq, c), F32),
                        pltpu.VMEM((seq, c), F32)],
        compiler_params=_cparams(("parallel",)),
        name="rglru",
    )(proj, proj, cw, cb.reshape(1, c), wr, br.reshape(1, c), wi, bi.reshape(1, c), lam.reshape(1, c))


def _ssd_kernel(z_ref, xd_ref, cw_ref, cb_ref, dtb_ref, alog_ref, dskip_ref, ng_ref, o_ref,
                xpad_scr, st_scr):
    L = o_ref.shape[0]
    di = SSM_D_INNER
    ns = SSM_D_STATE
    pad = SUBLANES
    c = pl.program_id(1)

    @pl.when(c == 0)
    def _():
        xpad_scr[0:pad, :] = jnp.zeros((pad, SSM_XBC), F32)
        st_scr[...] = jnp.zeros(st_scr.shape, F32)

    xpad_scr[pad:pad + L, :] = xd_ref[:, :SSM_XBC].astype(F32)
    xc = cb_ref[...] + sum(
        cw_ref[j:j + 1, :] * xpad_scr[pad - (SSM_CONV - 1) + j:pad - (SSM_CONV - 1) + j + L, :]
        for j in range(SSM_CONV))
    xpad_scr[0:pad, :] = xpad_scr[L:L + pad, :]
    xc = xc * jax.nn.sigmoid(xc)
    xs = xc[:, :di]

    dt_in = xd_ref[:, SSM_XBC:SSM_XBC + LANES].astype(F32) + dtb_ref[...]
    dt = jnp.maximum(dt_in, 0.0) + jnp.log1p(jnp.exp(-jnp.abs(dt_in)))
    adt = dt * (-jnp.exp(alog_ref[...]))
    ri = lax.broadcasted_iota(jnp.int32, (L, L), 0)
    ci = lax.broadcasted_iota(jnp.int32, (L, L), 1)
    causal = ri >= ci
    acum = jnp.dot(causal.astype(F32), adt, precision=HIGHEST, preferred_element_type=F32)
    acum_t = acum.T
    a_last = acum[L - 1:L, :]
    hl = lax.broadcasted_iota(jnp.int32, (LANES, di), 0)
    cl = lax.broadcasted_iota(jnp.int32, (LANES, di), 1)
    expand = (cl // SSM_HEAD_DIM == hl).astype(F32)
    dt_c = jnp.dot(dt, expand, precision=HIGHEST, preferred_element_type=F32)
    ea_c = jnp.dot(jnp.exp(acum), expand, precision=HIGHEST, preferred_element_type=F32)
    ds_c = jnp.dot(jnp.exp(a_last - acum), expand, precision=HIGHEST, preferred_element_type=F32)
    xdt = xs * dt_c
    xdt_b = xdt.astype(BF16)
    xw_b = (xdt * ds_c).astype(BF16)
    lane = lax.broadcasted_iota(jnp.int32, (L, LANES), 1)
    lo = lane < SSM_HEAD_DIM
    nt = (((1,), (1,)), ((), ()))
    heads_per_group = SSM_HEADS // SSM_GROUPS
    ys = []
    for g in range(SSM_GROUPS):
        bm = xc[:, di + g * ns:di + (g + 1) * ns]
        cm = xc[:, di + SSM_GROUPS * ns + g * ns:di + SSM_GROUPS * ns + (g + 1) * ns]
        bm_b = bm.astype(BF16)
        cm_b = cm.astype(BF16)
        bm_t = bm.T.astype(BF16)
        cb = lax.dot_general(cm_b, bm_b, nt, preferred_element_type=F32)
        for jp in range(heads_per_group // 2):
            j = g * (heads_per_group // 2) + jp
            cols = slice(j * LANES, (j + 1) * LANES)
            ms = []
            for hh in (2 * j, 2 * j + 1):
                seg = acum[:, hh:hh + 1] - acum_t[hh:hh + 1, :]
                decay = jnp.exp(jnp.where(causal, seg, -jnp.inf))
                ms.append((cb * decay).astype(BF16))
            mcat = jnp.concatenate(ms, axis=1)
            xp = xdt_b[:, cols]
            zero = jnp.zeros_like(xp)
            xcat = jnp.concatenate([jnp.where(lo, xp, zero), jnp.where(lo, zero, xp)], axis=0)
            y_diag = jnp.dot(mcat, xcat, preferred_element_type=F32)
            ent = st_scr[j]
            y_off = jnp.dot(cm_b, ent.astype(BF16), preferred_element_type=F32) * ea_c[:, cols]
            st_new = jnp.dot(bm_t, xw_b[:, cols], preferred_element_type=F32)
            st_scr[j] = st_new + ea_c[L - 1:L, cols] * ent
            ys.append(y_diag + y_off)
    y = jnp.concatenate(ys, axis=1) + dskip_ref[...] * xs
    zf = z_ref[...].astype(F32)
    y = y * (zf * jax.nn.sigmoid(zf))
    gw = di // SSM_GROUPS
    outs = []
    for g in range(SSM_GROUPS):
        yg = y[:, g * gw:(g + 1) * gw]
        outs.append(yg * lax.rsqrt(jnp.mean(yg * yg, axis=-1, keepdims=True) + RMS_EPS))
    o_ref[...] = (jnp.concatenate(outs, axis=1) * ng_ref[...]).astype(o_ref.dtype)


def _ssd(proj, cw, cb, dt_bias, a_log, d_skip, norm_g, batch, seq):
    n = proj.shape[0]
    L = SSM_CHUNK
    nc = seq // L
    di = SSM_D_INNER
    pad_h = LANES - SSM_HEADS
    dtb = jnp.pad(dt_bias, (0, pad_h)).reshape(1, LANES)
    alog = jnp.pad(a_log, (0, pad_h)).reshape(1, LANES)
    dskip = jnp.repeat(d_skip, SSM_HEAD_DIM).reshape(1, di)
    const = lambda b, c: (0, 0)
    return pl.pallas_call(
        _ssd_kernel,
        grid=(batch, nc),
        in_specs=[pl.BlockSpec((L, di), lambda b, c: (b * nc + c, COL_SSM_Z // di)),
                  pl.BlockSpec((L, XBCDT_W), lambda b, c: (b * nc + c, COL_XBCDT // XBCDT_W)),
                  pl.BlockSpec((SSM_CONV, SSM_XBC), const),
                  pl.BlockSpec((1, SSM_XBC), const),
                  pl.BlockSpec((1, LANES), const),
                  pl.BlockSpec((1, LANES), const),
                  pl.BlockSpec((1, di), const),
                  pl.BlockSpec((1, di), const)],
        out_specs=pl.BlockSpec((L, di), lambda b, c: (b * nc + c, 0)),
        out_shape=jax.ShapeDtypeStruct((n, di), BF16),
        scratch_shapes=[pltpu.VMEM((L + SUBLANES, SSM_XBC), F32),
                        pltpu.VMEM((SSM_HEADS // 2, SSM_D_STATE, LANES), F32)],
        compiler_params=_cparams(("parallel", "arbitrary")),
        name="ssd",
    )(proj, proj, cw, cb.reshape(1, SSM_XBC), dtb, alog, dskip, norm_g.reshape(1, di))


def _merge_kernel(g_ref, bg_ref, yl_ref, ys_ref, ya_ref, wl_ref, ws_ref, wa_ref, wo_ref,
                  h_ref, lg_ref, lb_ref, hn_ref, hb_ref, hrows_ref):
    d = h_ref.shape[1]
    merged = None
    for i, (y_ref, w_ref) in enumerate(((yl_ref, wl_ref), (ys_ref, ws_ref), (ya_ref, wa_ref))):
        gate = jax.nn.sigmoid(g_ref[:, i * d:(i + 1) * d].astype(F32) + bg_ref[:, i * d:(i + 1) * d])
        term = gate * jnp.dot(y_ref[...], w_ref[...], preferred_element_type=F32)
        merged = term if merged is None else merged + term
    mix = jnp.dot(merged.astype(BF16), wo_ref[...], preferred_element_type=F32)
    hn = _layer_norm(DEEPNORM_ALPHA * h_ref[...] + mix, lg_ref[...], lb_ref[...])
    hn_ref[...] = hn
    hb_ref[...] = hn.astype(BF16)
    _store_token_major(hrows_ref, hn)


def _merge_outproj_ln(proj, b_gate, y_lru, y_ssm, y_att, wl, ws, wa, wo, h, ln_g, ln_b):
    n, d = h.shape
    tm = 256
    gw = N_BRANCH * d
    kb = y_lru.shape[1]
    const = lambda i: (0, 0)
    once = pl.Buffered(1)
    return pl.pallas_call(
        _merge_kernel,
        grid=(n // tm,),
        in_specs=[pl.BlockSpec((tm, gw), lambda i: (i, 0)),
                  pl.BlockSpec((1, gw), const),
                  pl.BlockSpec((tm, kb), lambda i: (i, 0)),
                  pl.BlockSpec((tm, kb), lambda i: (i, 0)),
                  pl.BlockSpec((tm, kb), lambda i: (i, 0)),
                  pl.BlockSpec((kb, d), const, pipeline_mode=once),
                  pl.BlockSpec((kb, d), const, pipeline_mode=once),
                  pl.BlockSpec((kb, d), const, pipeline_mode=once),
                  pl.BlockSpec((d, d), const, pipeline_mode=once),
                  pl.BlockSpec((tm, d), lambda i: (i, 0)),
                  pl.BlockSpec((1, d), const),
                  pl.BlockSpec((1, d), const)],
        out_specs=[pl.BlockSpec((tm, d), lambda i: (i, 0)),
                   pl.BlockSpec((tm, d), lambda i: (i, 0)),
                   pl.BlockSpec((tm * (d // LANES), LANES), lambda i: (i, 0))],
        out_shape=[jax.ShapeDtypeStruct((n, d), F32), jax.ShapeDtypeStruct((n, d), BF16),
                   jax.ShapeDtypeStruct((n * (d // LANES), LANES), F32)],
        compiler_params=_cparams(("parallel",)),
        name="merge_outproj_ln",
    )(proj, b_gate.reshape(1, gw), y_lru, y_ssm, y_att, wl, ws, wa, wo, h,
      ln_g.reshape(1, d), ln_b.reshape(1, d))


def _seg_reduce(v, lane, op):
    for s in (1, 2, 4):
        up = pltpu.roll(v, LANES - s, axis=1)
        dn = pltpu.roll(v, s, axis=1)
        v = op(v, jnp.where((lane & s) == 0, up, dn))
    return v


def _router_kernel(h_ref, w_ref, b_ref, idx_ref, wt_ref):
    tm = h_ref.shape[0]
    logits = jnp.dot(h_ref[...], w_ref[...], precision=HIGHEST, preferred_element_type=F32)
    scores = jax.nn.sigmoid(logits)
    lane = lax.broadcasted_iota(jnp.int32, (tm, LANES), 1)
    lane_f = lane.astype(F32)
    real = lane < N_EXPERTS
    neg = -jnp.inf
    choice = jnp.where(real, scores + b_ref[...], neg)
    per_group = N_EXPERTS // N_EXPERT_GROUPS
    assert per_group == 8
    m1 = _seg_reduce(choice, lane, jnp.maximum)
    first = _seg_reduce(jnp.where(choice == m1, lane_f, float(LANES)), lane, jnp.minimum)
    m2 = _seg_reduce(jnp.where(lane_f == first, neg, choice), lane, jnp.maximum)
    gs = m1 + m2
    gidx = lane // per_group
    n_slots = LANES // per_group
    beaten = jnp.zeros((tm, LANES), jnp.int32)
    for k in range(1, n_slots):
        other = pltpu.roll(gs, per_group * k, axis=1)
        og = (gidx - k) & (n_slots - 1)
        wins = (other > gs) | ((other == gs) & (og < gidx))
        beaten = beaten + wins.astype(jnp.int32)
    masked = jnp.where((beaten < TOPK_GROUPS) & real, choice, neg)
    sel_i = jnp.zeros((tm, LANES), F32)
    sel_w = jnp.zeros((tm, LANES), F32)
    for k in range(TOP_K):
        m = jnp.max(masked, axis=1, keepdims=True)
        am = jnp.min(jnp.where(masked == m, lane_f, float(LANES)), axis=1, keepdims=True)
        hit = lane_f == am
        wk = jnp.sum(jnp.where(hit, scores, 0.0), axis=1, keepdims=True)
        sel_i = jnp.where(lane == k, am, sel_i)
        sel_w = jnp.where(lane == k, wk, sel_w)
        masked = jnp.where(hit, neg, masked)
    wsum = jnp.sum(sel_w, axis=1, keepdims=True)
    idx_ref[...] = sel_i.astype(jnp.int32)
    wt_ref[...] = sel_w / wsum * ROUTED_SCALE


def _router(h, router_w, router_bias):
    n, d = h.shape
    tm = 512
    pad_e = LANES - N_EXPERTS
    w = jnp.pad(router_w, ((0, 0), (0, pad_e)))
    b = jnp.pad(router_bias, (0, pad_e)).reshape(1, LANES)
    return pl.pallas_call(
        _router_kernel,
        grid=(n // tm,),
        in_specs=[pl.BlockSpec((tm, d), lambda i: (i, 0)),
                  pl.BlockSpec((d, LANES), lambda i: (0, 0)),
                  pl.BlockSpec((1, LANES), lambda i: (0, 0))],
        out_specs=[pl.BlockSpec((tm, LANES), lambda i: (i, 0)),
                   pl.BlockSpec((tm, LANES), lambda i: (i, 0))],
        out_shape=[jax.ShapeDtypeStruct((n, LANES), jnp.int32), jax.ShapeDtypeStruct((n, LANES), F32)],
        compiler_params=_cparams(("parallel",)),
        name="moe_router",
    )(h, w, b)


def _pack_bf16_pair(lo, hi):
    lo_bits = lax.bitcast_convert_type(lo.astype(BF16).astype(F32), jnp.uint32)
    hi_bits = lax.bitcast_convert_type(hi.astype(BF16).astype(F32), jnp.uint32)
    return (hi_bits & jnp.uint32(0xFFFF0000)) | (lo_bits >> 16)


def _unpack_bf16_pair(w):
    lo = lax.bitcast_convert_type(w << 16, F32)
    hi = lax.bitcast_convert_type(w & jnp.uint32(0xFFFF0000), F32)
    return lo, hi


def _expert_kernel(be_ref, nv_ref, src_ref, nsrc_ref, dst_ref, rw_ref, h_hbm, w1_ref, w3_ref, w2_ref,
                   slots_hbm, xbuf, ybuf, gsem, ssem):
    i = pl.program_id(0)
    n_valid = nv_ref[0]
    rows = rw_ref.shape[0]
    kx = xbuf.shape[1] // rows
    ky = ybuf.shape[1] // rows
    n_real = slots_hbm.shape[0] - 2 * rows * ky
    slot = lax.rem(i, 2)
    unroll = 8

    def start_gathers(idx_ref, s):
        def body(g, c):
            for u in range(unroll):
                r = g * unroll + u
                tok = idx_ref[0, 0, r]
                pltpu.make_async_copy(h_hbm.at[pl.ds(pl.multiple_of(tok * kx, kx), kx), :],
                                      xbuf.at[s, pl.ds(pl.multiple_of(r * kx, kx), kx), :], gsem.at[s]).start()
            return c
        lax.fori_loop(0, rows // unroll, body, 0)

    def wait_gathers(s):
        pltpu.make_async_copy(h_hbm.at[pl.ds(0, rows * kx), :], xbuf.at[s], gsem.at[s]).wait()

    def start_scatters(s):
        def body(g, c):
            for u in range(unroll):
                r = g * unroll + u
                dst = dst_ref[0, 0, r]
                pltpu.make_async_copy(ybuf.at[s, pl.ds(pl.multiple_of(r * ky, ky), ky), :],
                                      slots_hbm.at[pl.ds(pl.multiple_of(dst * ky, ky), ky), :], ssem.at[s]).start()
            return c
        lax.fori_loop(0, rows // unroll, body, 0)

    def wait_scatters(s):
        pltpu.make_async_copy(ybuf.at[s], slots_hbm.at[pl.ds(0, rows * ky), :], ssem.at[s]).wait()

    @pl.when(i == 0)
    def _():
        ybuf[0] = jnp.zeros((rows * ky, LANES), jnp.uint32)
        for p in range(2):
            pltpu.make_async_copy(ybuf.at[0], slots_hbm.at[pl.ds(n_real + p * rows * ky, rows * ky), :],
                                  ssem.at[0]).start()
        for p in range(2):
            pltpu.make_async_copy(ybuf.at[0], slots_hbm.at[pl.ds(n_real + p * rows * ky, rows * ky), :],
                                  ssem.at[0]).wait()
        start_gathers(src_ref, 0)

    @pl.when(i < n_valid)
    def _():
        @pl.when(i + 1 < n_valid)
        def _():
            start_gathers(nsrc_ref, 1 - slot)

        wait_gathers(slot)
        x = _load_token_major(xbuf.at[slot], rows, kx).astype(BF16)
        a = jnp.dot(x, w1_ref[...], preferred_element_type=F32)
        b = jnp.dot(x, w3_ref[...], preferred_element_type=F32)
        hb = (a * jax.nn.sigmoid(a) * b).astype(BF16)
        y = jnp.dot(hb, w2_ref[...], preferred_element_type=F32) * rw_ref[...]
        half = y.shape[1] // 2
        _store_token_major(ybuf.at[slot], _pack_bf16_pair(y[:, :half], y[:, half:]))

        @pl.when(i >= 1)
        def _():
            wait_scatters(1 - slot)

        start_scatters(slot)

        @pl.when(i + 1 == n_valid)
        def _():
            wait_scatters(slot)


def _routed_experts(h_rows, block_e, n_valid, row_src, row_dst, row_w, w1, w3, w2):
    d = w1.shape[1]
    kx = d // LANES
    ky = d // 2 // LANES
    n = h_rows.shape[0] // kx
    nblk = block_e.shape[0]
    rows = MOE_ROWS
    f = w1.shape[2]
    idx_block = (1, 1, rows)
    grid_spec = pltpu.PrefetchScalarGridSpec(
        num_scalar_prefetch=2,
        grid=(nblk,),
        in_specs=[pl.BlockSpec(idx_block, lambda i, be, nv: (i, 0, 0), memory_space=pltpu.SMEM),
                  pl.BlockSpec(idx_block, lambda i, be, nv: (jnp.minimum(i + 1, nblk - 1), 0, 0),
                               memory_space=pltpu.SMEM),
                  pl.BlockSpec(idx_block, lambda i, be, nv: (i, 0, 0), memory_space=pltpu.SMEM),
                  pl.BlockSpec((rows, 1), lambda i, be, nv: (i, 0)),
                  pl.BlockSpec(memory_space=pl.ANY),
                  pl.BlockSpec((None, d, f), lambda i, be, nv: (be[i], 0, 0)),
                  pl.BlockSpec((None, d, f), lambda i, be, nv: (be[i], 0, 0)),
                  pl.BlockSpec((None, f, d), lambda i, be, nv: (be[i], 0, 0))],
        out_specs=pl.BlockSpec(memory_space=pl.ANY),
        scratch_shapes=[pltpu.VMEM((2, rows * kx, LANES), F32),
                        pltpu.VMEM((2, rows * ky, LANES), jnp.uint32),
                        pltpu.SemaphoreType.DMA((2,)),
                        pltpu.SemaphoreType.DMA((2,))],
    )
    src3 = row_src.reshape(nblk, 1, rows)
    return pl.pallas_call(
        _expert_kernel,
        grid_spec=grid_spec,
        out_shape=jax.ShapeDtypeStruct(((TOP_K * n + 2 * rows) * ky, LANES), jnp.uint32),
        compiler_params=_cparams(("arbitrary",)),
        name="routed_experts",
    )(block_e, n_valid, src3, src3, row_dst.reshape(nblk, 1, rows), row_w.reshape(nblk * rows, 1),
      h_rows, w1, w3, w2)


def _dispatch_plan(idx, wts, n):
    rows = MOE_ROWS
    e = N_EXPERTS
    a_total = n * TOP_K
    nblk = (a_total + e * (rows - 1)) // rows
    e_flat = idx.reshape(a_total)
    order = jnp.argsort(e_flat).astype(jnp.int32)
    counts = jnp.sum((e_flat[:, None] == jnp.arange(e, dtype=jnp.int32)[None, :]).astype(jnp.int32), axis=0)
    blocks_e = (counts + rows - 1) // rows
    blk_end = jnp.cumsum(blocks_e)
    blk_start = blk_end - blocks_e
    start = jnp.cumsum(counts) - counts
    bi = jnp.arange(nblk, dtype=jnp.int32)
    block_e = jnp.minimum(jnp.sum((bi[:, None] >= blk_end[None, :]).astype(jnp.int32), axis=1), e - 1)
    n_valid = blk_end[-1:].astype(jnp.int32)
    r = jnp.arange(rows, dtype=jnp.int32)[None, :]
    j = (bi - blk_start[block_e])[:, None] * rows + r
    valid = (j < counts[block_e][:, None]) & (bi < n_valid[0])[:, None]
    a = order[jnp.clip(start[block_e][:, None] + j, 0, a_total - 1)]
    tok = a // TOP_K
    slot = a % TOP_K
    row_src = jnp.where(valid, tok, 0).astype(jnp.int32)
    dump = TOP_K * n + (bi % 2)[:, None] * rows + r
    row_dst = jnp.where(valid, slot * n + tok, dump).astype(jnp.int32)
    row_w = jnp.where(valid, wts.reshape(a_total)[a], 0.0).astype(F32)
    return block_e, n_valid, row_src, row_dst, row_w


def _combine_kernel(*refs):
    slot_refs = refs[:TOP_K]
    hb_ref, h_ref, w1_ref, w3_ref, w2_ref, lg_ref, lb_ref, hn_ref, hbn_ref = refs[TOP_K:]
    tm, d = h_ref.shape
    ky = d // 2 // LANES
    lo = None
    hi = None
    for s_ref in slot_refs:
        l, u = _unpack_bf16_pair(_load_token_major(s_ref, tm, ky))
        lo = l if lo is None else lo + l
        hi = u if hi is None else hi + u
    routed = jnp.concatenate([lo, hi], axis=1)
    x = hb_ref[...]
    a = jnp.dot(x, w1_ref[...], preferred_element_type=F32)
    b = jnp.dot(x, w3_ref[...], preferred_element_type=F32)
    shared = jnp.dot((a * jax.nn.sigmoid(a) * b).astype(BF16), w2_ref[...], preferred_element_type=F32)
    hn = _layer_norm(DEEPNORM_ALPHA * h_ref[...] + routed + shared, lg_ref[...], lb_ref[...])
    hn_ref[...] = hn
    hbn_ref[...] = hn.astype(BF16)


def _combine_shared_ln(slots, hb, h, ws1, ws3, ws2, ln_g, ln_b):
    n, d = h.shape
    tm = 256
    ky = d // 2 // LANES
    f = ws1.shape[1]
    nt = n // tm
    const = lambda i: (0, 0)
    slot_specs = [pl.BlockSpec((tm * ky, LANES), lambda i, k=k: (k * nt + i, 0)) for k in range(TOP_K)]
    return pl.pallas_call(
        _combine_kernel,
        grid=(nt,),
        in_specs=slot_specs + [pl.BlockSpec((tm, d), lambda i: (i, 0)),
                               pl.BlockSpec((tm, d), lambda i: (i, 0)),
                               pl.BlockSpec((d, f), const),
                               pl.BlockSpec((d, f), const),
                               pl.BlockSpec((f, d), const),
                               pl.BlockSpec((1, d), const),
                               pl.BlockSpec((1, d), const)],
        out_specs=[pl.BlockSpec((tm, d), lambda i: (i, 0)),
                   pl.BlockSpec((tm, d), lambda i: (i, 0))],
        out_shape=[jax.ShapeDtypeStruct((n, d), F32), jax.ShapeDtypeStruct((n, d), BF16)],
        compiler_params=_cparams(("parallel",)),
        name="combine_shared_ln",
    )(*([slots] * TOP_K), hb, h, ws1, ws3, ws2, ln_g.reshape(1, d), ln_b.reshape(1, d))


def _split_w_in(w):
    sizes = (N_BRANCH * D_MODEL, D_RNN, D_RNN, SSM_D_INNER, SSM_XBC, SSM_HEADS,
             ATT_Q_HEADS * ATT_HEAD_DIM, ATT_D_OUT, ATT_D_OUT)
    offs = [0]
    for s in sizes:
        offs.append(offs[-1] + s)
    g, lx, lg, sz, xbc, dt, q, k, v = (w[:, offs[i]:offs[i + 1]] for i in range(len(sizes)))
    w_main = jnp.concatenate([g, lx, lg, xbc, dt, jnp.zeros((w.shape[0], DT_PAD), w.dtype), sz], axis=1)
    return w_main.astype(BF16), q.astype(BF16), jnp.concatenate([k, v], axis=1).astype(BF16)


def kernel(x, emb_ln_g, emb_ln_b, w_in, b_gate, lru_conv_w, lru_conv_b, lru_wr, lru_br, lru_wi, lru_bi, lru_lambda, ssm_conv_w, ssm_conv_b, ssm_dt_bias, ssm_a_log, ssm_d, ssm_norm_g, w_proj_lru, w_proj_ssm, w_proj_att, w_out, ln1_g, ln1_b, router_w, router_bias, w1, w3, w2, ws1, ws3, ws2, ln2_g, ln2_b):
    batch, seq, d = x.shape
    n = batch * seq
    slopes = (2.0 ** (-8.0 * jnp.arange(1, ATT_Q_HEADS + 1, dtype=F32) / ATT_Q_HEADS)).reshape(ATT_GROUPS, ATT_KV_HEADS)
    h, hb = _embed_ln(x.reshape(n, d), emb_ln_g, emb_ln_b)
    for l in range(DEPTH):
        w_main, w_q, w_kv = _split_w_in(w_in[l])
        proj = _matmul(hb, w_main, 1024, 1024, "in_proj")
        q, kv = _qkv_proj(hb, w_q, w_kv, batch, seq)
        y_lru = _rglru(proj, lru_conv_w[l], lru_conv_b[l], lru_wr[l].astype(BF16), lru_br[l],
                       lru_wi[l].astype(BF16), lru_bi[l], lru_lambda[l], batch, seq)
        y_ssm = _ssd(proj, ssm_conv_w[l], ssm_conv_b[l], ssm_dt_bias[l], ssm_a_log[l], ssm_d[l],
                     ssm_norm_g[l], batch, seq)
        y_att = _attention(q, kv, slopes, batch, seq)
        h, hb, h_rows = _merge_outproj_ln(proj, b_gate[l], y_lru, y_ssm, y_att,
                                          w_proj_lru[l].astype(BF16), w_proj_ssm[l].astype(BF16),
                                          w_proj_att[l].astype(BF16), w_out[l].astype(BF16), h, ln1_g[l], ln1_b[l])
        idx, wts = _router(h, router_w[l], router_bias[l])
        plan = _dispatch_plan(idx[:, :TOP_K], wts[:, :TOP_K], n)
        slots = _routed_experts(h_rows, *plan, w1[l].astype(BF16), w3[l].astype(BF16), w2[l].astype(BF16))
        h, hb = _combine_shared_ln(slots, hb, h, ws1[l].astype(BF16), ws3[l].astype(BF16),
                                   ws2[l].astype(BF16), ln2_g[l], ln2_b[l])
    return h.reshape(batch, seq, d)
```

```python
import functools
import math

import jax
import jax.numpy as jnp
from jax import lax
from jax.experimental import pallas as pl
from jax.experimental.pallas import tpu as pltpu

F32 = jnp.float32
BF16 = jnp.bfloat16
HIGHEST = lax.Precision.HIGHEST

D_MODEL = 2048
DEPTH = 2
D_RNN = 1024
LRU_BLOCKS = 8
LRU_CONV = 4
LRU_C = 8.0
SSM_D_INNER = 1024
SSM_HEAD_DIM = 64
SSM_HEADS = SSM_D_INNER // SSM_HEAD_DIM
SSM_GROUPS = 2
SSM_D_STATE = 128
SSM_CONV = 4
SSM_CHUNK = 128
SSM_XBC = SSM_D_INNER + 2 * SSM_GROUPS * SSM_D_STATE
ATT_HEAD_DIM = 128
ATT_KV_HEADS = 8
ATT_PATTERNS = ((128, 1), (512, 4), (2048, 16))
ATT_GROUPS = len(ATT_PATTERNS)
ATT_Q_HEADS = ATT_GROUPS * ATT_KV_HEADS
ATT_BLOCK = 128
ATT_D_OUT = ATT_KV_HEADS * ATT_HEAD_DIM
N_BRANCH = 3
N_EXPERTS = 64
EXPERT_DIM = 512
TOP_K = 8
N_EXPERT_GROUPS = 8
TOPK_GROUPS = 4
ROUTED_SCALE = 2.5
DEEPNORM_ALPHA = (2 * DEPTH) ** 0.25
LN_EPS = 1e-5
RMS_EPS = 1e-6

LANES = 128
SUBLANES = 8
VMEM_LIMIT_BYTES = 56 * 1024 * 1024

IN_TILE = 1024
COL_LRU_X = N_BRANCH * D_MODEL
COL_LRU_G = COL_LRU_X + D_RNN
COL_SSM_Z = COL_LRU_G + D_RNN
COL_XBC = COL_SSM_Z + SSM_D_INNER
COL_DT = COL_XBC + SSM_XBC
COL_Q = COL_DT + SSM_HEADS
MAIN_COLS = -(-COL_Q // IN_TILE) * IN_TILE

MOE_ROWS = 256


def _cparams(sem):
    return pltpu.CompilerParams(dimension_semantics=sem, vmem_limit_bytes=VMEM_LIMIT_BYTES)


def _layer_norm(x, g, b):
    mu = jnp.mean(x, axis=-1, keepdims=True)
    xc = x - mu
    var = jnp.mean(xc * xc, axis=-1, keepdims=True)
    return xc * lax.rsqrt(var + LN_EPS) * g + b


def _store_token_major(ref, val):
    rows, w = val.shape
    k = w // LANES
    for j in range(k):
        ref[pl.ds(j, rows, stride=k), :] = val[:, j * LANES:(j + 1) * LANES]


def _load_token_major(ref, rows, k):
    return jnp.concatenate([ref[pl.ds(j, rows, stride=k), :] for j in range(k)], axis=1)


def _ln_kernel(x_ref, g_ref, b_ref, h_ref, hb_ref):
    y = _layer_norm(x_ref[...], g_ref[...], b_ref[...])
    h_ref[...] = y
    hb_ref[...] = y.astype(BF16)


def _embed_ln(x2d, g, b):
    n, d = x2d.shape
    tm = 512
    return pl.pallas_call(
        _ln_kernel,
        grid=(n // tm,),
        in_specs=[pl.BlockSpec((tm, d), lambda i: (i, 0)),
                  pl.BlockSpec((1, d), lambda i: (0, 0)),
                  pl.BlockSpec((1, d), lambda i: (0, 0))],
        out_specs=[pl.BlockSpec((tm, d), lambda i: (i, 0)),
                   pl.BlockSpec((tm, d), lambda i: (i, 0))],
        out_shape=[jax.ShapeDtypeStruct((n, d), F32), jax.ShapeDtypeStruct((n, d), BF16)],
        compiler_params=_cparams(("parallel",)),
        name="embed_ln",
    )(x2d, g.reshape(1, d), b.reshape(1, d))


def _in_proj_kernel(a_ref, w_ref, o_ref, wb_ref):
    @pl.when(pl.program_id(1) == 0)
    def _():
        wb_ref[...] = w_ref[...].astype(BF16)

    o_ref[...] = jnp.dot(a_ref[...], wb_ref[...], preferred_element_type=F32).astype(o_ref.dtype)


def _in_proj(hb, w_in, layer):
    m, k = hb.shape
    tm = 1024
    tn = IN_TILE
    return pl.pallas_call(
        _in_proj_kernel,
        grid=(MAIN_COLS // tn, m // tm),
        in_specs=[pl.BlockSpec((tm, k), lambda j, i: (i, 0)),
                  pl.BlockSpec((None, k, tn), lambda j, i: (layer, 0, j))],
        out_specs=pl.BlockSpec((tm, tn), lambda j, i: (i, j)),
        out_shape=jax.ShapeDtypeStruct((m, MAIN_COLS), BF16),
        scratch_shapes=[pltpu.VMEM((k, tn), BF16)],
        compiler_params=_cparams(("parallel", "arbitrary")),
        name="in_proj",
    )(hb, w_in)


def _store_dilated(acc_ref, dst_ref, d):
    nc, t, _ = acc_ref.shape
    u = t // d
    for c in range(nc):
        cols = slice(c * LANES, (c + 1) * LANES)
        if d == 1:
            dst_ref[:, cols] = acc_ref[c].astype(dst_ref.dtype)
            continue
        for r in range(d):
            dst_ref[r * u:(r + 1) * u, cols] = acc_ref[c, pl.ds(r, u, stride=d), :].astype(dst_ref.dtype)


def _dot_to_lane_tiles(a_ref, w_ref, acc_ref):
    res = jnp.dot(a_ref[...], w_ref[...], preferred_element_type=F32)
    for c in range(acc_ref.shape[0]):
        acc_ref[c] = res[:, c * LANES:(c + 1) * LANES]


def _q_proj_kernel(a_ref, w_ref, o_ref, acc_ref, *, tiles_per_group):
    j = pl.program_id(1)
    _dot_to_lane_tiles(a_ref, w_ref, acc_ref)
    for gi, (_, d) in enumerate(ATT_PATTERNS):
        @pl.when(j // tiles_per_group == gi)
        def _(d=d):
            _store_dilated(acc_ref, o_ref, d)


def _kv_proj_kernel(a_ref, w_ref, o_ref, acc_ref):
    _dot_to_lane_tiles(a_ref, w_ref, acc_ref)
    for gi, (_, d) in enumerate(ATT_PATTERNS):
        _store_dilated(acc_ref, o_ref.at[gi], d)


def _qkv_proj(hb, w_q, w_kv, batch, seq):
    n, k = hb.shape
    tn = 512
    nq = w_q.shape[1]
    q = pl.pallas_call(
        functools.partial(_q_proj_kernel, tiles_per_group=(nq // ATT_GROUPS) // tn),
        grid=(batch, nq // tn),
        in_specs=[pl.BlockSpec((seq, k), lambda b, j: (b, 0)),
                  pl.BlockSpec((k, tn), lambda b, j: (0, j))],
        out_specs=pl.BlockSpec((seq, tn), lambda b, j: (b, j)),
        out_shape=jax.ShapeDtypeStruct((n, nq), BF16),
        scratch_shapes=[pltpu.VMEM((tn // LANES, seq, LANES), F32)],
        compiler_params=_cparams(("parallel", "parallel")),
        name="q_proj",
    )(hb, w_q)
    nkv = w_kv.shape[1]
    kv = pl.pallas_call(
        _kv_proj_kernel,
        grid=(batch, nkv // tn),
        in_specs=[pl.BlockSpec((seq, k), lambda b, j: (b, 0)),
                  pl.BlockSpec((k, tn), lambda b, j: (0, j))],
        out_specs=pl.BlockSpec((ATT_GROUPS, seq, tn), lambda b, j: (0, b, j)),
        out_shape=jax.ShapeDtypeStruct((ATT_GROUPS, n, nkv), BF16),
        scratch_shapes=[pltpu.VMEM((tn // LANES, seq, LANES), F32)],
        compiler_params=_cparams(("parallel", "parallel")),
        name="kv_proj",
    )(hb, w_kv)
    return q, kv


def _attn_kernel(slopes_ref, q0_ref, q1_ref, q2_ref, k0_ref, k1_ref, k2_ref,
                 v0_ref, v1_ref, v2_ref, o_ref, acc_scr, m_scr, l_scr):
    h = pl.program_id(1)
    t = o_ref.shape[0]
    blk = ATT_BLOCK
    scale = ATT_HEAD_DIM ** -0.5
    qi = lax.broadcasted_iota(jnp.int32, (blk, 2 * blk), 0)
    kj = lax.broadcasted_iota(jnp.int32, (blk, 2 * blk), 1)
    dist = blk + qi - kj
    q_refs = (q0_ref, q1_ref, q2_ref)
    k_refs = (k0_ref, k1_ref, k2_ref)
    v_refs = (v0_ref, v1_ref, v2_ref)
    nt = (((1,), (1,)), ((), ()))
    for g, (window, d) in enumerate(ATT_PATTERNS):
        reach = window // d
        assert reach <= blk
        valid = (dist >= 0) & (dist <= reach)
        slope = slopes_ref[g, h]
        bias = jnp.where(valid, -(slope * d) * dist.astype(F32), -jnp.inf)
        bias_cur = bias[:, blk:]
        nb = (t // d) // blk
        q_ref, k_ref, v_ref = q_refs[g], k_refs[g], v_refs[g]
        for f in range(t // blk):
            r, i = divmod(f, nb)
            qb = q_ref[f * blk:(f + 1) * blk, :]
            if i > 0:
                kc = k_ref[(f - 1) * blk:(f + 1) * blk, :]
                vc = v_ref[(f - 1) * blk:(f + 1) * blk, :]
                s = lax.dot_general(qb, kc, nt, preferred_element_type=F32) * scale + bias
            else:
                kc = k_ref[f * blk:(f + 1) * blk, :]
                vc = v_ref[f * blk:(f + 1) * blk, :]
                s = lax.dot_general(qb, kc, nt, preferred_element_type=F32) * scale + bias_cur
            m = jnp.max(s, axis=-1, keepdims=True)
            p = jnp.exp(s - m)
            l = jnp.sum(p, axis=-1, keepdims=True)
            o = jnp.dot(p.astype(BF16), vc, preferred_element_type=F32)
            if d == 1:
                rows = slice(f * blk, (f + 1) * blk)
            else:
                rows = pl.ds(r + d * blk * i, blk, stride=d)
            acc_scr[g, rows, :] = o
            m_scr[g, rows, :] = jnp.broadcast_to(m, (blk, LANES))
            l_scr[g, rows, :] = jnp.broadcast_to(l, (blk, LANES))
    ch = 256
    for c in range(t // ch):
        rows = slice(c * ch, (c + 1) * ch)
        ms = [m_scr[g, rows, :] for g in range(ATT_GROUPS)]
        mx = jnp.maximum(jnp.maximum(ms[0], ms[1]), ms[2])
        num = jnp.zeros((ch, LANES), F32)
        den = jnp.zeros((ch, LANES), F32)
        for g in range(ATT_GROUPS):
            w = jnp.exp(ms[g] - mx)
            num = num + w * acc_scr[g, rows, :]
            den = den + w * l_scr[g, rows, :]
        o_ref[rows, :] = (num / den).astype(o_ref.dtype)


def _attention(q, kv, slopes, batch, seq):
    n = q.shape[0]
    hd = ATT_HEAD_DIM
    nh = ATT_KV_HEADS
    q_specs = [pl.BlockSpec((seq, hd), lambda b, h, g=g: (b, g * nh + h)) for g in range(ATT_GROUPS)]
    k_specs = [pl.BlockSpec((None, seq, hd), lambda b, h, g=g: (g, b, h)) for g in range(ATT_GROUPS)]
    v_specs = [pl.BlockSpec((None, seq, hd), lambda b, h, g=g: (g, b, nh + h)) for g in range(ATT_GROUPS)]
    return pl.pallas_call(
        _attn_kernel,
        grid=(batch, nh),
        in_specs=[pl.BlockSpec(memory_space=pltpu.SMEM)] + q_specs + k_specs + v_specs,
        out_specs=pl.BlockSpec((seq, hd), lambda b, h: (b, h)),
        out_shape=jax.ShapeDtypeStruct((n, nh * hd), BF16),
        scratch_shapes=[pltpu.VMEM((ATT_GROUPS, seq, hd), F32),
                        pltpu.VMEM((ATT_GROUPS, seq, LANES), F32),
                        pltpu.VMEM((ATT_GROUPS, seq, LANES), F32)],
        compiler_params=_cparams(("parallel", "parallel")),
        name="dilated_attention",
    )(slopes, q, q, q, kv, kv, kv, kv, kv, kv)


def _scan8(a, u, carry, row):
    for s in (1, 2, 4):
        a_sh = pltpu.roll(a, s, axis=0)
        u_sh = pltpu.roll(u, s, axis=0)
        m = row >= s
        u = jnp.where(m, a * u_sh + u, u)
        a = jnp.where(m, a * a_sh, a)
    return u + a * carry


def _lru_kernel(x_ref, g_ref, cw_ref, cb_ref, wr_ref, br_ref, wi_ref, bi_ref, lam_ref, o_ref,
                xpad_scr, a_scr, u_scr):
    t, c = o_ref.shape
    nb = LRU_BLOCKS
    bs = c // nb
    ch = 256
    pad = SUBLANES
    xpad_scr[0:pad, :] = jnp.zeros((pad, c), F32)
    for k in range(t // ch):
        xpad_scr[pad + k * ch:pad + (k + 1) * ch, :] = x_ref[k * ch:(k + 1) * ch, :].astype(F32)
    neg_lam = -lam_ref[...]
    sp = jnp.maximum(neg_lam, 0.0) + jnp.log1p(jnp.exp(-jnp.abs(neg_lam)))
    for k in range(t // ch):
        base = k * ch
        xc = cb_ref[...] + sum(
            cw_ref[j:j + 1, :] * xpad_scr[base + pad - (LRU_CONV - 1) + j:base + pad - (LRU_CONV - 1) + j + ch, :]
            for j in range(LRU_CONV))
        xcb = xc.astype(BF16)
        for n in range(nb):
            cols = slice(n * bs, (n + 1) * bs)
            xn = xcb[:, cols]
            r = jax.nn.sigmoid(jnp.dot(xn, wr_ref[n], preferred_element_type=F32) + br_ref[:, cols])
            ig = jax.nn.sigmoid(jnp.dot(xn, wi_ref[n], preferred_element_type=F32) + bi_ref[:, cols])
            log_a = -LRU_C * r * sp[:, cols]
            th = jnp.tanh(log_a)
            a_scr[base:base + ch, cols] = jnp.exp(log_a)
            u_scr[base:base + ch, cols] = jnp.sqrt(-2.0 * th / (1.0 - th)) * ig * xc[:, cols]
    row = lax.broadcasted_iota(jnp.int32, (SUBLANES, c), 0)

    def step(j, carry):
        rows = pl.ds(pl.multiple_of(j * SUBLANES, SUBLANES), SUBLANES)
        h8 = _scan8(a_scr[rows, :], u_scr[rows, :], carry, row)
        u_scr[rows, :] = h8
        return jnp.broadcast_to(h8[SUBLANES - 1:SUBLANES, :], (SUBLANES, c))

    lax.fori_loop(0, t // SUBLANES, step, jnp.zeros((SUBLANES, c), F32))
    for k in range(t // ch):
        rows = slice(k * ch, (k + 1) * ch)
        o_ref[rows, :] = (u_scr[rows, :] * jax.nn.gelu(g_ref[rows, :].astype(F32))).astype(o_ref.dtype)


def _rglru(proj, cw, cb, wr, br, wi, bi, lam, batch, seq):
    n = proj.shape[0]
    c = D_RNN
    bs = c // LRU_BLOCKS
    row = lambda b: (0, 0)
    return pl.pallas_call(
        _lru_kernel,
        grid=(batch,),
        in_specs=[pl.BlockSpec((seq, c), lambda b: (b, COL_LRU_X // c)),
                  pl.BlockSpec((seq, c), lambda b: (b, COL_LRU_G // c)),
                  pl.BlockSpec((LRU_CONV, c), row),
                  pl.BlockSpec((1, c), row),
                  pl.BlockSpec((LRU_BLOCKS, bs, bs), lambda b: (0, 0, 0)),
                  pl.BlockSpec((1, c), row),
                  pl.BlockSpec((LRU_BLOCKS, bs, bs), lambda b: (0, 0, 0)),
                  pl.BlockSpec((1, c), row),
                  pl.BlockSpec((1, c), row)],
        out_specs=pl.BlockSpec((seq, c), lambda b: (b, 0)),
        out_shape=jax.ShapeDtypeStruct((n, c), BF16),
        scratch_shapes=[pltpu.VMEM((seq + SUBLANES, c), F32),
                        pltpu.VMEM((seq, c), F32),
                        pltpu.VMEM((seq, c), F32)],
        compiler_params=_cparams(("parallel",)),
        name="rglru",
    )(proj, proj, cw, cb.reshape(1, c), wr, br.reshape(1, c), wi, bi.reshape(1, c), lam.reshape(1, c))


def _ssd_kernel(z_ref, xbc_ref, dtr_ref, cw_ref, cb_ref, dtb_ref, alog_ref, dskip_ref, ng_ref, o_ref,
                xpad_scr, st_scr):
    L = o_ref.shape[0]
    di = SSM_D_INNER
    ns = SSM_D_STATE
    pad = SUBLANES
    c = pl.program_id(1)

    @pl.when(c == 0)
    def _():
        xpad_scr[0:pad, :] = jnp.zeros((pad, SSM_XBC), F32)
        st_scr[...] = jnp.zeros(st_scr.shape, F32)

    xpad_scr[pad:pad + L, :] = xbc_ref[...].astype(F32)
    xc = cb_ref[...] + sum(
        cw_ref[j:j + 1, :] * xpad_scr[pad - (SSM_CONV - 1) + j:pad - (SSM_CONV - 1) + j + L, :]
        for j in range(SSM_CONV))
    xpad_scr[0:pad, :] = xpad_scr[L:L + pad, :]
    xc = xc * jax.nn.sigmoid(xc)
    xs = xc[:, :di]

    dt_in = dtr_ref[...].astype(F32) + dtb_ref[...]
    dt = jnp.maximum(dt_in, 0.0) + jnp.log1p(jnp.exp(-jnp.abs(dt_in)))
    adt = dt * (-jnp.exp(alog_ref[...]))
    ri = lax.broadcasted_iota(jnp.int32, (L, L), 0)
    ci = lax.broadcasted_iota(jnp.int32, (L, L), 1)
    causal = ri >= ci
    acum = jnp.dot(causal.astype(F32), adt, precision=HIGHEST, preferred_element_type=F32)
    acum_t = acum.T
    a_last = acum[L - 1:L, :]
    hl = lax.broadcasted_iota(jnp.int32, (LANES, di), 0)
    cl = lax.broadcasted_iota(jnp.int32, (LANES, di), 1)
    expand = (cl // SSM_HEAD_DIM == hl).astype(F32)
    dt_c = jnp.dot(dt, expand, precision=HIGHEST, preferred_element_type=F32)
    ea_c = jnp.dot(jnp.exp(acum), expand, precision=HIGHEST, preferred_element_type=F32)
    ds_c = jnp.dot(jnp.exp(a_last - acum), expand, precision=HIGHEST, preferred_element_type=F32)
    xdt = xs * dt_c
    xdt_b = xdt.astype(BF16)
    xw_b = (xdt * ds_c).astype(BF16)
    lane = lax.broadcasted_iota(jnp.int32, (L, LANES), 1)
    lo = lane < SSM_HEAD_DIM
    nt = (((1,), (1,)), ((), ()))
    heads_per_group = SSM_HEADS // SSM_GROUPS
    ys = []
    for g in range(SSM_GROUPS):
        bm = xc[:, di + g * ns:di + (g + 1) * ns]
        cm = xc[:, di + SSM_GROUPS * ns + g * ns:di + SSM_GROUPS * ns + (g + 1) * ns]
        bm_b = bm.astype(BF16)
        cm_b = cm.astype(BF16)
        bm_t = bm.T.astype(BF16)
        cb = lax.dot_general(cm_b, bm_b, nt, preferred_element_type=F32)
        for jp in range(heads_per_group // 2):
            j = g * (heads_per_group // 2) + jp
            cols = slice(j * LANES, (j + 1) * LANES)
            ms = []
            for hh in (2 * j, 2 * j + 1):
                seg = acum[:, hh:hh + 1] - acum_t[hh:hh + 1, :]
                decay = jnp.exp(jnp.where(causal, seg, -jnp.inf))
                ms.append((cb * decay).astype(BF16))
            mcat = jnp.concatenate(ms, axis=1)
            xp = xdt_b[:, cols]
            zero = jnp.zeros_like(xp)
            xcat = jnp.concatenate([jnp.where(lo, xp, zero), jnp.where(lo, zero, xp)], axis=0)
            y_diag = jnp.dot(mcat, xcat, preferred_element_type=F32)
            ent = st_scr[j]
            y_off = jnp.dot(cm_b, ent.astype(BF16), preferred_element_type=F32) * ea_c[:, cols]
            st_new = jnp.dot(bm_t, xw_b[:, cols], preferred_element_type=F32)
            st_scr[j] = st_new + ea_c[L - 1:L, cols] * ent
            ys.append(y_diag + y_off)
    y = jnp.concatenate(ys, axis=1) + dskip_ref[...] * xs
    zf = z_ref[...].astype(F32)
    y = y * (zf * jax.nn.sigmoid(zf))
    gw = di // SSM_GROUPS
    outs = []
    for g in range(SSM_GROUPS):
        yg = y[:, g * gw:(g + 1) * gw]
        outs.append(yg * lax.rsqrt(jnp.mean(yg * yg, axis=-1, keepdims=True) + RMS_EPS))
    o_ref[...] = (jnp.concatenate(outs, axis=1) * ng_ref[...]).astype(o_ref.dtype)


def _ssd(proj, cw, cb, dt_bias, a_log, d_skip, norm_g, batch, seq):
    n = proj.shape[0]
    L = SSM_CHUNK
    nc = seq // L
    di = SSM_D_INNER
    pad_h = LANES - SSM_HEADS
    dtb = jnp.pad(dt_bias, (0, pad_h)).reshape(1, LANES)
    alog = jnp.pad(a_log, (0, pad_h)).reshape(1, LANES)
    dskip = jnp.repeat(d_skip, SSM_HEAD_DIM).reshape(1, di)
    const = lambda b, c: (0, 0)
    return pl.pallas_call(
        _ssd_kernel,
        grid=(batch, nc),
        in_specs=[pl.BlockSpec((L, di), lambda b, c: (b * nc + c, COL_SSM_Z // di)),
                  pl.BlockSpec((L, SSM_XBC), lambda b, c: (b * nc + c, COL_XBC // SSM_XBC)),
                  pl.BlockSpec((L, LANES), lambda b, c: (b * nc + c, COL_DT // LANES)),
                  pl.BlockSpec((SSM_CONV, SSM_XBC), const),
                  pl.BlockSpec((1, SSM_XBC), const),
                  pl.BlockSpec((1, LANES), const),
                  pl.BlockSpec((1, LANES), const),
                  pl.BlockSpec((1, di), const),
                  pl.BlockSpec((1, di), const)],
        out_specs=pl.BlockSpec((L, di), lambda b, c: (b * nc + c, 0)),
        out_shape=jax.ShapeDtypeStruct((n, di), BF16),
        scratch_shapes=[pltpu.VMEM((L + SUBLANES, SSM_XBC), F32),
                        pltpu.VMEM((SSM_HEADS // 2, SSM_D_STATE, LANES), F32)],
        compiler_params=_cparams(("parallel", "arbitrary")),
        name="ssd",
    )(proj, proj, proj, cw, cb.reshape(1, SSM_XBC), dtb, alog, dskip, norm_g.reshape(1, di))


def _merge_kernel(g_ref, bg_ref, yl_ref, ys_ref, ya_ref, wl_ref, ws_ref, wa_ref, wo_ref,
                  h_ref, lg_ref, lb_ref, hn_ref, hb_ref, hrows_ref):
    d = h_ref.shape[1]
    merged = None
    for i, (y_ref, w_ref) in enumerate(((yl_ref, wl_ref), (ys_ref, ws_ref), (ya_ref, wa_ref))):
        gate = jax.nn.sigmoid(g_ref[:, i * d:(i + 1) * d].astype(F32) + bg_ref[:, i * d:(i + 1) * d])
        term = gate * jnp.dot(y_ref[...], w_ref[...], preferred_element_type=F32)
        merged = term if merged is None else merged + term
    mix = jnp.dot(merged.astype(BF16), wo_ref[...], preferred_element_type=F32)
    hn = _layer_norm(DEEPNORM_ALPHA * h_ref[...] + mix, lg_ref[...], lb_ref[...])
    hn_ref[...] = hn
    hb_ref[...] = hn.astype(BF16)
    _store_token_major(hrows_ref, hn)


def _merge_outproj_ln(proj, b_gate, y_lru, y_ssm, y_att, wl, ws, wa, wo, h, ln_g, ln_b):
    n, d = h.shape
    tm = 256
    gw = N_BRANCH * d
    kb = y_lru.shape[1]
    const = lambda i: (0, 0)
    once = pl.Buffered(1)
    return pl.pallas_call(
        _merge_kernel,
        grid=(n // tm,),
        in_specs=[pl.BlockSpec((tm, gw), lambda i: (i, 0)),
                  pl.BlockSpec((1, gw), const),
                  pl.BlockSpec((tm, kb), lambda i: (i, 0)),
                  pl.BlockSpec((tm, kb), lambda i: (i, 0)),
                  pl.BlockSpec((tm, kb), lambda i: (i, 0)),
                  pl.BlockSpec((kb, d), const, pipeline_mode=once),
                  pl.BlockSpec((kb, d), const, pipeline_mode=once),
                  pl.BlockSpec((kb, d), const, pipeline_mode=once),
                  pl.BlockSpec((d, d), const, pipeline_mode=once),
                  pl.BlockSpec((tm, d), lambda i: (i, 0)),
                  pl.BlockSpec((1, d), const),
                  pl.BlockSpec((1, d), const)],
        out_specs=[pl.BlockSpec((tm, d), lambda i: (i, 0)),
                   pl.BlockSpec((tm, d), lambda i: (i, 0)),
                   pl.BlockSpec((tm * (d // LANES), LANES), lambda i: (i, 0))],
        out_shape=[jax.ShapeDtypeStruct((n, d), F32), jax.ShapeDtypeStruct((n, d), BF16),
                   jax.ShapeDtypeStruct((n * (d // LANES), LANES), F32)],
        compiler_params=_cparams(("parallel",)),
        name="merge_outproj_ln",
    )(proj, b_gate.reshape(1, gw), y_lru, y_ssm, y_att, wl, ws, wa, wo, h,
      ln_g.reshape(1, d), ln_b.reshape(1, d))


def _seg_reduce(v, lane, op):
    for s in (1, 2, 4):
        up = pltpu.roll(v, LANES - s, axis=1)
        dn = pltpu.roll(v, s, axis=1)
        v = op(v, jnp.where((lane & s) == 0, up, dn))
    return v


def _router_kernel(h_ref, w_ref, b_ref, idx_ref, wt_ref):
    tm = h_ref.shape[0]
    logits = jnp.dot(h_ref[...], w_ref[...], precision=HIGHEST, preferred_element_type=F32)
    scores = jax.nn.sigmoid(logits)
    lane = lax.broadcasted_iota(jnp.int32, (tm, LANES), 1)
    lane_f = lane.astype(F32)
    real = lane < N_EXPERTS
    neg = -jnp.inf
    choice = jnp.where(real, scores + b_ref[...], neg)
    per_group = N_EXPERTS // N_EXPERT_GROUPS
    assert per_group == 8
    m1 = _seg_reduce(choice, lane, jnp.maximum)
    first = _seg_reduce(jnp.where(choice == m1, lane_f, float(LANES)), lane, jnp.minimum)
    m2 = _seg_reduce(jnp.where(lane_f == first, neg, choice), lane, jnp.maximum)
    gs = m1 + m2
    gidx = lane // per_group
    n_slots = LANES // per_group
    beaten = jnp.zeros((tm, LANES), jnp.int32)
    for k in range(1, n_slots):
        other = pltpu.roll(gs, per_group * k, axis=1)
        og = (gidx - k) & (n_slots - 1)
        wins = (other > gs) | ((other == gs) & (og < gidx))
        beaten = beaten + wins.astype(jnp.int32)
    masked = jnp.where((beaten < TOPK_GROUPS) & real, choice, neg)
    sel_i = jnp.zeros((tm, LANES), F32)
    sel_w = jnp.zeros((tm, LANES), F32)
    for k in range(TOP_K):
        m = jnp.max(masked, axis=1, keepdims=True)
        am = jnp.min(jnp.where(masked == m, lane_f, float(LANES)), axis=1, keepdims=True)
        hit = lane_f == am
        wk = jnp.sum(jnp.where(hit, scores, 0.0), axis=1, keepdims=True)
        sel_i = jnp.where(lane == k, am, sel_i)
        sel_w = jnp.where(lane == k, wk, sel_w)
        masked = jnp.where(hit, neg, masked)
    wsum = jnp.sum(sel_w, axis=1, keepdims=True)
    idx_ref[...] = sel_i.astype(jnp.int32)
    wt_ref[...] = sel_w / wsum * ROUTED_SCALE


def _router(h, router_w, router_bias):
    n, d = h.shape
    tm = 512
    pad_e = LANES - N_EXPERTS
    w = jnp.pad(router_w, ((0, 0), (0, pad_e)))
    b = jnp.pad(router_bias, (0, pad_e)).reshape(1, LANES)
    return pl.pallas_call(
        _router_kernel,
        grid=(n // tm,),
        in_specs=[pl.BlockSpec((tm, d), lambda i: (i, 0)),
                  pl.BlockSpec((d, LANES), lambda i: (0, 0)),
                  pl.BlockSpec((1, LANES), lambda i: (0, 0))],
        out_specs=[pl.BlockSpec((tm, LANES), lambda i: (i, 0)),
                   pl.BlockSpec((tm, LANES), lambda i: (i, 0))],
        out_shape=[jax.ShapeDtypeStruct((n, LANES), jnp.int32), jax.ShapeDtypeStruct((n, LANES), F32)],
        compiler_params=_cparams(("parallel",)),
        name="moe_router",
    )(h, w, b)


def _pack_bf16_pair(lo, hi):
    lo_bits = lax.bitcast_convert_type(lo.astype(BF16).astype(F32), jnp.uint32)
    hi_bits = lax.bitcast_convert_type(hi.astype(BF16).astype(F32), jnp.uint32)
    return (hi_bits & jnp.uint32(0xFFFF0000)) | (lo_bits >> 16)


def _unpack_bf16_pair(w):
    lo = lax.bitcast_convert_type(w << 16, F32)
    hi = lax.bitcast_convert_type(w & jnp.uint32(0xFFFF0000), F32)
    return lo, hi


def _expert_kernel(be_ref, nv_ref, first_ref, wslot_ref, nexte_ref, src_ref, nsrc_ref, dst_ref, rw_ref,
                   h_hbm, w1_hbm, w3_hbm, w2_hbm, slots_hbm,
                   xbuf, ybuf, wf1, wf3, wf2, w1_ref, w3_ref, w2_ref, gsem, ssem, wsem, *, layer):
    i = pl.program_id(0)
    n_valid = nv_ref[0]

    def weight_copies(e, ws):
        return (pltpu.make_async_copy(w1_hbm.at[layer, e], wf1.at[ws], wsem.at[ws]),
                pltpu.make_async_copy(w3_hbm.at[layer, e], wf3.at[ws], wsem.at[ws]),
                pltpu.make_async_copy(w2_hbm.at[layer, e], wf2.at[ws], wsem.at[ws]))

    @pl.when(i == 0)
    def _():
        for c in weight_copies(be_ref[0], 0):
            c.start()

    for ws in range(2):
        @pl.when((i < n_valid) & (first_ref[i] == 1) & (wslot_ref[i] == ws))
        def _(ws=ws):
            for c in weight_copies(0, ws):
                c.wait()

            @pl.when(nexte_ref[i] >= 0)
            def _():
                for c in weight_copies(nexte_ref[i], 1 - ws):
                    c.start()

            w1_ref[...] = wf1[ws].astype(BF16)
            w3_ref[...] = wf3[ws].astype(BF16)
            w2_ref[...] = wf2[ws].astype(BF16)

    rows = rw_ref.shape[0]
    kx = xbuf.shape[1] // rows
    ky = ybuf.shape[1] // rows
    n_real = slots_hbm.shape[0] - 2 * rows * ky
    slot = lax.rem(i, 2)

    def gather(idx_ref, s, r):
        tok = idx_ref[0, 0, r]
        return pltpu.make_async_copy(h_hbm.at[pl.ds(pl.multiple_of(tok * kx, kx), kx), :],
                                     xbuf.at[s, pl.ds(pl.multiple_of(r * kx, kx), kx), :], gsem.at[s])

    def scatter(s, r):
        dst = dst_ref[0, 0, r]
        return pltpu.make_async_copy(ybuf.at[s, pl.ds(pl.multiple_of(r * ky, ky), ky), :],
                                     slots_hbm.at[pl.ds(pl.multiple_of(dst * ky, ky), ky), :], ssem.at[s])

    def wait_gathers(s):
        pltpu.make_async_copy(h_hbm.at[pl.ds(0, rows * kx), :], xbuf.at[s], gsem.at[s]).wait()

    def wait_scatters(s):
        pltpu.make_async_copy(ybuf.at[s], slots_hbm.at[pl.ds(0, rows * ky), :], ssem.at[s]).wait()

    @pl.when(i == 0)
    def _():
        ybuf[0] = jnp.zeros((rows * ky, LANES), jnp.uint32)
        for p in range(2):
            pltpu.make_async_copy(ybuf.at[0], slots_hbm.at[pl.ds(n_real + p * rows * ky, rows * ky), :],
                                  ssem.at[0]).start()
        for p in range(2):
            pltpu.make_async_copy(ybuf.at[0], slots_hbm.at[pl.ds(n_real + p * rows * ky, rows * ky), :],
                                  ssem.at[0]).wait()

        def body(r, c):
            gather(src_ref, 0, r).start()
            return c
        lax.fori_loop(0, rows, body, 0, unroll=8)

    def step(s):
        wait_gathers(s)
        x = _load_token_major(xbuf.at[s], rows, kx).astype(BF16)
        for r in range(rows):
            gather(nsrc_ref, 1 - s, r).start()
        a = jnp.dot(x, w1_ref[...], preferred_element_type=F32)
        b = jnp.dot(x, w3_ref[...], preferred_element_type=F32)
        hb = (a * jax.nn.sigmoid(a) * b).astype(BF16)
        y = jnp.dot(hb, w2_ref[...], preferred_element_type=F32) * rw_ref[...]
        half = y.shape[1] // 2
        _store_token_major(ybuf.at[s], _pack_bf16_pair(y[:, :half], y[:, half:]))

        @pl.when(i >= 1)
        def _():
            wait_scatters(1 - s)

        for r in range(rows):
            scatter(s, r).start()

        @pl.when(i + 1 == n_valid)
        def _():
            wait_scatters(s)
            wait_gathers(1 - s)

    for s in range(2):
        @pl.when((i < n_valid) & (slot == s))
        def _(s=s):
            step(s)


def _routed_experts(h_rows, plan, w1, w3, w2, layer):
    block_e, n_valid, first, wslot, next_e, row_src, row_dst, row_w = plan
    d = w1.shape[2]
    kx = d // LANES
    ky = d // 2 // LANES
    n = h_rows.shape[0] // kx
    nblk = block_e.shape[0]
    rows = MOE_ROWS
    f = w1.shape[3]
    idx_block = (1, 1, rows)
    here = lambda i, *_: (i, 0, 0)
    ahead = lambda i, *_: (jnp.minimum(i + 1, nblk - 1), 0, 0)
    grid_spec = pltpu.PrefetchScalarGridSpec(
        num_scalar_prefetch=5,
        grid=(nblk,),
        in_specs=[pl.BlockSpec(idx_block, here, memory_space=pltpu.SMEM),
                  pl.BlockSpec(idx_block, ahead, memory_space=pltpu.SMEM),
                  pl.BlockSpec(idx_block, here, memory_space=pltpu.SMEM),
                  pl.BlockSpec((rows, 1), lambda i, *_: (i, 0)),
                  pl.BlockSpec(memory_space=pl.ANY),
                  pl.BlockSpec(memory_space=pl.ANY),
                  pl.BlockSpec(memory_space=pl.ANY),
                  pl.BlockSpec(memory_space=pl.ANY)],
        out_specs=pl.BlockSpec(memory_space=pl.ANY),
        scratch_shapes=[pltpu.VMEM((2, rows * kx, LANES), F32),
                        pltpu.VMEM((2, rows * ky, LANES), jnp.uint32),
                        pltpu.VMEM((2, d, f), F32),
                        pltpu.VMEM((2, d, f), F32),
                        pltpu.VMEM((2, f, d), F32),
                        pltpu.VMEM((d, f), BF16),
                        pltpu.VMEM((d, f), BF16),
                        pltpu.VMEM((f, d), BF16),
                        pltpu.SemaphoreType.DMA((2,)),
                        pltpu.SemaphoreType.DMA((2,)),
                        pltpu.SemaphoreType.DMA((2,))],
    )
    src3 = row_src.reshape(nblk, 1, rows)
    return pl.pallas_call(
        functools.partial(_expert_kernel, layer=layer),
        grid_spec=grid_spec,
        out_shape=jax.ShapeDtypeStruct(((TOP_K * n + 2 * rows) * ky, LANES), jnp.uint32),
        compiler_params=_cparams(("arbitrary",)),
        name="routed_experts",
    )(block_e, n_valid, first, wslot, next_e, src3, src3, row_dst.reshape(nblk, 1, rows),
      row_w.reshape(nblk * rows, 1), h_rows, w1, w3, w2)


def _dispatch_plan(idx, wts, n):
    rows = MOE_ROWS
    e = N_EXPERTS
    a_total = n * TOP_K
    nblk = (a_total + e * (rows - 1)) // rows
    e_flat = idx.reshape(a_total)
    order = jnp.argsort(e_flat).astype(jnp.int32)
    counts = jnp.sum((e_flat[:, None] == jnp.arange(e, dtype=jnp.int32)[None, :]).astype(jnp.int32), axis=0)
    blocks_e = (counts + rows - 1) // rows
    blk_end = jnp.cumsum(blocks_e)
    blk_start = blk_end - blocks_e
    start = jnp.cumsum(counts) - counts
    bi = jnp.arange(nblk, dtype=jnp.int32)
    block_e = jnp.minimum(jnp.sum((bi[:, None] >= blk_end[None, :]).astype(jnp.int32), axis=1), e - 1)
    n_valid = blk_end[-1:].astype(jnp.int32)
    r = jnp.arange(rows, dtype=jnp.int32)[None, :]
    j = (bi - blk_start[block_e])[:, None] * rows + r
    valid = (j < counts[block_e][:, None]) & (bi < n_valid[0])[:, None]
    a = order[jnp.clip(start[block_e][:, None] + j, 0, a_total - 1)]
    tok = a // TOP_K
    slot = a % TOP_K
    row_src = jnp.where(valid, tok, 0).astype(jnp.int32)
    dump = TOP_K * n + (bi % 2)[:, None] * rows + r
    row_dst = jnp.where(valid, slot * n + tok, dump).astype(jnp.int32)
    row_w = jnp.where(valid, wts.reshape(a_total)[a], 0.0).astype(F32)
    live = bi < n_valid[0]
    first = (jnp.concatenate([jnp.ones((1,), bool), block_e[1:] != block_e[:-1]]) & live).astype(jnp.int32)
    wslot = ((jnp.cumsum(first) - 1) % 2).astype(jnp.int32)
    run_start = jnp.where(first > 0, bi, nblk)
    nxt = lax.cummin(jnp.concatenate([run_start[1:], jnp.full((1,), nblk, jnp.int32)]), reverse=True)
    next_e = jnp.where(nxt < nblk, block_e[jnp.minimum(nxt, nblk - 1)], -1).astype(jnp.int32)
    return block_e, n_valid, first, wslot, next_e, row_src, row_dst, row_w


def _combine_kernel(*refs):
    slot_refs = refs[:TOP_K]
    hb_ref, h_ref, w1_ref, w3_ref, w2_ref, lg_ref, lb_ref, hn_ref, hbn_ref = refs[TOP_K:]
    tm, d = h_ref.shape
    ky = d // 2 // LANES
    lo = None
    hi = None
    for s_ref in slot_refs:
        l, u = _unpack_bf16_pair(_load_token_major(s_ref, tm, ky))
        lo = l if lo is None else lo + l
        hi = u if hi is None else hi + u
    routed = jnp.concatenate([lo, hi], axis=1)
    x = hb_ref[...]
    a = jnp.dot(x, w1_ref[...], preferred_element_type=F32)
    b = jnp.dot(x, w3_ref[...], preferred_element_type=F32)
    shared = jnp.dot((a * jax.nn.sigmoid(a) * b).astype(BF16), w2_ref[...], preferred_element_type=F32)
    hn = _layer_norm(DEEPNORM_ALPHA * h_ref[...] + routed + shared, lg_ref[...], lb_ref[...])
    hn_ref[...] = hn
    hbn_ref[...] = hn.astype(BF16)


def _combine_shared_ln(slots, hb, h, ws1, ws3, ws2, ln_g, ln_b):
    n, d = h.shape
    tm = 256
    ky = d // 2 // LANES
    f = ws1.shape[1]
    nt = n // tm
    const = lambda i: (0, 0)
    slot_specs = [pl.BlockSpec((tm * ky, LANES), lambda i, k=k: (k * nt + i, 0)) for k in range(TOP_K)]
    return pl.pallas_call(
        _combine_kernel,
        grid=(nt,),
        in_specs=slot_specs + [pl.BlockSpec((tm, d), lambda i: (i, 0)),
                               pl.BlockSpec((tm, d), lambda i: (i, 0)),
                               pl.BlockSpec((d, f), const),
                               pl.BlockSpec((d, f), const),
                               pl.BlockSpec((f, d), const),
                               pl.BlockSpec((1, d), const),
                               pl.BlockSpec((1, d), const)],
        out_specs=[pl.BlockSpec((tm, d), lambda i: (i, 0)),
                   pl.BlockSpec((tm, d), lambda i: (i, 0))],
        out_shape=[jax.ShapeDtypeStruct((n, d), F32), jax.ShapeDtypeStruct((n, d), BF16)],
        compiler_params=_cparams(("parallel",)),
        name="combine_shared_ln",
    )(*([slots] * TOP_K), hb, h, ws1, ws3, ws2, ln_g.reshape(1, d), ln_b.reshape(1, d))


def _qkv_weights(w):
    nq = ATT_Q_HEADS * ATT_HEAD_DIM
    return w[:, COL_Q:COL_Q + nq].astype(BF16), w[:, COL_Q + nq:].astype(BF16)


def kernel(x, emb_ln_g, emb_ln_b, w_in, b_gate, lru_conv_w, lru_conv_b, lru_wr, lru_br, lru_wi, lru_bi, lru_lambda, ssm_conv_w, ssm_conv_b, ssm_dt_bias, ssm_a_log, ssm_d, ssm_norm_g, w_proj_lru, w_proj_ssm, w_proj_att, w_out, ln1_g, ln1_b, router_w, router_bias, w1, w3, w2, ws1, ws3, ws2, ln2_g, ln2_b):
    batch, seq, d = x.shape
    n = batch * seq
    slopes = (2.0 ** (-8.0 * jnp.arange(1, ATT_Q_HEADS + 1, dtype=F32) / ATT_Q_HEADS)).reshape(ATT_GROUPS, ATT_KV_HEADS)
    h, hb = _embed_ln(x.reshape(n, d), emb_ln_g, emb_ln_b)
    for l in range(DEPTH):
        w_q, w_kv = _qkv_weights(w_in[l])
        proj = _in_proj(hb, w_in, l)
        q, kv = _qkv_proj(hb, w_q, w_kv, batch, seq)
        y_lru = _rglru(proj, lru_conv_w[l], lru_conv_b[l], lru_wr[l].astype(BF16), lru_br[l],
                       lru_wi[l].astype(BF16), lru_bi[l], lru_lambda[l], batch, seq)
        y_ssm = _ssd(proj, ssm_conv_w[l], ssm_conv_b[l], ssm_dt_bias[l], ssm_a_log[l], ssm_d[l],
                     ssm_norm_g[l], batch, seq)
        y_att = _attention(q, kv, slopes, batch, seq)
        h, hb, h_rows = _merge_outproj_ln(proj, b_gate[l], y_lru, y_ssm, y_att,
                                          w_proj_lru[l].astype(BF16), w_proj_ssm[l].astype(BF16),
                                          w_proj_att[l].astype(BF16), w_out[l].astype(BF16), h, ln1_g[l], ln1_b[l])
        idx, wts = _router(h, router_w[l], router_bias[l])
        plan = _dispatch_plan(idx[:, :TOP_K], wts[:, :TOP_K], n)
        slots = _routed_experts(h_rows, plan, w1, w3, w2, l)
        h, hb = _combine_shared_ln(slots, hb, h, ws1[l].astype(BF16), ws3[l].astype(BF16),
                                   ws2[l].astype(BF16), ln2_g[l], ln2_b[l])
    return h.reshape(batch, seq, d)
```

```python
import functools
import math

import jax
import jax.numpy as jnp
from jax import lax
from jax.experimental import pallas as pl
from jax.experimental.pallas import tpu as pltpu

F32 = jnp.float32
BF16 = jnp.bfloat16
HIGHEST = lax.Precision.HIGHEST

D_MODEL = 2048
DEPTH = 2
D_RNN = 1024
LRU_BLOCKS = 8
LRU_CONV = 4
LRU_C = 8.0
SSM_D_INNER = 1024
SSM_HEAD_DIM = 64
SSM_HEADS = SSM_D_INNER // SSM_HEAD_DIM
SSM_GROUPS = 2
SSM_D_STATE = 128
SSM_CONV = 4
SSM_CHUNK = 128
SSM_XBC = SSM_D_INNER + 2 * SSM_GROUPS * SSM_D_STATE
ATT_HEAD_DIM = 128
ATT_KV_HEADS = 8
ATT_PATTERNS = ((128, 1), (512, 4), (2048, 16))
ATT_GROUPS = len(ATT_PATTERNS)
ATT_Q_HEADS = ATT_GROUPS * ATT_KV_HEADS
ATT_BLOCK = 128
ATT_D_OUT = ATT_KV_HEADS * ATT_HEAD_DIM
N_BRANCH = 3
N_EXPERTS = 64
EXPERT_DIM = 512
TOP_K = 8
N_EXPERT_GROUPS = 8
TOPK_GROUPS = 4
ROUTED_SCALE = 2.5
DEEPNORM_ALPHA = (2 * DEPTH) ** 0.25
LN_EPS = 1e-5
RMS_EPS = 1e-6

LANES = 128
SUBLANES = 8
VMEM_LIMIT_BYTES = 56 * 1024 * 1024

IN_TILE = 1024
COL_LRU_X = N_BRANCH * D_MODEL
COL_LRU_G = COL_LRU_X + D_RNN
COL_SSM_Z = COL_LRU_G + D_RNN
COL_XBC = COL_SSM_Z + SSM_D_INNER
COL_DT = COL_XBC + SSM_XBC
COL_Q = COL_DT + SSM_HEADS
MAIN_COLS = -(-COL_Q // IN_TILE) * IN_TILE

MOE_ROWS = 256


def _cparams(sem):
    return pltpu.CompilerParams(dimension_semantics=sem, vmem_limit_bytes=VMEM_LIMIT_BYTES)


_NT = (((1,), (1,)), ((), ()))


def _layer_norm(x, g, b):
    mu = jnp.mean(x, axis=-1, keepdims=True)
    xc = x - mu
    var = jnp.mean(xc * xc, axis=-1, keepdims=True)
    return xc * lax.rsqrt(var + LN_EPS) * g + b


def _store_token_major(ref, val):
    rows, w = val.shape
    k = w // LANES
    for j in range(k):
        ref[pl.ds(j, rows, stride=k), :] = val[:, j * LANES:(j + 1) * LANES]


def _load_token_major(ref, rows, k):
    return jnp.concatenate([ref[pl.ds(j, rows, stride=k), :] for j in range(k)], axis=1)


def _ln_kernel(x_ref, g_ref, b_ref, h_ref, hb_ref):
    y = _layer_norm(x_ref[...], g_ref[...], b_ref[...])
    h_ref[...] = y
    hb_ref[...] = y.astype(BF16)


def _embed_ln(x2d, g, b):
    n, d = x2d.shape
    tm = 512
    return pl.pallas_call(
        _ln_kernel,
        grid=(n // tm,),
        in_specs=[pl.BlockSpec((tm, d), lambda i: (i, 0)),
                  pl.BlockSpec((1, d), lambda i: (0, 0)),
                  pl.BlockSpec((1, d), lambda i: (0, 0))],
        out_specs=[pl.BlockSpec((tm, d), lambda i: (i, 0)),
                   pl.BlockSpec((tm, d), lambda i: (i, 0))],
        out_shape=[jax.ShapeDtypeStruct((n, d), F32), jax.ShapeDtypeStruct((n, d), BF16)],
        compiler_params=_cparams(("parallel",)),
        name="embed_ln",
    )(x2d, g.reshape(1, d), b.reshape(1, d))


def _in_proj_kernel(a_ref, wt_ref, o_ref, wb_ref):
    @pl.when(pl.program_id(1) == 0)
    def _():
        wb_ref[...] = wt_ref[...].astype(BF16)

    o_ref[...] = lax.dot_general(a_ref[...], wb_ref[...], _NT, preferred_element_type=F32).astype(o_ref.dtype)


def _in_proj(hb, w_in_t, layer):
    m, k = hb.shape
    tm = 1024
    tn = IN_TILE
    return pl.pallas_call(
        _in_proj_kernel,
        grid=(MAIN_COLS // tn, m // tm),
        in_specs=[pl.BlockSpec((tm, k), lambda j, i: (i, 0)),
                  pl.BlockSpec((None, tn, k), lambda j, i: (layer, j, 0))],
        out_specs=pl.BlockSpec((tm, tn), lambda j, i: (i, j)),
        out_shape=jax.ShapeDtypeStruct((m, MAIN_COLS), BF16),
        scratch_shapes=[pltpu.VMEM((tn, k), BF16)],
        compiler_params=_cparams(("parallel", "arbitrary")),
        name="in_proj",
    )(hb, w_in_t)


def _store_dilated(acc_ref, dst_ref, d):
    nc, t, _ = acc_ref.shape
    u = t // d
    for c in range(nc):
        cols = slice(c * LANES, (c + 1) * LANES)
        if d == 1:
            dst_ref[:, cols] = acc_ref[c].astype(dst_ref.dtype)
            continue
        for r in range(d):
            dst_ref[r * u:(r + 1) * u, cols] = acc_ref[c, pl.ds(r, u, stride=d), :].astype(dst_ref.dtype)


def _load_weight_rows(wt_hbm, wf_ref, wb_ref, sem, layer, row0):
    cp = pltpu.make_async_copy(wt_hbm.at[layer, pl.ds(row0, wf_ref.shape[0]), :], wf_ref, sem)
    cp.start()
    cp.wait()
    wb_ref[...] = wf_ref[...].astype(BF16)


def _dot_to_lane_tiles(a_ref, wb_ref, acc_ref):
    res = lax.dot_general(a_ref[...], wb_ref[...], _NT, preferred_element_type=F32)
    for c in range(acc_ref.shape[0]):
        acc_ref[c] = res[:, c * LANES:(c + 1) * LANES]


def _q_proj_kernel(a_ref, wt_hbm, o_ref, acc_ref, wf_ref, wb_ref, sem, *, layer, row0, tiles_per_group):
    j = pl.program_id(0)

    @pl.when(pl.program_id(1) == 0)
    def _():
        _load_weight_rows(wt_hbm, wf_ref, wb_ref, sem, layer, pl.multiple_of(row0 + j * wf_ref.shape[0], SUBLANES))

    _dot_to_lane_tiles(a_ref, wb_ref, acc_ref)
    for gi, (_, d) in enumerate(ATT_PATTERNS):
        @pl.when(j // tiles_per_group == gi)
        def _(d=d):
            _store_dilated(acc_ref, o_ref, d)


def _kv_proj_kernel(a_ref, wt_hbm, o_ref, acc_ref, wf_ref, wb_ref, sem, *, layer, row0):
    j = pl.program_id(0)

    @pl.when(pl.program_id(1) == 0)
    def _():
        _load_weight_rows(wt_hbm, wf_ref, wb_ref, sem, layer, pl.multiple_of(row0 + j * wf_ref.shape[0], SUBLANES))

    _dot_to_lane_tiles(a_ref, wb_ref, acc_ref)
    for gi, (_, d) in enumerate(ATT_PATTERNS):
        _store_dilated(acc_ref, o_ref.at[gi], d)


def _qkv_proj(hb, w_in_t, layer, batch, seq):
    n, k = hb.shape
    tn = 512
    nq = ATT_Q_HEADS * ATT_HEAD_DIM
    nkv = 2 * ATT_D_OUT
    scratch = [pltpu.VMEM((tn // LANES, seq, LANES), F32),
               pltpu.VMEM((tn, k), F32),
               pltpu.VMEM((tn, k), BF16),
               pltpu.SemaphoreType.DMA(())]
    q = pl.pallas_call(
        functools.partial(_q_proj_kernel, layer=layer, row0=COL_Q, tiles_per_group=(nq // ATT_GROUPS) // tn),
        grid=(nq // tn, batch),
        in_specs=[pl.BlockSpec((seq, k), lambda j, b: (b, 0)),
                  pl.BlockSpec(memory_space=pl.ANY)],
        out_specs=pl.BlockSpec((seq, tn), lambda j, b: (b, j)),
        out_shape=jax.ShapeDtypeStruct((n, nq), BF16),
        scratch_shapes=scratch,
        compiler_params=_cparams(("parallel", "arbitrary")),
        name="q_proj",
    )(hb, w_in_t)
    kv = pl.pallas_call(
        functools.partial(_kv_proj_kernel, layer=layer, row0=COL_Q + nq),
        grid=(nkv // tn, batch),
        in_specs=[pl.BlockSpec((seq, k), lambda j, b: (b, 0)),
                  pl.BlockSpec(memory_space=pl.ANY)],
        out_specs=pl.BlockSpec((ATT_GROUPS, seq, tn), lambda j, b: (0, b, j)),
        out_shape=jax.ShapeDtypeStruct((ATT_GROUPS, n, nkv), BF16),
        scratch_shapes=scratch,
        compiler_params=_cparams(("parallel", "arbitrary")),
        name="kv_proj",
    )(hb, w_in_t)
    return q, kv


def _attn_kernel(slopes_ref, q0_ref, q1_ref, q2_ref, k0_ref, k1_ref, k2_ref,
                 v0_ref, v1_ref, v2_ref, o_ref, acc_scr, m_scr, l_scr):
    h = pl.program_id(1)
    t = o_ref.shape[0]
    blk = ATT_BLOCK
    scale = ATT_HEAD_DIM ** -0.5
    qi = lax.broadcasted_iota(jnp.int32, (blk, 2 * blk), 0)
    kj = lax.broadcasted_iota(jnp.int32, (blk, 2 * blk), 1)
    dist = blk + qi - kj
    q_refs = (q0_ref, q1_ref, q2_ref)
    k_refs = (k0_ref, k1_ref, k2_ref)
    v_refs = (v0_ref, v1_ref, v2_ref)
    nt = (((1,), (1,)), ((), ()))
    for g, (window, d) in enumerate(ATT_PATTERNS):
        reach = window // d
        assert reach <= blk
        valid = (dist >= 0) & (dist <= reach)
        slope = slopes_ref[g, h]
        bias = jnp.where(valid, -(slope * d) * dist.astype(F32), -jnp.inf)
        bias_cur = bias[:, blk:]
        nb = (t // d) // blk
        q_ref, k_ref, v_ref = q_refs[g], k_refs[g], v_refs[g]
        for f in range(t // blk):
            r, i = divmod(f, nb)
            qb = q_ref[f * blk:(f + 1) * blk, :]
            if i > 0:
                kc = k_ref[(f - 1) * blk:(f + 1) * blk, :]
                vc = v_ref[(f - 1) * blk:(f + 1) * blk, :]
                s = lax.dot_general(qb, kc, nt, preferred_element_type=F32) * scale + bias
            else:
                kc = k_ref[f * blk:(f + 1) * blk, :]
                vc = v_ref[f * blk:(f + 1) * blk, :]
                s = lax.dot_general(qb, kc, nt, preferred_element_type=F32) * scale + bias_cur
            m = jnp.max(s, axis=-1, keepdims=True)
            p = jnp.exp(s - m)
            l = jnp.sum(p, axis=-1, keepdims=True)
            o = jnp.dot(p.astype(BF16), vc, preferred_element_type=F32)
            if d == 1:
                rows = slice(f * blk, (f + 1) * blk)
            else:
                rows = pl.ds(r + d * blk * i, blk, stride=d)
            acc_scr[g, rows, :] = o
            m_scr[g, rows, :] = jnp.broadcast_to(m, (blk, LANES))
            l_scr[g, rows, :] = jnp.broadcast_to(l, (blk, LANES))
    ch = 256
    for c in range(t // ch):
        rows = slice(c * ch, (c + 1) * ch)
        ms = [m_scr[g, rows, :] for g in range(ATT_GROUPS)]
        mx = jnp.maximum(jnp.maximum(ms[0], ms[1]), ms[2])
        num = jnp.zeros((ch, LANES), F32)
        den = jnp.zeros((ch, LANES), F32)
        for g in range(ATT_GROUPS):
            w = jnp.exp(ms[g] - mx)
            num = num + w * acc_scr[g, rows, :]
            den = den + w * l_scr[g, rows, :]
        o_ref[rows, :] = (num / den).astype(o_ref.dtype)


def _attention(q, kv, slopes, batch, seq):
    n = q.shape[0]
    hd = ATT_HEAD_DIM
    nh = ATT_KV_HEADS
    q_specs = [pl.BlockSpec((seq, hd), lambda b, h, g=g: (b, g * nh + h)) for g in range(ATT_GROUPS)]
    k_specs = [pl.BlockSpec((None, seq, hd), lambda b, h, g=g: (g, b, h)) for g in range(ATT_GROUPS)]
    v_specs = [pl.BlockSpec((None, seq, hd), lambda b, h, g=g: (g, b, nh + h)) for g in range(ATT_GROUPS)]
    return pl.pallas_call(
        _attn_kernel,
        grid=(batch, nh),
        in_specs=[pl.BlockSpec(memory_space=pltpu.SMEM)] + q_specs + k_specs + v_specs,
        out_specs=pl.BlockSpec((seq, hd), lambda b, h: (b, h)),
        out_shape=jax.ShapeDtypeStruct((n, nh * hd), BF16),
        scratch_shapes=[pltpu.VMEM((ATT_GROUPS, seq, hd), F32),
                        pltpu.VMEM((ATT_GROUPS, seq, LANES), F32),
                        pltpu.VMEM((ATT_GROUPS, seq, LANES), F32)],
        compiler_params=_cparams(("parallel", "parallel")),
        name="dilated_attention",
    )(slopes, q, q, q, kv, kv, kv, kv, kv, kv)


def _scan8(a, u, carry, row):
    for s in (1, 2, 4):
        a_sh = pltpu.roll(a, s, axis=0)
        u_sh = pltpu.roll(u, s, axis=0)
        m = row >= s
        u = jnp.where(m, a * u_sh + u, u)
        a = jnp.where(m, a * a_sh, a)
    return u + a * carry


def _lru_kernel(x_ref, g_ref, cw_ref, cb_ref, wr_ref, br_ref, wi_ref, bi_ref, lam_ref, o_ref,
                xpad_scr, a_scr, u_scr):
    t, c = o_ref.shape
    nb = LRU_BLOCKS
    bs = c // nb
    ch = 256
    pad = SUBLANES
    xpad_scr[0:pad, :] = jnp.zeros((pad, c), F32)
    for k in range(t // ch):
        xpad_scr[pad + k * ch:pad + (k + 1) * ch, :] = x_ref[k * ch:(k + 1) * ch, :].astype(F32)
    neg_lam = -lam_ref[...]
    sp = jnp.maximum(neg_lam, 0.0) + jnp.log1p(jnp.exp(-jnp.abs(neg_lam)))
    for k in range(t // ch):
        base = k * ch
        xc = cb_ref[...] + sum(
            cw_ref[j:j + 1, :] * xpad_scr[base + pad - (LRU_CONV - 1) + j:base + pad - (LRU_CONV - 1) + j + ch, :]
            for j in range(LRU_CONV))
        xcb = xc.astype(BF16)
        for n in range(nb):
            cols = slice(n * bs, (n + 1) * bs)
            xn = xcb[:, cols]
            r = jax.nn.sigmoid(jnp.dot(xn, wr_ref[n], preferred_element_type=F32) + br_ref[:, cols])
            ig = jax.nn.sigmoid(jnp.dot(xn, wi_ref[n], preferred_element_type=F32) + bi_ref[:, cols])
            log_a = -LRU_C * r * sp[:, cols]
            th = jnp.tanh(log_a)
            a_scr[base:base + ch, cols] = jnp.exp(log_a)
            u_scr[base:base + ch, cols] = jnp.sqrt(-2.0 * th / (1.0 - th)) * ig * xc[:, cols]
    row = lax.broadcasted_iota(jnp.int32, (SUBLANES, c), 0)

    def step(j, carry):
        rows = pl.ds(pl.multiple_of(j * SUBLANES, SUBLANES), SUBLANES)
        h8 = _scan8(a_scr[rows, :], u_scr[rows, :], carry, row)
        u_scr[rows, :] = h8
        return jnp.broadcast_to(h8[SUBLANES - 1:SUBLANES, :], (SUBLANES, c))

    lax.fori_loop(0, t // SUBLANES, step, jnp.zeros((SUBLANES, c), F32))
    for k in range(t // ch):
        rows = slice(k * ch, (k + 1) * ch)
        o_ref[rows, :] = (u_scr[rows, :] * jax.nn.gelu(g_ref[rows, :].astype(F32))).astype(o_ref.dtype)


def _rglru(proj, cw, cb, wr, br, wi, bi, lam, batch, seq):
    n = proj.shape[0]
    c = D_RNN
    bs = c // LRU_BLOCKS
    row = lambda b: (0, 0)
    return pl.pallas_call(
        _lru_kernel,
        grid=(batch,),
        in_specs=[pl.BlockSpec((seq, c), lambda b: (b, COL_LRU_X // c)),
                  pl.BlockSpec((seq, c), lambda b: (b, COL_LRU_G // c)),
                  pl.BlockSpec((LRU_CONV, c), row),
                  pl.BlockSpec((1, c), row),
                  pl.BlockSpec((LRU_BLOCKS, bs, bs), lambda b: (0, 0, 0)),
                  pl.BlockSpec((1, c), row),
                  pl.BlockSpec((LRU_BLOCKS, bs, bs), lambda b: (0, 0, 0)),
                  pl.BlockSpec((1, c), row),
                  pl.BlockSpec((1, c), row)],
        out_specs=pl.BlockSpec((seq, c), lambda b: (b, 0)),
        out_shape=jax.ShapeDtypeStruct((n, c), BF16),
        scratch_shapes=[pltpu.VMEM((seq + SUBLANES, c), F32),
                        pltpu.VMEM((seq, c), F32),
                        pltpu.VMEM((seq, c), F32)],
        compiler_params=_cparams(("parallel",)),
        name="rglru",
    )(proj, proj, cw, cb.reshape(1, c), wr, br.reshape(1, c), wi, bi.reshape(1, c), lam.reshape(1, c))


def _ssd_kernel(z_ref, xbc_ref, dtr_ref, cw_ref, cb_ref, dtb_ref, alog_ref, dskip_ref, ng_ref, o_ref,
                xpad_scr, st_scr):
    L = o_ref.shape[0]
    di = SSM_D_INNER
    ns = SSM_D_STATE
    pad = SUBLANES
    c = pl.program_id(1)

    @pl.when(c == 0)
    def _():
        xpad_scr[0:pad, :] = jnp.zeros((pad, SSM_XBC), F32)
        st_scr[...] = jnp.zeros(st_scr.shape, F32)

    xpad_scr[pad:pad + L, :] = xbc_ref[...].astype(F32)
    xc = cb_ref[...] + sum(
        cw_ref[j:j + 1, :] * xpad_scr[pad - (SSM_CONV - 1) + j:pad - (SSM_CONV - 1) + j + L, :]
        for j in range(SSM_CONV))
    xpad_scr[0:pad, :] = xpad_scr[L:L + pad, :]
    xc = xc * jax.nn.sigmoid(xc)
    xs = xc[:, :di]

    dt_in = dtr_ref[...].astype(F32) + dtb_ref[...]
    dt = jnp.maximum(dt_in, 0.0) + jnp.log1p(jnp.exp(-jnp.abs(dt_in)))
    adt = dt * (-jnp.exp(alog_ref[...]))
    ri = lax.broadcasted_iota(jnp.int32, (L, L), 0)
    ci = lax.broadcasted_iota(jnp.int32, (L, L), 1)
    causal = ri >= ci
    acum = jnp.dot(causal.astype(F32), adt, precision=HIGHEST, preferred_element_type=F32)
    acum_t = acum.T
    a_last = acum[L - 1:L, :]
    hl = lax.broadcasted_iota(jnp.int32, (LANES, di), 0)
    cl = lax.broadcasted_iota(jnp.int32, (LANES, di), 1)
    expand = (cl // SSM_HEAD_DIM == hl).astype(F32)
    dt_c = jnp.dot(dt, expand, precision=HIGHEST, preferred_element_type=F32)
    ea_c = jnp.dot(jnp.exp(acum), expand, precision=HIGHEST, preferred_element_type=F32)
    ds_c = jnp.dot(jnp.exp(a_last - acum), expand, precision=HIGHEST, preferred_element_type=F32)
    xdt = xs * dt_c
    xdt_b = xdt.astype(BF16)
    xw_b = (xdt * ds_c).astype(BF16)
    lane = lax.broadcasted_iota(jnp.int32, (L, LANES), 1)
    lo = lane < SSM_HEAD_DIM
    nt = (((1,), (1,)), ((), ()))
    heads_per_group = SSM_HEADS // SSM_GROUPS
    ys = []
    for g in range(SSM_GROUPS):
        bm = xc[:, di + g * ns:di + (g + 1) * ns]
        cm = xc[:, di + SSM_GROUPS * ns + g * ns:di + SSM_GROUPS * ns + (g + 1) * ns]
        bm_b = bm.astype(BF16)
        cm_b = cm.astype(BF16)
        bm_t = bm.T.astype(BF16)
        cb = lax.dot_general(cm_b, bm_b, nt, preferred_element_type=F32)
        for jp in range(heads_per_group // 2):
            j = g * (heads_per_group // 2) + jp
            cols = slice(j * LANES, (j + 1) * LANES)
            ms = []
            for hh in (2 * j, 2 * j + 1):
                seg = acum[:, hh:hh + 1] - acum_t[hh:hh + 1, :]
                decay = jnp.exp(jnp.where(causal, seg, -jnp.inf))
                ms.append((cb * decay).astype(BF16))
            mcat = jnp.concatenate(ms, axis=1)
            xp = xdt_b[:, cols]
            zero = jnp.zeros_like(xp)
            xcat = jnp.concatenate([jnp.where(lo, xp, zero), jnp.where(lo, zero, xp)], axis=0)
            y_diag = jnp.dot(mcat, xcat, preferred_element_type=F32)
            ent = st_scr[j]
            y_off = jnp.dot(cm_b, ent.astype(BF16), preferred_element_type=F32) * ea_c[:, cols]
            st_new = jnp.dot(bm_t, xw_b[:, cols], preferred_element_type=F32)
            st_scr[j] = st_new + ea_c[L - 1:L, cols] * ent
            ys.append(y_diag + y_off)
    y = jnp.concatenate(ys, axis=1) + dskip_ref[...] * xs
    zf = z_ref[...].astype(F32)
    y = y * (zf * jax.nn.sigmoid(zf))
    gw = di // SSM_GROUPS
    outs = []
    for g in range(SSM_GROUPS):
        yg = y[:, g * gw:(g + 1) * gw]
        outs.append(yg * lax.rsqrt(jnp.mean(yg * yg, axis=-1, keepdims=True) + RMS_EPS))
    o_ref[...] = (jnp.concatenate(outs, axis=1) * ng_ref[...]).astype(o_ref.dtype)


def _ssd(proj, cw, cb, dt_bias, a_log, d_skip, norm_g, batch, seq):
    n = proj.shape[0]
    L = SSM_CHUNK
    nc = seq // L
    di = SSM_D_INNER
    pad_h = LANES - SSM_HEADS
    dtb = jnp.pad(dt_bias, (0, pad_h)).reshape(1, LANES)
    alog = jnp.pad(a_log, (0, pad_h)).reshape(1, LANES)
    dskip = jnp.repeat(d_skip, SSM_HEAD_DIM).reshape(1, di)
    const = lambda b, c: (0, 0)
    return pl.pallas_call(
        _ssd_kernel,
        grid=(batch, nc),
        in_specs=[pl.BlockSpec((L, di), lambda b, c: (b * nc + c, COL_SSM_Z // di)),
                  pl.BlockSpec((L, SSM_XBC), lambda b, c: (b * nc + c, COL_XBC // SSM_XBC)),
                  pl.BlockSpec((L, LANES), lambda b, c: (b * nc + c, COL_DT // LANES)),
                  pl.BlockSpec((SSM_CONV, SSM_XBC), const),
                  pl.BlockSpec((1, SSM_XBC), const),
                  pl.BlockSpec((1, LANES), const),
                  pl.BlockSpec((1, LANES), const),
                  pl.BlockSpec((1, di), const),
                  pl.BlockSpec((1, di), const)],
        out_specs=pl.BlockSpec((L, di), lambda b, c: (b * nc + c, 0)),
        out_shape=jax.ShapeDtypeStruct((n, di), BF16),
        scratch_shapes=[pltpu.VMEM((L + SUBLANES, SSM_XBC), F32),
                        pltpu.VMEM((SSM_HEADS // 2, SSM_D_STATE, LANES), F32)],
        compiler_params=_cparams(("parallel", "arbitrary")),
        name="ssd",
    )(proj, proj, proj, cw, cb.reshape(1, SSM_XBC), dtb, alog, dskip, norm_g.reshape(1, di))


def _merge_kernel(g_ref, bg_ref, yl_ref, ys_ref, ya_ref, wl_ref, ws_ref, wa_ref, wo_ref,
                  h_ref, lg_ref, lb_ref, hn_ref, hb_ref, hrows_ref):
    d = h_ref.shape[1]
    merged = None
    for i, (y_ref, w_ref) in enumerate(((yl_ref, wl_ref), (ys_ref, ws_ref), (ya_ref, wa_ref))):
        gate = jax.nn.sigmoid(g_ref[:, i * d:(i + 1) * d].astype(F32) + bg_ref[:, i * d:(i + 1) * d])
        term = gate * jnp.dot(y_ref[...], w_ref[...], preferred_element_type=F32)
        merged = term if merged is None else merged + term
    mix = jnp.dot(merged.astype(BF16), wo_ref[...], preferred_element_type=F32)
    hn = _layer_norm(DEEPNORM_ALPHA * h_ref[...] + mix, lg_ref[...], lb_ref[...])
    hn_ref[...] = hn
    hb_ref[...] = hn.astype(BF16)
    _store_token_major(hrows_ref, hn)


def _merge_outproj_ln(proj, b_gate, y_lru, y_ssm, y_att, wl, ws, wa, wo, h, ln_g, ln_b):
    n, d = h.shape
    tm = 256
    gw = N_BRANCH * d
    kb = y_lru.shape[1]
    const = lambda i: (0, 0)
    once = pl.Buffered(1)
    return pl.pallas_call(
        _merge_kernel,
        grid=(n // tm,),
        in_specs=[pl.BlockSpec((tm, gw), lambda i: (i, 0)),
                  pl.BlockSpec((1, gw), const),
                  pl.BlockSpec((tm, kb), lambda i: (i, 0)),
                  pl.BlockSpec((tm, kb), lambda i: (i, 0)),
                  pl.BlockSpec((tm, kb), lambda i: (i, 0)),
                  pl.BlockSpec((kb, d), const, pipeline_mode=once),
                  pl.BlockSpec((kb, d), const, pipeline_mode=once),
                  pl.BlockSpec((kb, d), const, pipeline_mode=once),
                  pl.BlockSpec((d, d), const, pipeline_mode=once),
                  pl.BlockSpec((tm, d), lambda i: (i, 0)),
                  pl.BlockSpec((1, d), const),
                  pl.BlockSpec((1, d), const)],
        out_specs=[pl.BlockSpec((tm, d), lambda i: (i, 0)),
                   pl.BlockSpec((tm, d), lambda i: (i, 0)),
                   pl.BlockSpec((tm * (d // LANES), LANES), lambda i: (i, 0))],
        out_shape=[jax.ShapeDtypeStruct((n, d), F32), jax.ShapeDtypeStruct((n, d), BF16),
                   jax.ShapeDtypeStruct((n * (d // LANES), LANES), F32)],
        compiler_params=_cparams(("parallel",)),
        name="merge_outproj_ln",
    )(proj, b_gate.reshape(1, gw), y_lru, y_ssm, y_att, wl, ws, wa, wo, h,
      ln_g.reshape(1, d), ln_b.reshape(1, d))


def _seg_reduce(v, lane, op):
    for s in (1, 2, 4):
        up = pltpu.roll(v, LANES - s, axis=1)
        dn = pltpu.roll(v, s, axis=1)
        v = op(v, jnp.where((lane & s) == 0, up, dn))
    return v


def _router_kernel(h_ref, w_ref, b_ref, idx_ref, wt_ref):
    tm = h_ref.shape[0]
    logits = jnp.dot(h_ref[...], w_ref[...], precision=HIGHEST, preferred_element_type=F32)
    scores = jax.nn.sigmoid(logits)
    lane = lax.broadcasted_iota(jnp.int32, (tm, LANES), 1)
    lane_f = lane.astype(F32)
    real = lane < N_EXPERTS
    neg = -jnp.inf
    choice = jnp.where(real, scores + b_ref[...], neg)
    per_group = N_EXPERTS // N_EXPERT_GROUPS
    assert per_group == 8
    m1 = _seg_reduce(choice, lane, jnp.maximum)
    first = _seg_reduce(jnp.where(choice == m1, lane_f, float(LANES)), lane, jnp.minimum)
    m2 = _seg_reduce(jnp.where(lane_f == first, neg, choice), lane, jnp.maximum)
    gs = m1 + m2
    gidx = lane // per_group
    n_slots = LANES // per_group
    beaten = jnp.zeros((tm, LANES), jnp.int32)
    for k in range(1, n_slots):
        other = pltpu.roll(gs, per_group * k, axis=1)
        og = (gidx - k) & (n_slots - 1)
        wins = (other > gs) | ((other == gs) & (og < gidx))
        beaten = beaten + wins.astype(jnp.int32)
    masked = jnp.where((beaten < TOPK_GROUPS) & real, choice, neg)
    sel_i = jnp.zeros((tm, LANES), F32)
    sel_w = jnp.zeros((tm, LANES), F32)
    for k in range(TOP_K):
        m = jnp.max(masked, axis=1, keepdims=True)
        am = jnp.min(jnp.where(masked == m, lane_f, float(LANES)), axis=1, keepdims=True)
        hit = lane_f == am
        wk = jnp.sum(jnp.where(hit, scores, 0.0), axis=1, keepdims=True)
        sel_i = jnp.where(lane == k, am, sel_i)
        sel_w = jnp.where(lane == k, wk, sel_w)
        masked = jnp.where(hit, neg, masked)
    wsum = jnp.sum(sel_w, axis=1, keepdims=True)
    idx_ref[...] = sel_i.astype(jnp.int32)
    wt_ref[...] = sel_w / wsum * ROUTED_SCALE


def _router(h, router_w, router_bias):
    n, d = h.shape
    tm = 512
    pad_e = LANES - N_EXPERTS
    w = jnp.pad(router_w, ((0, 0), (0, pad_e)))
    b = jnp.pad(router_bias, (0, pad_e)).reshape(1, LANES)
    return pl.pallas_call(
        _router_kernel,
        grid=(n // tm,),
        in_specs=[pl.BlockSpec((tm, d), lambda i: (i, 0)),
                  pl.BlockSpec((d, LANES), lambda i: (0, 0)),
                  pl.BlockSpec((1, LANES), lambda i: (0, 0))],
        out_specs=[pl.BlockSpec((tm, LANES), lambda i: (i, 0)),
                   pl.BlockSpec((tm, LANES), lambda i: (i, 0))],
        out_shape=[jax.ShapeDtypeStruct((n, LANES), jnp.int32), jax.ShapeDtypeStruct((n, LANES), F32)],
        compiler_params=_cparams(("parallel",)),
        name="moe_router",
    )(h, w, b)


def _pack_bf16_pair(lo, hi):
    lo_bits = lax.bitcast_convert_type(lo.astype(BF16).astype(F32), jnp.uint32)
    hi_bits = lax.bitcast_convert_type(hi.astype(BF16).astype(F32), jnp.uint32)
    return (hi_bits & jnp.uint32(0xFFFF0000)) | (lo_bits >> 16)


def _unpack_bf16_pair(w):
    lo = lax.bitcast_convert_type(w << 16, F32)
    hi = lax.bitcast_convert_type(w & jnp.uint32(0xFFFF0000), F32)
    return lo, hi


def _expert_kernel(be_ref, nv_ref, first_ref, wslot_ref, nexte_ref, src0_ref, src1_ref, src2_ref, dst_ref, rw_ref,
                   h_hbm, w1_hbm, w3_hbm, w2_hbm, slots_hbm,
                   xbuf, ybuf, wf1, wf3, wf2, w1_ref, w3_ref, w2_ref, gsem, ssem, wsem, *, layer):
    i = pl.program_id(0)
    n_valid = nv_ref[0]
    src_refs = (src0_ref, src1_ref, src2_ref)

    def weight_copies(e, ws):
        return (pltpu.make_async_copy(w1_hbm.at[layer, e], wf1.at[ws], wsem.at[ws]),
                pltpu.make_async_copy(w3_hbm.at[layer, e], wf3.at[ws], wsem.at[ws]),
                pltpu.make_async_copy(w2_hbm.at[layer, e], wf2.at[ws], wsem.at[ws]))

    @pl.when(i == 0)
    def _():
        for c in weight_copies(be_ref[0], 0):
            c.start()

    for ws in range(2):
        @pl.when((i < n_valid) & (first_ref[i] == 1) & (wslot_ref[i] == ws))
        def _(ws=ws):
            for c in weight_copies(0, ws):
                c.wait()

            @pl.when(nexte_ref[i] >= 0)
            def _():
                for c in weight_copies(nexte_ref[i], 1 - ws):
                    c.start()

            w1_ref[...] = wf1[ws].astype(BF16)
            w3_ref[...] = wf3[ws].astype(BF16)
            w2_ref[...] = wf2[ws].astype(BF16)

    rows = rw_ref.shape[0]
    kx = xbuf.shape[1] // rows
    ky = ybuf.shape[1] // rows
    n_real = slots_hbm.shape[0] - 2 * rows * ky
    depth = xbuf.shape[0]
    slot = lax.rem(i, depth)

    def gather(idx_ref, s, r):
        tok = idx_ref[0, 0, r]
        return pltpu.make_async_copy(h_hbm.at[pl.ds(pl.multiple_of(tok * kx, kx), kx), :],
                                     xbuf.at[s, pl.ds(pl.multiple_of(r * kx, kx), kx), :], gsem.at[s])

    def scatter(s, r):
        dst = dst_ref[0, 0, r]
        return pltpu.make_async_copy(ybuf.at[s, pl.ds(pl.multiple_of(r * ky, ky), ky), :],
                                     slots_hbm.at[pl.ds(pl.multiple_of(dst * ky, ky), ky), :], ssem.at[s])

    def wait_gathers(s):
        pltpu.make_async_copy(h_hbm.at[pl.ds(0, rows * kx), :], xbuf.at[s], gsem.at[s]).wait()

    def wait_scatters(s):
        pltpu.make_async_copy(ybuf.at[s], slots_hbm.at[pl.ds(0, rows * ky), :], ssem.at[s]).wait()

    @pl.when(i == 0)
    def _():
        ybuf[0] = jnp.zeros((rows * ky, LANES), jnp.uint32)
        for p in range(2):
            pltpu.make_async_copy(ybuf.at[0], slots_hbm.at[pl.ds(n_real + p * rows * ky, rows * ky), :],
                                  ssem.at[0]).start()
        for p in range(2):
            pltpu.make_async_copy(ybuf.at[0], slots_hbm.at[pl.ds(n_real + p * rows * ky, rows * ky), :],
                                  ssem.at[0]).wait()
        for blk, idx_ref in enumerate(src_refs[:depth - 1]):
            def body(r, c, blk=blk, idx_ref=idx_ref):
                gather(idx_ref, blk, r).start()
                return c
            lax.fori_loop(0, rows, body, 0, unroll=8)

    def step(s):
        ahead = (s + depth - 1) % depth
        wait_gathers(s)
        x = _load_token_major(xbuf.at[s], rows, kx).astype(BF16)
        for r in range(rows):
            gather(src_refs[depth - 1], ahead, r).start()
        a = jnp.dot(x, w1_ref[...], preferred_element_type=F32)
        b = jnp.dot(x, w3_ref[...], preferred_element_type=F32)
        hb = (a * jax.nn.sigmoid(a) * b).astype(BF16)
        y = jnp.dot(hb, w2_ref[...], preferred_element_type=F32) * rw_ref[...]
        half = y.shape[1] // 2
        _store_token_major(ybuf.at[s], _pack_bf16_pair(y[:, :half], y[:, half:]))

        @pl.when(i >= 1)
        def _():
            wait_scatters(ahead)

        for r in range(rows):
            scatter(s, r).start()

        @pl.when(i + 1 == n_valid)
        def _():
            wait_scatters(s)
            for k in range(1, depth):
                wait_gathers((s + k) % depth)

    for s in range(depth):
        @pl.when((i < n_valid) & (slot == s))
        def _(s=s):
            step(s)


def _routed_experts(h_rows, plan, w1, w3, w2, layer):
    block_e, n_valid, first, wslot, next_e, row_src, row_dst, row_w = plan
    d = w1.shape[2]
    kx = d // LANES
    ky = d // 2 // LANES
    n = h_rows.shape[0] // kx
    nblk = block_e.shape[0]
    rows = MOE_ROWS
    f = w1.shape[3]
    idx_block = (1, 1, rows)
    here = lambda i, *_: (i, 0, 0)
    ahead1 = lambda i, *_: (jnp.minimum(i + 1, nblk - 1), 0, 0)
    ahead2 = lambda i, *_: (jnp.minimum(i + 2, nblk - 1), 0, 0)
    grid_spec = pltpu.PrefetchScalarGridSpec(
        num_scalar_prefetch=5,
        grid=(nblk,),
        in_specs=[pl.BlockSpec(idx_block, here, memory_space=pltpu.SMEM),
                  pl.BlockSpec(idx_block, ahead1, memory_space=pltpu.SMEM),
                  pl.BlockSpec(idx_block, ahead2, memory_space=pltpu.SMEM),
                  pl.BlockSpec(idx_block, here, memory_space=pltpu.SMEM),
                  pl.BlockSpec((rows, 1), lambda i, *_: (i, 0)),
                  pl.BlockSpec(memory_space=pl.ANY),
                  pl.BlockSpec(memory_space=pl.ANY),
                  pl.BlockSpec(memory_space=pl.ANY),
                  pl.BlockSpec(memory_space=pl.ANY)],
        out_specs=pl.BlockSpec(memory_space=pl.ANY),
        scratch_shapes=[pltpu.VMEM((3, rows * kx, LANES), F32),
                        pltpu.VMEM((3, rows * ky, LANES), jnp.uint32),
                        pltpu.VMEM((2, d, f), F32),
                        pltpu.VMEM((2, d, f), F32),
                        pltpu.VMEM((2, f, d), F32),
                        pltpu.VMEM((d, f), BF16),
                        pltpu.VMEM((d, f), BF16),
                        pltpu.VMEM((f, d), BF16),
                        pltpu.SemaphoreType.DMA((3,)),
                        pltpu.SemaphoreType.DMA((3,)),
                        pltpu.SemaphoreType.DMA((2,))],
    )
    src3 = row_src.reshape(nblk, 1, rows)
    return pl.pallas_call(
        functools.partial(_expert_kernel, layer=layer),
        grid_spec=grid_spec,
        out_shape=jax.ShapeDtypeStruct(((TOP_K * n + 2 * rows) * ky, LANES), jnp.uint32),
        compiler_params=_cparams(("arbitrary",)),
        name="routed_experts",
    )(block_e, n_valid, first, wslot, next_e, src3, src3, src3, row_dst.reshape(nblk, 1, rows),
      row_w.reshape(nblk * rows, 1), h_rows, w1, w3, w2)


def _dispatch_plan(idx, wts, n):
    rows = MOE_ROWS
    e = N_EXPERTS
    a_total = n * TOP_K
    nblk = (a_total + e * (rows - 1)) // rows
    e_flat = idx.reshape(a_total)
    order = jnp.argsort(e_flat).astype(jnp.int32)
    counts = jnp.sum((e_flat[:, None] == jnp.arange(e, dtype=jnp.int32)[None, :]).astype(jnp.int32), axis=0)
    blocks_e = (counts + rows - 1) // rows
    blk_end = jnp.cumsum(blocks_e)
    blk_start = blk_end - blocks_e
    start = jnp.cumsum(counts) - counts
    bi = jnp.arange(nblk, dtype=jnp.int32)
    block_e = jnp.minimum(jnp.sum((bi[:, None] >= blk_end[None, :]).astype(jnp.int32), axis=1), e - 1)
    n_valid = blk_end[-1:].astype(jnp.int32)
    r = jnp.arange(rows, dtype=jnp.int32)[None, :]
    j = (bi - blk_start[block_e])[:, None] * rows + r
    valid = (j < counts[block_e][:, None]) & (bi < n_valid[0])[:, None]
    a = order[jnp.clip(start[block_e][:, None] + j, 0, a_total - 1)]
    tok = a // TOP_K
    slot = a % TOP_K
    row_src = jnp.where(valid, tok, 0).astype(jnp.int32)
    dump = TOP_K * n + (bi % 2)[:, None] * rows + r
    row_dst = jnp.where(valid, slot * n + tok, dump).astype(jnp.int32)
    row_w = jnp.where(valid, wts.reshape(a_total)[a], 0.0).astype(F32)
    live = bi < n_valid[0]
    first = (jnp.concatenate([jnp.ones((1,), bool), block_e[1:] != block_e[:-1]]) & live).astype(jnp.int32)
    wslot = ((jnp.cumsum(first) - 1) % 2).astype(jnp.int32)
    run_start = jnp.where(first > 0, bi, nblk)
    nxt = lax.cummin(jnp.concatenate([run_start[1:], jnp.full((1,), nblk, jnp.int32)]), reverse=True)
    next_e = jnp.where(nxt < nblk, block_e[jnp.minimum(nxt, nblk - 1)], -1).astype(jnp.int32)
    return block_e, n_valid, first, wslot, next_e, row_src, row_dst, row_w


def _combine_kernel(*refs):
    slot_refs = refs[:TOP_K]
    hb_ref, h_ref, w1_ref, w3_ref, w2_ref, lg_ref, lb_ref, hn_ref, hbn_ref = refs[TOP_K:]
    tm, d = h_ref.shape
    ky = d // 2 // LANES
    lo = None
    hi = None
    for s_ref in slot_refs:
        l, u = _unpack_bf16_pair(_load_token_major(s_ref, tm, ky))
        lo = l if lo is None else lo + l
        hi = u if hi is None else hi + u
    routed = jnp.concatenate([lo, hi], axis=1)
    x = hb_ref[...]
    a = jnp.dot(x, w1_ref[...], preferred_element_type=F32)
    b = jnp.dot(x, w3_ref[...], preferred_element_type=F32)
    shared = jnp.dot((a * jax.nn.sigmoid(a) * b).astype(BF16), w2_ref[...], preferred_element_type=F32)
    hn = _layer_norm(DEEPNORM_ALPHA * h_ref[...] + routed + shared, lg_ref[...], lb_ref[...])
    hn_ref[...] = hn
    hbn_ref[...] = hn.astype(BF16)


def _combine_shared_ln(slots, hb, h, ws1, ws3, ws2, ln_g, ln_b):
    n, d = h.shape
    tm = 256
    ky = d // 2 // LANES
    f = ws1.shape[1]
    nt = n // tm
    const = lambda i: (0, 0)
    slot_specs = [pl.BlockSpec((tm * ky, LANES), lambda i, k=k: (k * nt + i, 0)) for k in range(TOP_K)]
    return pl.pallas_call(
        _combine_kernel,
        grid=(nt,),
        in_specs=slot_specs + [pl.BlockSpec((tm, d), lambda i: (i, 0)),
                               pl.BlockSpec((tm, d), lambda i: (i, 0)),
                               pl.BlockSpec((d, f), const),
                               pl.BlockSpec((d, f), const),
                               pl.BlockSpec((f, d), const),
                               pl.BlockSpec((1, d), const),
                               pl.BlockSpec((1, d), const)],
        out_specs=[pl.BlockSpec((tm, d), lambda i: (i, 0)),
                   pl.BlockSpec((tm, d), lambda i: (i, 0))],
        out_shape=[jax.ShapeDtypeStruct((n, d), F32), jax.ShapeDtypeStruct((n, d), BF16)],
        compiler_params=_cparams(("parallel",)),
        name="combine_shared_ln",
    )(*([slots] * TOP_K), hb, h, ws1, ws3, ws2, ln_g.reshape(1, d), ln_b.reshape(1, d))


def kernel(x, emb_ln_g, emb_ln_b, w_in, b_gate, lru_conv_w, lru_conv_b, lru_wr, lru_br, lru_wi, lru_bi, lru_lambda, ssm_conv_w, ssm_conv_b, ssm_dt_bias, ssm_a_log, ssm_d, ssm_norm_g, w_proj_lru, w_proj_ssm, w_proj_att, w_out, ln1_g, ln1_b, router_w, router_bias, w1, w3, w2, ws1, ws3, ws2, ln2_g, ln2_b):
    batch, seq, d = x.shape
    n = batch * seq
    slopes = (2.0 ** (-8.0 * jnp.arange(1, ATT_Q_HEADS + 1, dtype=F32) / ATT_Q_HEADS)).reshape(ATT_GROUPS, ATT_KV_HEADS)
    h, hb = _embed_ln(x.reshape(n, d), emb_ln_g, emb_ln_b)
    w_in_t = jnp.swapaxes(w_in, 1, 2)
    for l in range(DEPTH):
        proj = _in_proj(hb, w_in_t, l)
        q, kv = _qkv_proj(hb, w_in_t, l, batch, seq)
        y_lru = _rglru(proj, lru_conv_w[l], lru_conv_b[l], lru_wr[l].astype(BF16), lru_br[l],
                       lru_wi[l].astype(BF16), lru_bi[l], lru_lambda[l], batch, seq)
        y_ssm = _ssd(proj, ssm_conv_w[l], ssm_conv_b[l], ssm_dt_bias[l], ssm_a_log[l], ssm_d[l],
                     ssm_norm_g[l], batch, seq)
        y_att = _attention(q, kv, slopes, batch, seq)
        h, hb, h_rows = _merge_outproj_ln(proj, b_gate[l], y_lru, y_ssm, y_att,
                                          w_proj_lru[l].astype(BF16), w_proj_ssm[l].astype(BF16),
                                          w_proj_att[l].astype(BF16), w_out[l].astype(BF16), h, ln1_g[l], ln1_b[l])
        idx, wts = _router(h, router_w[l], router_bias[l])
        plan = _dispatch_plan(idx[:, :TOP_K], wts[:, :TOP_K], n)
        slots = _routed_experts(h_rows, plan, w1, w3, w2, l)
        h, hb = _combine_shared_ln(slots, hb, h, ws1[l].astype(BF16), ws3[l].astype(BF16),
                                   ws2[l].astype(BF16), ln2_g[l], ln2_b[l])
    return h.reshape(batch, seq, d)
```

```python
import functools
import math

import jax
import jax.numpy as jnp
from jax import lax
from jax.experimental import pallas as pl
from jax.experimental.pallas import tpu as pltpu

F32 = jnp.float32
BF16 = jnp.bfloat16
HIGHEST = lax.Precision.HIGHEST

D_MODEL = 2048
DEPTH = 2
D_RNN = 1024
LRU_BLOCKS = 8
LRU_CONV = 4
LRU_C = 8.0
SSM_D_INNER = 1024
SSM_HEAD_DIM = 64
SSM_HEADS = SSM_D_INNER // SSM_HEAD_DIM
SSM_GROUPS = 2
SSM_D_STATE = 128
SSM_CONV = 4
SSM_CHUNK = 128
SSM_XBC = SSM_D_INNER + 2 * SSM_GROUPS * SSM_D_STATE
ATT_HEAD_DIM = 128
ATT_KV_HEADS = 8
ATT_PATTERNS = ((128, 1), (512, 4), (2048, 16))
ATT_GROUPS = len(ATT_PATTERNS)
ATT_Q_HEADS = ATT_GROUPS * ATT_KV_HEADS
ATT_BLOCK = 128
ATT_D_OUT = ATT_KV_HEADS * ATT_HEAD_DIM
N_BRANCH = 3
N_EXPERTS = 64
EXPERT_DIM = 512
TOP_K = 8
N_EXPERT_GROUPS = 8
TOPK_GROUPS = 4
ROUTED_SCALE = 2.5
DEEPNORM_ALPHA = (2 * DEPTH) ** 0.25
LN_EPS = 1e-5
RMS_EPS = 1e-6

LANES = 128
SUBLANES = 8
VMEM_LIMIT_BYTES = 56 * 1024 * 1024

IN_TILE = 1024
COL_LRU_X = N_BRANCH * D_MODEL
COL_LRU_G = COL_LRU_X + D_RNN
COL_SSM_Z = COL_LRU_G + D_RNN
COL_XBC = COL_SSM_Z + SSM_D_INNER
COL_DT = COL_XBC + SSM_XBC
COL_Q = COL_DT + SSM_HEADS
MAIN_COLS = -(-COL_Q // IN_TILE) * IN_TILE

MOE_ROWS = 256


def _cparams(sem):
    return pltpu.CompilerParams(dimension_semantics=sem, vmem_limit_bytes=VMEM_LIMIT_BYTES)


_NT = (((1,), (1,)), ((), ()))


def _layer_norm(x, g, b):
    mu = jnp.mean(x, axis=-1, keepdims=True)
    xc = x - mu
    var = jnp.mean(xc * xc, axis=-1, keepdims=True)
    return xc * lax.rsqrt(var + LN_EPS) * g + b


def _store_token_major(ref, val):
    rows, w = val.shape
    k = w // LANES
    for j in range(k):
        ref[pl.ds(j, rows, stride=k), :] = val[:, j * LANES:(j + 1) * LANES]


def _load_token_major(ref, rows, k):
    return jnp.concatenate([ref[pl.ds(j, rows, stride=k), :] for j in range(k)], axis=1)


def _ln_kernel(x_ref, g_ref, b_ref, h_ref, hb_ref):
    y = _layer_norm(x_ref[...], g_ref[...], b_ref[...])
    h_ref[...] = y
    hb_ref[...] = y.astype(BF16)


def _embed_ln(x2d, g, b):
    n, d = x2d.shape
    tm = 512
    return pl.pallas_call(
        _ln_kernel,
        grid=(n // tm,),
        in_specs=[pl.BlockSpec((tm, d), lambda i: (i, 0)),
                  pl.BlockSpec((1, d), lambda i: (0, 0)),
                  pl.BlockSpec((1, d), lambda i: (0, 0))],
        out_specs=[pl.BlockSpec((tm, d), lambda i: (i, 0)),
                   pl.BlockSpec((tm, d), lambda i: (i, 0))],
        out_shape=[jax.ShapeDtypeStruct((n, d), F32), jax.ShapeDtypeStruct((n, d), BF16)],
        compiler_params=_cparams(("parallel",)),
        name="embed_ln",
    )(x2d, g.reshape(1, d), b.reshape(1, d))


def _in_proj_kernel(a_ref, wt_ref, o_ref, wb_ref):
    @pl.when(pl.program_id(1) == 0)
    def _():
        wb_ref[...] = wt_ref[...].astype(BF16)

    o_ref[...] = lax.dot_general(a_ref[...], wb_ref[...], _NT, preferred_element_type=F32).astype(o_ref.dtype)


def _in_proj(hb, w_in_t, layer):
    m, k = hb.shape
    tm = 1024
    tn = IN_TILE
    return pl.pallas_call(
        _in_proj_kernel,
        grid=(MAIN_COLS // tn, m // tm),
        in_specs=[pl.BlockSpec((tm, k), lambda j, i: (i, 0)),
                  pl.BlockSpec((None, tn, k), lambda j, i: (layer, j, 0))],
        out_specs=pl.BlockSpec((tm, tn), lambda j, i: (i, j)),
        out_shape=jax.ShapeDtypeStruct((m, MAIN_COLS), BF16),
        scratch_shapes=[pltpu.VMEM((tn, k), BF16)],
        compiler_params=_cparams(("parallel", "arbitrary")),
        name="in_proj",
    )(hb, w_in_t)


def _store_dilated(acc_ref, dst_ref, d):
    nc, t, _ = acc_ref.shape
    u = t // d
    for c in range(nc):
        cols = slice(c * LANES, (c + 1) * LANES)
        if d == 1:
            dst_ref[:, cols] = acc_ref[c].astype(dst_ref.dtype)
            continue
        for r in range(d):
            dst_ref[r * u:(r + 1) * u, cols] = acc_ref[c, pl.ds(r, u, stride=d), :].astype(dst_ref.dtype)


def _load_weight_rows(wt_hbm, wf_ref, wb_ref, sem, layer, row0):
    cp = pltpu.make_async_copy(wt_hbm.at[layer, pl.ds(row0, wf_ref.shape[0]), :], wf_ref, sem)
    cp.start()
    cp.wait()
    wb_ref[...] = wf_ref[...].astype(BF16)


def _dot_to_lane_tiles(a_ref, wb_ref, acc_ref):
    res = lax.dot_general(a_ref[...], wb_ref[...], _NT, preferred_element_type=F32)
    for c in range(acc_ref.shape[0]):
        acc_ref[c] = res[:, c * LANES:(c + 1) * LANES]


def _q_proj_kernel(a_ref, wt_hbm, o_ref, acc_ref, wf_ref, wb_ref, sem, *, layer, row0, tiles_per_group):
    j = pl.program_id(0)

    @pl.when(pl.program_id(1) == 0)
    def _():
        _load_weight_rows(wt_hbm, wf_ref, wb_ref, sem, layer, pl.multiple_of(row0 + j * wf_ref.shape[0], SUBLANES))

    _dot_to_lane_tiles(a_ref, wb_ref, acc_ref)
    for gi, (_, d) in enumerate(ATT_PATTERNS):
        @pl.when(j // tiles_per_group == gi)
        def _(d=d):
            _store_dilated(acc_ref, o_ref, d)


def _kv_proj_kernel(a_ref, wt_hbm, o_ref, acc_ref, wf_ref, wb_ref, sem, *, layer, row0):
    j = pl.program_id(0)

    @pl.when(pl.program_id(1) == 0)
    def _():
        _load_weight_rows(wt_hbm, wf_ref, wb_ref, sem, layer, pl.multiple_of(row0 + j * wf_ref.shape[0], SUBLANES))

    _dot_to_lane_tiles(a_ref, wb_ref, acc_ref)
    for gi, (_, d) in enumerate(ATT_PATTERNS):
        _store_dilated(acc_ref, o_ref.at[gi], d)


def _qkv_proj(hb, w_in_t, layer, batch, seq):
    n, k = hb.shape
    tn = 512
    nq = ATT_Q_HEADS * ATT_HEAD_DIM
    nkv = 2 * ATT_D_OUT
    scratch = [pltpu.VMEM((tn // LANES, seq, LANES), F32),
               pltpu.VMEM((tn, k), F32),
               pltpu.VMEM((tn, k), BF16),
               pltpu.SemaphoreType.DMA(())]
    q = pl.pallas_call(
        functools.partial(_q_proj_kernel, layer=layer, row0=COL_Q, tiles_per_group=(nq // ATT_GROUPS) // tn),
        grid=(nq // tn, batch),
        in_specs=[pl.BlockSpec((seq, k), lambda j, b: (b, 0)),
                  pl.BlockSpec(memory_space=pl.ANY)],
        out_specs=pl.BlockSpec((seq, tn), lambda j, b: (b, j)),
        out_shape=jax.ShapeDtypeStruct((n, nq), BF16),
        scratch_shapes=scratch,
        compiler_params=_cparams(("parallel", "arbitrary")),
        name="q_proj",
    )(hb, w_in_t)
    kv = pl.pallas_call(
        functools.partial(_kv_proj_kernel, layer=layer, row0=COL_Q + nq),
        grid=(nkv // tn, batch),
        in_specs=[pl.BlockSpec((seq, k), lambda j, b: (b, 0)),
                  pl.BlockSpec(memory_space=pl.ANY)],
        out_specs=pl.BlockSpec((ATT_GROUPS, seq, tn), lambda j, b: (0, b, j)),
        out_shape=jax.ShapeDtypeStruct((ATT_GROUPS, n, nkv), BF16),
        scratch_shapes=scratch,
        compiler_params=_cparams(("parallel", "arbitrary")),
        name="kv_proj",
    )(hb, w_in_t)
    return q, kv


def _attn_kernel(slopes_ref, q0_ref, q1_ref, q2_ref, k0_ref, k1_ref, k2_ref,
                 v0_ref, v1_ref, v2_ref, o_ref, acc_scr, m_scr, l_scr):
    h = pl.program_id(1)
    t = o_ref.shape[0]
    blk = ATT_BLOCK
    scale = ATT_HEAD_DIM ** -0.5
    qi = lax.broadcasted_iota(jnp.int32, (blk, 2 * blk), 0)
    kj = lax.broadcasted_iota(jnp.int32, (blk, 2 * blk), 1)
    dist = blk + qi - kj
    q_refs = (q0_ref, q1_ref, q2_ref)
    k_refs = (k0_ref, k1_ref, k2_ref)
    v_refs = (v0_ref, v1_ref, v2_ref)
    nt = (((1,), (1,)), ((), ()))
    for g, (window, d) in enumerate(ATT_PATTERNS):
        reach = window // d
        assert reach <= blk
        valid = (dist >= 0) & (dist <= reach)
        slope = slopes_ref[g, h]
        bias = jnp.where(valid, -(slope * d) * dist.astype(F32), -jnp.inf)
        bias_cur = bias[:, blk:]
        nb = (t // d) // blk
        q_ref, k_ref, v_ref = q_refs[g], k_refs[g], v_refs[g]
        for f in range(t // blk):
            r, i = divmod(f, nb)
            qb = q_ref[f * blk:(f + 1) * blk, :]
            if i > 0:
                kc = k_ref[(f - 1) * blk:(f + 1) * blk, :]
                vc = v_ref[(f - 1) * blk:(f + 1) * blk, :]
                s = lax.dot_general(qb, kc, nt, preferred_element_type=F32) * scale + bias
            else:
                kc = k_ref[f * blk:(f + 1) * blk, :]
                vc = v_ref[f * blk:(f + 1) * blk, :]
                s = lax.dot_general(qb, kc, nt, preferred_element_type=F32) * scale + bias_cur
            m = jnp.max(s, axis=-1, keepdims=True)
            p = jnp.exp(s - m)
            l = jnp.sum(p, axis=-1, keepdims=True)
            o = jnp.dot(p.astype(BF16), vc, preferred_element_type=F32)
            if d == 1:
                rows = slice(f * blk, (f + 1) * blk)
            else:
                rows = pl.ds(r + d * blk * i, blk, stride=d)
            acc_scr[g, rows, :] = o
            m_scr[g, rows, :] = jnp.broadcast_to(m, (blk, LANES))
            l_scr[g, rows, :] = jnp.broadcast_to(l, (blk, LANES))
    ch = 256
    for c in range(t // ch):
        rows = slice(c * ch, (c + 1) * ch)
        ms = [m_scr[g, rows, :] for g in range(ATT_GROUPS)]
        mx = jnp.maximum(jnp.maximum(ms[0], ms[1]), ms[2])
        num = jnp.zeros((ch, LANES), F32)
        den = jnp.zeros((ch, LANES), F32)
        for g in range(ATT_GROUPS):
            w = jnp.exp(ms[g] - mx)
            num = num + w * acc_scr[g, rows, :]
            den = den + w * l_scr[g, rows, :]
        o_ref[rows, :] = (num / den).astype(o_ref.dtype)


def _attention(q, kv, slopes, batch, seq):
    n = q.shape[0]
    hd = ATT_HEAD_DIM
    nh = ATT_KV_HEADS
    q_specs = [pl.BlockSpec((seq, hd), lambda b, h, g=g: (b, g * nh + h)) for g in range(ATT_GROUPS)]
    k_specs = [pl.BlockSpec((None, seq, hd), lambda b, h, g=g: (g, b, h)) for g in range(ATT_GROUPS)]
    v_specs = [pl.BlockSpec((None, seq, hd), lambda b, h, g=g: (g, b, nh + h)) for g in range(ATT_GROUPS)]
    return pl.pallas_call(
        _attn_kernel,
        grid=(batch, nh),
        in_specs=[pl.BlockSpec(memory_space=pltpu.SMEM)] + q_specs + k_specs + v_specs,
        out_specs=pl.BlockSpec((seq, hd), lambda b, h: (b, h)),
        out_shape=jax.ShapeDtypeStruct((n, nh * hd), BF16),
        scratch_shapes=[pltpu.VMEM((ATT_GROUPS, seq, hd), F32),
                        pltpu.VMEM((ATT_GROUPS, seq, LANES), F32),
                        pltpu.VMEM((ATT_GROUPS, seq, LANES), F32)],
        compiler_params=_cparams(("parallel", "parallel")),
        name="dilated_attention",
    )(slopes, q, q, q, kv, kv, kv, kv, kv, kv)


def _scan8(a, u, carry, row):
    for s in (1, 2, 4):
        a_sh = pltpu.roll(a, s, axis=0)
        u_sh = pltpu.roll(u, s, axis=0)
        m = row >= s
        u = jnp.where(m, a * u_sh + u, u)
        a = jnp.where(m, a * a_sh, a)
    return u + a * carry


def _lru_kernel(x_ref, g_ref, cw_ref, cb_ref, wr_ref, br_ref, wi_ref, bi_ref, lam_ref, o_ref,
                xpad_scr, a_scr, u_scr):
    t, c = o_ref.shape
    nb = LRU_BLOCKS
    bs = c // nb
    ch = 256
    pad = SUBLANES
    xpad_scr[0:pad, :] = jnp.zeros((pad, c), F32)
    for k in range(t // ch):
        xpad_scr[pad + k * ch:pad + (k + 1) * ch, :] = x_ref[k * ch:(k + 1) * ch, :].astype(F32)
    neg_lam = -lam_ref[...]
    sp = jnp.maximum(neg_lam, 0.0) + jnp.log1p(jnp.exp(-jnp.abs(neg_lam)))
    for k in range(t // ch):
        base = k * ch
        xc = cb_ref[...] + sum(
            cw_ref[j:j + 1, :] * xpad_scr[base + pad - (LRU_CONV - 1) + j:base + pad - (LRU_CONV - 1) + j + ch, :]
            for j in range(LRU_CONV))
        xcb = xc.astype(BF16)
        for n in range(nb):
            cols = slice(n * bs, (n + 1) * bs)
            xn = xcb[:, cols]
            r = jax.nn.sigmoid(jnp.dot(xn, wr_ref[n], preferred_element_type=F32) + br_ref[:, cols])
            ig = jax.nn.sigmoid(jnp.dot(xn, wi_ref[n], preferred_element_type=F32) + bi_ref[:, cols])
            log_a = -LRU_C * r * sp[:, cols]
            th = jnp.tanh(log_a)
            a_scr[base:base + ch, cols] = jnp.exp(log_a)
            u_scr[base:base + ch, cols] = jnp.sqrt(-2.0 * th / (1.0 - th)) * ig * xc[:, cols]
    row = lax.broadcasted_iota(jnp.int32, (SUBLANES, c), 0)

    def step(j, carry):
        rows = pl.ds(pl.multiple_of(j * SUBLANES, SUBLANES), SUBLANES)
        h8 = _scan8(a_scr[rows, :], u_scr[rows, :], carry, row)
        u_scr[rows, :] = h8
        return jnp.broadcast_to(h8[SUBLANES - 1:SUBLANES, :], (SUBLANES, c))

    lax.fori_loop(0, t // SUBLANES, step, jnp.zeros((SUBLANES, c), F32))
    for k in range(t // ch):
        rows = slice(k * ch, (k + 1) * ch)
        o_ref[rows, :] = (u_scr[rows, :] * jax.nn.gelu(g_ref[rows, :].astype(F32))).astype(o_ref.dtype)


def _rglru(proj, cw, cb, wr, br, wi, bi, lam, batch, seq):
    n = proj.shape[0]
    c = D_RNN
    bs = c // LRU_BLOCKS
    row = lambda b: (0, 0)
    return pl.pallas_call(
        _lru_kernel,
        grid=(batch,),
        in_specs=[pl.BlockSpec((seq, c), lambda b: (b, COL_LRU_X // c)),
                  pl.BlockSpec((seq, c), lambda b: (b, COL_LRU_G // c)),
                  pl.BlockSpec((LRU_CONV, c), row),
                  pl.BlockSpec((1, c), row),
                  pl.BlockSpec((LRU_BLOCKS, bs, bs), lambda b: (0, 0, 0)),
                  pl.BlockSpec((1, c), row),
                  pl.BlockSpec((LRU_BLOCKS, bs, bs), lambda b: (0, 0, 0)),
                  pl.BlockSpec((1, c), row),
                  pl.BlockSpec((1, c), row)],
        out_specs=pl.BlockSpec((seq, c), lambda b: (b, 0)),
        out_shape=jax.ShapeDtypeStruct((n, c), BF16),
        scratch_shapes=[pltpu.VMEM((seq + SUBLANES, c), F32),
                        pltpu.VMEM((seq, c), F32),
                        pltpu.VMEM((seq, c), F32)],
        compiler_params=_cparams(("parallel",)),
        name="rglru",
    )(proj, proj, cw, cb.reshape(1, c), wr, br.reshape(1, c), wi, bi.reshape(1, c), lam.reshape(1, c))


def _ssd_kernel(z_ref, xbc_ref, dtr_ref, cw_ref, cb_ref, dtb_ref, alog_ref, dskip_ref, ng_ref, o_ref,
                xpad_scr, st_scr):
    L = o_ref.shape[0]
    di = SSM_D_INNER
    ns = SSM_D_STATE
    pad = SUBLANES
    c = pl.program_id(1)

    @pl.when(c == 0)
    def _():
        xpad_scr[0:pad, :] = jnp.zeros((pad, SSM_XBC), F32)
        st_scr[...] = jnp.zeros(st_scr.shape, F32)

    xpad_scr[pad:pad + L, :] = xbc_ref[...].astype(F32)
    xc = cb_ref[...] + sum(
        cw_ref[j:j + 1, :] * xpad_scr[pad - (SSM_CONV - 1) + j:pad - (SSM_CONV - 1) + j + L, :]
        for j in range(SSM_CONV))
    xpad_scr[0:pad, :] = xpad_scr[L:L + pad, :]
    xc = xc * jax.nn.sigmoid(xc)
    xs = xc[:, :di]

    dt_in = dtr_ref[...].astype(F32) + dtb_ref[...]
    dt = jnp.maximum(dt_in, 0.0) + jnp.log1p(jnp.exp(-jnp.abs(dt_in)))
    adt = dt * (-jnp.exp(alog_ref[...]))
    ri = lax.broadcasted_iota(jnp.int32, (L, L), 0)
    ci = lax.broadcasted_iota(jnp.int32, (L, L), 1)
    causal = ri >= ci
    acum = jnp.dot(causal.astype(F32), adt, precision=HIGHEST, preferred_element_type=F32)
    acum_t = acum.T
    a_last = acum[L - 1:L, :]
    hl = lax.broadcasted_iota(jnp.int32, (LANES, di), 0)
    cl = lax.broadcasted_iota(jnp.int32, (LANES, di), 1)
    expand = (cl // SSM_HEAD_DIM == hl).astype(F32)
    dt_c = jnp.dot(dt, expand, precision=HIGHEST, preferred_element_type=F32)
    ea_c = jnp.dot(jnp.exp(acum), expand, precision=HIGHEST, preferred_element_type=F32)
    ds_c = jnp.dot(jnp.exp(a_last - acum), expand, precision=HIGHEST, preferred_element_type=F32)
    xdt = xs * dt_c
    xdt_b = xdt.astype(BF16)
    xw_b = (xdt * ds_c).astype(BF16)
    lane = lax.broadcasted_iota(jnp.int32, (L, LANES), 1)
    lo = lane < SSM_HEAD_DIM
    nt = (((1,), (1,)), ((), ()))
    heads_per_group = SSM_HEADS // SSM_GROUPS
    ys = []
    for g in range(SSM_GROUPS):
        bm = xc[:, di + g * ns:di + (g + 1) * ns]
        cm = xc[:, di + SSM_GROUPS * ns + g * ns:di + SSM_GROUPS * ns + (g + 1) * ns]
        bm_b = bm.astype(BF16)
        cm_b = cm.astype(BF16)
        bm_t = bm.T.astype(BF16)
        cb = lax.dot_general(cm_b, bm_b, nt, preferred_element_type=F32)
        for jp in range(heads_per_group // 2):
            j = g * (heads_per_group // 2) + jp
            cols = slice(j * LANES, (j + 1) * LANES)
            ms = []
            for hh in (2 * j, 2 * j + 1):
                seg = acum[:, hh:hh + 1] - acum_t[hh:hh + 1, :]
                decay = jnp.exp(jnp.where(causal, seg, -jnp.inf))
                ms.append((cb * decay).astype(BF16))
            mcat = jnp.concatenate(ms, axis=1)
            xp = xdt_b[:, cols]
            zero = jnp.zeros_like(xp)
            xcat = jnp.concatenate([jnp.where(lo, xp, zero), jnp.where(lo, zero, xp)], axis=0)
            y_diag = jnp.dot(mcat, xcat, preferred_element_type=F32)
            ent = st_scr[j]
            y_off = jnp.dot(cm_b, ent.astype(BF16), preferred_element_type=F32) * ea_c[:, cols]
            st_new = jnp.dot(bm_t, xw_b[:, cols], preferred_element_type=F32)
            st_scr[j] = st_new + ea_c[L - 1:L, cols] * ent
            ys.append(y_diag + y_off)
    y = jnp.concatenate(ys, axis=1) + dskip_ref[...] * xs
    zf = z_ref[...].astype(F32)
    y = y * (zf * jax.nn.sigmoid(zf))
    gw = di // SSM_GROUPS
    outs = []
    for g in range(SSM_GROUPS):
        yg = y[:, g * gw:(g + 1) * gw]
        outs.append(yg * lax.rsqrt(jnp.mean(yg * yg, axis=-1, keepdims=True) + RMS_EPS))
    o_ref[...] = (jnp.concatenate(outs, axis=1) * ng_ref[...]).astype(o_ref.dtype)


def _ssd(proj, cw, cb, dt_bias, a_log, d_skip, norm_g, batch, seq):
    n = proj.shape[0]
    L = SSM_CHUNK
    nc = seq // L
    di = SSM_D_INNER
    pad_h = LANES - SSM_HEADS
    dtb = jnp.pad(dt_bias, (0, pad_h)).reshape(1, LANES)
    alog = jnp.pad(a_log, (0, pad_h)).reshape(1, LANES)
    dskip = jnp.repeat(d_skip, SSM_HEAD_DIM).reshape(1, di)
    const = lambda b, c: (0, 0)
    return pl.pallas_call(
        _ssd_kernel,
        grid=(batch, nc),
        in_specs=[pl.BlockSpec((L, di), lambda b, c: (b * nc + c, COL_SSM_Z // di)),
                  pl.BlockSpec((L, SSM_XBC), lambda b, c: (b * nc + c, COL_XBC // SSM_XBC)),
                  pl.BlockSpec((L, LANES), lambda b, c: (b * nc + c, COL_DT // LANES)),
                  pl.BlockSpec((SSM_CONV, SSM_XBC), const),
                  pl.BlockSpec((1, SSM_XBC), const),
                  pl.BlockSpec((1, LANES), const),
                  pl.BlockSpec((1, LANES), const),
                  pl.BlockSpec((1, di), const),
                  pl.BlockSpec((1, di), const)],
        out_specs=pl.BlockSpec((L, di), lambda b, c: (b * nc + c, 0)),
        out_shape=jax.ShapeDtypeStruct((n, di), BF16),
        scratch_shapes=[pltpu.VMEM((L + SUBLANES, SSM_XBC), F32),
                        pltpu.VMEM((SSM_HEADS // 2, SSM_D_STATE, LANES), F32)],
        compiler_params=_cparams(("parallel", "arbitrary")),
        name="ssd",
    )(proj, proj, proj, cw, cb.reshape(1, SSM_XBC), dtb, alog, dskip, norm_g.reshape(1, di))


def _merge_kernel(g_ref, bg_ref, yl_ref, ys_ref, ya_ref, wl_ref, ws_ref, wa_ref, wo_ref,
                  h_ref, lg_ref, lb_ref, hn_ref, hb_ref, hrows_ref):
    d = h_ref.shape[1]
    merged = None
    for i, (y_ref, w_ref) in enumerate(((yl_ref, wl_ref), (ys_ref, ws_ref), (ya_ref, wa_ref))):
        gate = jax.nn.sigmoid(g_ref[:, i * d:(i + 1) * d].astype(F32) + bg_ref[:, i * d:(i + 1) * d])
        term = gate * jnp.dot(y_ref[...], w_ref[...], preferred_element_type=F32)
        merged = term if merged is None else merged + term
    mix = jnp.dot(merged.astype(BF16), wo_ref[...], preferred_element_type=F32)
    hn = _layer_norm(DEEPNORM_ALPHA * h_ref[...] + mix, lg_ref[...], lb_ref[...])
    hn_ref[...] = hn
    hb_ref[...] = hn.astype(BF16)
    _store_token_major(hrows_ref, hn)


def _merge_outproj_ln(proj, b_gate, y_lru, y_ssm, y_att, wl, ws, wa, wo, h, ln_g, ln_b):
    n, d = h.shape
    tm = 256
    gw = N_BRANCH * d
    kb = y_lru.shape[1]
    const = lambda i: (0, 0)
    once = pl.Buffered(1)
    return pl.pallas_call(
        _merge_kernel,
        grid=(n // tm,),
        in_specs=[pl.BlockSpec((tm, gw), lambda i: (i, 0)),
                  pl.BlockSpec((1, gw), const),
                  pl.BlockSpec((tm, kb), lambda i: (i, 0)),
                  pl.BlockSpec((tm, kb), lambda i: (i, 0)),
                  pl.BlockSpec((tm, kb), lambda i: (i, 0)),
                  pl.BlockSpec((kb, d), const, pipeline_mode=once),
                  pl.BlockSpec((kb, d), const, pipeline_mode=once),
                  pl.BlockSpec((kb, d), const, pipeline_mode=once),
                  pl.BlockSpec((d, d), const, pipeline_mode=once),
                  pl.BlockSpec((tm, d), lambda i: (i, 0)),
                  pl.BlockSpec((1, d), const),
                  pl.BlockSpec((1, d), const)],
        out_specs=[pl.BlockSpec((tm, d), lambda i: (i, 0)),
                   pl.BlockSpec((tm, d), lambda i: (i, 0)),
                   pl.BlockSpec((tm * (d // LANES), LANES), lambda i: (i, 0))],
        out_shape=[jax.ShapeDtypeStruct((n, d), F32), jax.ShapeDtypeStruct((n, d), BF16),
                   jax.ShapeDtypeStruct((n * (d // LANES), LANES), F32)],
        compiler_params=_cparams(("parallel",)),
        name="merge_outproj_ln",
    )(proj, b_gate.reshape(1, gw), y_lru, y_ssm, y_att, wl, ws, wa, wo, h,
      ln_g.reshape(1, d), ln_b.reshape(1, d))


def _seg_reduce(v, lane, op):
    for s in (1, 2, 4):
        up = pltpu.roll(v, LANES - s, axis=1)
        dn = pltpu.roll(v, s, axis=1)
        v = op(v, jnp.where((lane & s) == 0, up, dn))
    return v


def _router_kernel(h_ref, w_ref, b_ref, idx_ref, wt_ref, cnt_ref):
    tm = h_ref.shape[0]
    logits = jnp.dot(h_ref[...], w_ref[...], precision=HIGHEST, preferred_element_type=F32)
    scores = jax.nn.sigmoid(logits)
    lane = lax.broadcasted_iota(jnp.int32, (tm, LANES), 1)
    lane_f = lane.astype(F32)
    real = lane < N_EXPERTS
    neg = -jnp.inf
    choice = jnp.where(real, scores + b_ref[...], neg)
    per_group = N_EXPERTS // N_EXPERT_GROUPS
    assert per_group == 8
    m1 = _seg_reduce(choice, lane, jnp.maximum)
    first = _seg_reduce(jnp.where(choice == m1, lane_f, float(LANES)), lane, jnp.minimum)
    m2 = _seg_reduce(jnp.where(lane_f == first, neg, choice), lane, jnp.maximum)
    gs = m1 + m2
    gidx = lane // per_group
    n_slots = LANES // per_group
    beaten = jnp.zeros((tm, LANES), jnp.int32)
    for k in range(1, n_slots):
        other = pltpu.roll(gs, per_group * k, axis=1)
        og = (gidx - k) & (n_slots - 1)
        wins = (other > gs) | ((other == gs) & (og < gidx))
        beaten = beaten + wins.astype(jnp.int32)
    masked = jnp.where((beaten < TOPK_GROUPS) & real, choice, neg)
    sel_i = jnp.zeros((tm, LANES), F32)
    sel_w = jnp.zeros((tm, LANES), F32)
    picked = jnp.zeros((tm, LANES), F32)
    for k in range(TOP_K):
        m = jnp.max(masked, axis=1, keepdims=True)
        am = jnp.min(jnp.where(masked == m, lane_f, float(LANES)), axis=1, keepdims=True)
        hit = lane_f == am
        wk = jnp.sum(jnp.where(hit, scores, 0.0), axis=1, keepdims=True)
        sel_i = jnp.where(lane == k, am, sel_i)
        sel_w = jnp.where(lane == k, wk, sel_w)
        picked = picked + jnp.where(hit, 1.0, 0.0)
        masked = jnp.where(hit, neg, masked)
    wsum = jnp.sum(sel_w, axis=1, keepdims=True)
    idx_ref[...] = sel_i.astype(jnp.int32)
    wt_ref[...] = sel_w / wsum * ROUTED_SCALE
    cnt_ref[...] = jnp.broadcast_to(jnp.sum(picked, axis=0, keepdims=True), cnt_ref.shape)


def _router(h, router_w, router_bias):
    n, d = h.shape
    tm = 512
    pad_e = LANES - N_EXPERTS
    w = jnp.pad(router_w, ((0, 0), (0, pad_e)))
    b = jnp.pad(router_bias, (0, pad_e)).reshape(1, LANES)
    return pl.pallas_call(
        _router_kernel,
        grid=(n // tm,),
        in_specs=[pl.BlockSpec((tm, d), lambda i: (i, 0)),
                  pl.BlockSpec((d, LANES), lambda i: (0, 0)),
                  pl.BlockSpec((1, LANES), lambda i: (0, 0))],
        out_specs=[pl.BlockSpec((tm, LANES), lambda i: (i, 0)),
                   pl.BlockSpec((tm, LANES), lambda i: (i, 0)),
                   pl.BlockSpec((SUBLANES, LANES), lambda i: (i, 0))],
        out_shape=[jax.ShapeDtypeStruct((n, LANES), jnp.int32), jax.ShapeDtypeStruct((n, LANES), F32),
                   jax.ShapeDtypeStruct((n // tm * SUBLANES, LANES), F32)],
        compiler_params=_cparams(("parallel",)),
        name="moe_router",
    )(h, w, b)


def _pack_bf16_pair(lo, hi):
    lo_bits = lax.bitcast_convert_type(lo.astype(BF16).astype(F32), jnp.uint32)
    hi_bits = lax.bitcast_convert_type(hi.astype(BF16).astype(F32), jnp.uint32)
    return (hi_bits & jnp.uint32(0xFFFF0000)) | (lo_bits >> 16)


def _unpack_bf16_pair(w):
    lo = lax.bitcast_convert_type(w << 16, F32)
    hi = lax.bitcast_convert_type(w & jnp.uint32(0xFFFF0000), F32)
    return lo, hi


def _expert_kernel(be_ref, nv_ref, first_ref, wslot_ref, nexte_ref, src0_ref, src1_ref, src2_ref, dst_ref,
                   h_hbm, w1_hbm, w3_hbm, w2_hbm, slots_hbm,
                   xbuf, ybuf, wf1, wf3, wf2, w1_ref, w3_ref, w2_ref, gsem, ssem, wsem, *, layer):
    i = pl.program_id(0)
    n_valid = nv_ref[0]
    src_refs = (src0_ref, src1_ref, src2_ref)

    def weight_copies(e, ws):
        return (pltpu.make_async_copy(w1_hbm.at[layer, e], wf1.at[ws], wsem.at[ws]),
                pltpu.make_async_copy(w3_hbm.at[layer, e], wf3.at[ws], wsem.at[ws]),
                pltpu.make_async_copy(w2_hbm.at[layer, e], wf2.at[ws], wsem.at[ws]))

    @pl.when(i == 0)
    def _():
        for c in weight_copies(be_ref[0], 0):
            c.start(priority=1)

    for ws in range(2):
        @pl.when((i < n_valid) & (first_ref[i] == 1) & (wslot_ref[i] == ws))
        def _(ws=ws):
            for c in weight_copies(0, ws):
                c.wait()

            @pl.when(nexte_ref[i] >= 0)
            def _():
                for c in weight_copies(nexte_ref[i], 1 - ws):
                    c.start(priority=1)

            w1_ref[...] = wf1[ws].astype(BF16)
            w3_ref[...] = wf3[ws].astype(BF16)
            w2_ref[...] = wf2[ws].astype(BF16)

    rows = dst_ref.shape[2]
    kx = xbuf.shape[1] // rows
    ky = ybuf.shape[1] // rows
    n_real = slots_hbm.shape[0] - 2 * rows * ky
    depth = xbuf.shape[0]
    slot = lax.rem(i, depth)

    def gather(idx_ref, s, r):
        tok = idx_ref[0, 0, r]
        return pltpu.make_async_copy(h_hbm.at[pl.ds(pl.multiple_of(tok * kx, kx), kx), :],
                                     xbuf.at[s, pl.ds(pl.multiple_of(r * kx, kx), kx), :], gsem.at[s])

    def scatter(s, r):
        dst = dst_ref[0, 0, r]
        return pltpu.make_async_copy(ybuf.at[s, pl.ds(pl.multiple_of(r * ky, ky), ky), :],
                                     slots_hbm.at[pl.ds(pl.multiple_of(dst * ky, ky), ky), :], ssem.at[s])

    def wait_gathers(s):
        pltpu.make_async_copy(h_hbm.at[pl.ds(0, rows * kx), :], xbuf.at[s], gsem.at[s]).wait()

    def wait_scatters(s):
        pltpu.make_async_copy(ybuf.at[s], slots_hbm.at[pl.ds(0, rows * ky), :], ssem.at[s]).wait()

    @pl.when(i == 0)
    def _():
        ybuf[0] = jnp.zeros((rows * ky, LANES), jnp.uint32)
        for p in range(2):
            pltpu.make_async_copy(ybuf.at[0], slots_hbm.at[pl.ds(n_real + p * rows * ky, rows * ky), :],
                                  ssem.at[0]).start()
        for p in range(2):
            pltpu.make_async_copy(ybuf.at[0], slots_hbm.at[pl.ds(n_real + p * rows * ky, rows * ky), :],
                                  ssem.at[0]).wait()
        for blk, idx_ref in enumerate(src_refs[:depth - 1]):
            def body(r, c, blk=blk, idx_ref=idx_ref):
                gather(idx_ref, blk, r).start()
                return c
            lax.fori_loop(0, rows, body, 0, unroll=8)

    def step(s):
        ahead = (s + depth - 1) % depth
        wait_gathers(s)
        x = _load_token_major(xbuf.at[s], rows, kx).astype(BF16)
        for r in range(rows):
            gather(src_refs[depth - 1], ahead, r).start()
        a = jnp.dot(x, w1_ref[...], preferred_element_type=F32)
        b = jnp.dot(x, w3_ref[...], preferred_element_type=F32)
        hb = (a * jax.nn.sigmoid(a) * b).astype(BF16)
        y = jnp.dot(hb, w2_ref[...], preferred_element_type=F32)
        half = y.shape[1] // 2
        _store_token_major(ybuf.at[s], _pack_bf16_pair(y[:, :half], y[:, half:]))

        @pl.when(i >= 1)
        def _():
            wait_scatters(ahead)

        for r in range(rows):
            scatter(s, r).start(priority=r % 2)

        @pl.when(i + 1 == n_valid)
        def _():
            wait_scatters(s)
            for k in range(1, depth):
                wait_gathers((s + k) % depth)

    for s in range(depth):
        @pl.when((i < n_valid) & (slot == s))
        def _(s=s):
            step(s)


def _routed_experts(h_rows, plan, w1, w3, w2, layer):
    block_e, n_valid, first, wslot, next_e, row_src, row_dst = plan
    d = w1.shape[2]
    kx = d // LANES
    ky = d // 2 // LANES
    n = h_rows.shape[0] // kx
    nblk = block_e.shape[0]
    rows = MOE_ROWS
    f = w1.shape[3]
    idx_block = (1, 1, rows)
    here = lambda i, *_: (i, 0, 0)
    ahead1 = lambda i, *_: (jnp.minimum(i + 1, nblk - 1), 0, 0)
    ahead2 = lambda i, *_: (jnp.minimum(i + 2, nblk - 1), 0, 0)
    grid_spec = pltpu.PrefetchScalarGridSpec(
        num_scalar_prefetch=5,
        grid=(nblk,),
        in_specs=[pl.BlockSpec(idx_block, here, memory_space=pltpu.SMEM),
                  pl.BlockSpec(idx_block, ahead1, memory_space=pltpu.SMEM),
                  pl.BlockSpec(idx_block, ahead2, memory_space=pltpu.SMEM),
                  pl.BlockSpec(idx_block, here, memory_space=pltpu.SMEM),
                  pl.BlockSpec(memory_space=pl.ANY),
                  pl.BlockSpec(memory_space=pl.ANY),
                  pl.BlockSpec(memory_space=pl.ANY),
                  pl.BlockSpec(memory_space=pl.ANY)],
        out_specs=pl.BlockSpec(memory_space=pl.ANY),
        scratch_shapes=[pltpu.VMEM((3, rows * kx, LANES), F32),
                        pltpu.VMEM((3, rows * ky, LANES), jnp.uint32),
                        pltpu.VMEM((2, d, f), F32),
                        pltpu.VMEM((2, d, f), F32),
                        pltpu.VMEM((2, f, d), F32),
                        pltpu.VMEM((d, f), BF16),
                        pltpu.VMEM((d, f), BF16),
                        pltpu.VMEM((f, d), BF16),
                        pltpu.SemaphoreType.DMA((3,)),
                        pltpu.SemaphoreType.DMA((3,)),
                        pltpu.SemaphoreType.DMA((2,))],
    )
    src3 = row_src.reshape(nblk, 1, rows)
    return pl.pallas_call(
        functools.partial(_expert_kernel, layer=layer),
        grid_spec=grid_spec,
        out_shape=jax.ShapeDtypeStruct(((TOP_K * n + 2 * rows) * ky, LANES), jnp.uint32),
        compiler_params=_cparams(("arbitrary",)),
        name="routed_experts",
    )(block_e, n_valid, first, wslot, next_e, src3, src3, src3, row_dst.reshape(nblk, 1, rows),
      h_rows, w1, w3, w2)


def _dispatch_plan(idx, counts, n):
    rows = MOE_ROWS
    e = N_EXPERTS
    a_total = n * TOP_K
    nblk = (a_total + e * (rows - 1)) // rows
    e_flat = idx.reshape(a_total)
    order = jnp.argsort(e_flat).astype(jnp.int32)
    blocks_e = (counts + rows - 1) // rows
    blk_end = jnp.cumsum(blocks_e)
    blk_start = blk_end - blocks_e
    start = jnp.cumsum(counts) - counts
    bi = jnp.arange(nblk, dtype=jnp.int32)
    block_e = jnp.minimum(jnp.sum((bi[:, None] >= blk_end[None, :]).astype(jnp.int32), axis=1), e - 1)
    n_valid = blk_end[-1:].astype(jnp.int32)
    r = jnp.arange(rows, dtype=jnp.int32)[None, :]
    j = (bi - blk_start[block_e])[:, None] * rows + r
    valid = (j < counts[block_e][:, None]) & (bi < n_valid[0])[:, None]
    a = order[jnp.clip(start[block_e][:, None] + j, 0, a_total - 1)]
    tok = a // TOP_K
    slot = a % TOP_K
    row_src = jnp.where(valid, tok, 0).astype(jnp.int32)
    dump = TOP_K * n + (bi % 2)[:, None] * rows + r
    row_dst = jnp.where(valid, slot * n + tok, dump).astype(jnp.int32)
    live = bi < n_valid[0]
    first = (jnp.concatenate([jnp.ones((1,), bool), block_e[1:] != block_e[:-1]]) & live).astype(jnp.int32)
    wslot = ((jnp.cumsum(first) - 1) % 2).astype(jnp.int32)
    run_start = jnp.where(first > 0, bi, nblk)
    nxt = lax.cummin(jnp.concatenate([run_start[1:], jnp.full((1,), nblk, jnp.int32)]), reverse=True)
    next_e = jnp.where(nxt < nblk, block_e[jnp.minimum(nxt, nblk - 1)], -1).astype(jnp.int32)
    return block_e, n_valid, first, wslot, next_e, row_src, row_dst


def _combine_kernel(*refs):
    slot_refs = refs[:TOP_K]
    wt_ref, hb_ref, h_ref, w1_ref, w3_ref, w2_ref, lg_ref, lb_ref, hn_ref, hbn_ref = refs[TOP_K:]
    tm, d = h_ref.shape
    ky = d // 2 // LANES
    lo = None
    hi = None
    for k, s_ref in enumerate(slot_refs):
        l, u = _unpack_bf16_pair(_load_token_major(s_ref, tm, ky))
        wk = wt_ref[:, k:k + 1]
        lo = wk * l if lo is None else lo + wk * l
        hi = wk * u if hi is None else hi + wk * u
    routed = jnp.concatenate([lo, hi], axis=1)
    x = hb_ref[...]
    a = jnp.dot(x, w1_ref[...], preferred_element_type=F32)
    b = jnp.dot(x, w3_ref[...], preferred_element_type=F32)
    shared = jnp.dot((a * jax.nn.sigmoid(a) * b).astype(BF16), w2_ref[...], preferred_element_type=F32)
    hn = _layer_norm(DEEPNORM_ALPHA * h_ref[...] + routed + shared, lg_ref[...], lb_ref[...])
    hn_ref[...] = hn
    hbn_ref[...] = hn.astype(BF16)


def _combine_shared_ln(slots, wts, hb, h, ws1, ws3, ws2, ln_g, ln_b):
    n, d = h.shape
    tm = 256
    ky = d // 2 // LANES
    f = ws1.shape[1]
    nt = n // tm
    const = lambda i: (0, 0)
    slot_specs = [pl.BlockSpec((tm * ky, LANES), lambda i, k=k: (k * nt + i, 0)) for k in range(TOP_K)]
    return pl.pallas_call(
        _combine_kernel,
        grid=(nt,),
        in_specs=slot_specs + [pl.BlockSpec((tm, LANES), lambda i: (i, 0)),
                               pl.BlockSpec((tm, d), lambda i: (i, 0)),
                               pl.BlockSpec((tm, d), lambda i: (i, 0)),
                               pl.BlockSpec((d, f), const),
                               pl.BlockSpec((d, f), const),
                               pl.BlockSpec((f, d), const),
                               pl.BlockSpec((1, d), const),
                               pl.BlockSpec((1, d), const)],
        out_specs=[pl.BlockSpec((tm, d), lambda i: (i, 0)),
                   pl.BlockSpec((tm, d), lambda i: (i, 0))],
        out_shape=[jax.ShapeDtypeStruct((n, d), F32), jax.ShapeDtypeStruct((n, d), BF16)],
        compiler_params=_cparams(("parallel",)),
        name="combine_shared_ln",
    )(*([slots] * TOP_K), wts, hb, h, ws1, ws3, ws2, ln_g.reshape(1, d), ln_b.reshape(1, d))


def kernel(x, emb_ln_g, emb_ln_b, w_in, b_gate, lru_conv_w, lru_conv_b, lru_wr, lru_br, lru_wi, lru_bi, lru_lambda, ssm_conv_w, ssm_conv_b, ssm_dt_bias, ssm_a_log, ssm_d, ssm_norm_g, w_proj_lru, w_proj_ssm, w_proj_att, w_out, ln1_g, ln1_b, router_w, router_bias, w1, w3, w2, ws1, ws3, ws2, ln2_g, ln2_b):
    batch, seq, d = x.shape
    n = batch * seq
    slopes = (2.0 ** (-8.0 * jnp.arange(1, ATT_Q_HEADS + 1, dtype=F32) / ATT_Q_HEADS)).reshape(ATT_GROUPS, ATT_KV_HEADS)
    h, hb = _embed_ln(x.reshape(n, d), emb_ln_g, emb_ln_b)
    w_in_t = jnp.swapaxes(w_in, 1, 2)
    for l in range(DEPTH):
        proj = _in_proj(hb, w_in_t, l)
        q, kv = _qkv_proj(hb, w_in_t, l, batch, seq)
        y_lru = _rglru(proj, lru_conv_w[l], lru_conv_b[l], lru_wr[l].astype(BF16), lru_br[l],
                       lru_wi[l].astype(BF16), lru_bi[l], lru_lambda[l], batch, seq)
        y_ssm = _ssd(proj, ssm_conv_w[l], ssm_conv_b[l], ssm_dt_bias[l], ssm_a_log[l], ssm_d[l],
                     ssm_norm_g[l], batch, seq)
        y_att = _attention(q, kv, slopes, batch, seq)
        h, hb, h_rows = _merge_outproj_ln(proj, b_gate[l], y_lru, y_ssm, y_att,
                                          w_proj_lru[l].astype(BF16), w_proj_ssm[l].astype(BF16),
                                          w_proj_att[l].astype(BF16), w_out[l].astype(BF16), h, ln1_g[l], ln1_b[l])
        idx, wts, tile_counts = _router(h, router_w[l], router_bias[l])
        counts = jnp.sum(tile_counts.reshape(-1, SUBLANES, LANES)[:, 0, :N_EXPERTS], axis=0).astype(jnp.int32)
        plan = _dispatch_plan(idx[:, :TOP_K], counts, n)
        slots = _routed_experts(h_rows, plan, w1, w3, w2, l)
        h, hb = _combine_shared_ln(slots, wts, hb, h, ws1[l].astype(BF16), ws3[l].astype(BF16),
                                   ws2[l].astype(BF16), ln2_g[l], ln2_b[l])
    return h.reshape(batch, seq, d)
```

```python
import functools
import math

import jax
import jax.numpy as jnp
from jax import lax
from jax.experimental import pallas as pl
from jax.experimental.pallas import tpu as pltpu

F32 = jnp.float32
BF16 = jnp.bfloat16
HIGHEST = lax.Precision.HIGHEST

D_MODEL = 2048
DEPTH = 2
D_RNN = 1024
LRU_BLOCKS = 8
LRU_CONV = 4
LRU_C = 8.0
SSM_D_INNER = 1024
SSM_HEAD_DIM = 64
SSM_HEADS = SSM_D_INNER // SSM_HEAD_DIM
SSM_GROUPS = 2
SSM_D_STATE = 128
SSM_CONV = 4
SSM_CHUNK = 128
SSM_XBC = SSM_D_INNER + 2 * SSM_GROUPS * SSM_D_STATE
ATT_HEAD_DIM = 128
ATT_KV_HEADS = 8
ATT_PATTERNS = ((128, 1), (512, 4), (2048, 16))
ATT_GROUPS = len(ATT_PATTERNS)
ATT_Q_HEADS = ATT_GROUPS * ATT_KV_HEADS
ATT_BLOCK = 128
ATT_D_OUT = ATT_KV_HEADS * ATT_HEAD_DIM
N_BRANCH = 3
N_EXPERTS = 64
EXPERT_DIM = 512
TOP_K = 8
N_EXPERT_GROUPS = 8
TOPK_GROUPS = 4
ROUTED_SCALE = 2.5
DEEPNORM_ALPHA = (2 * DEPTH) ** 0.25
LN_EPS = 1e-5
RMS_EPS = 1e-6

LANES = 128
SUBLANES = 8
VMEM_LIMIT_BYTES = 56 * 1024 * 1024

IN_TILE = 1024
COL_LRU_X = N_BRANCH * D_MODEL
COL_LRU_G = COL_LRU_X + D_RNN
COL_SSM_Z = COL_LRU_G + D_RNN
COL_XBC = COL_SSM_Z + SSM_D_INNER
COL_DT = COL_XBC + SSM_XBC
COL_Q = COL_DT + SSM_HEADS
MAIN_COLS = -(-COL_Q // IN_TILE) * IN_TILE

MOE_ROWS = 256


def _cparams(sem):
    return pltpu.CompilerParams(dimension_semantics=sem, vmem_limit_bytes=VMEM_LIMIT_BYTES)


_NT = (((1,), (1,)), ((), ()))


def _layer_norm(x, g, b):
    mu = jnp.mean(x, axis=-1, keepdims=True)
    xc = x - mu
    var = jnp.mean(xc * xc, axis=-1, keepdims=True)
    return xc * lax.rsqrt(var + LN_EPS) * g + b


def _store_token_major(ref, val):
    rows, w = val.shape
    k = w // LANES
    for j in range(k):
        ref[pl.ds(j, rows, stride=k), :] = val[:, j * LANES:(j + 1) * LANES]


def _load_token_major(ref, rows, k):
    return jnp.concatenate([ref[pl.ds(j, rows, stride=k), :] for j in range(k)], axis=1)


def _ln_kernel(x_ref, g_ref, b_ref, h_ref, hb_ref):
    y = _layer_norm(x_ref[...], g_ref[...], b_ref[...])
    h_ref[...] = y
    hb_ref[...] = y.astype(BF16)


def _embed_ln(x2d, g, b):
    n, d = x2d.shape
    tm = 512
    return pl.pallas_call(
        _ln_kernel,
        grid=(n // tm,),
        in_specs=[pl.BlockSpec((tm, d), lambda i: (i, 0)),
                  pl.BlockSpec((1, d), lambda i: (0, 0)),
                  pl.BlockSpec((1, d), lambda i: (0, 0))],
        out_specs=[pl.BlockSpec((tm, d), lambda i: (i, 0)),
                   pl.BlockSpec((tm, d), lambda i: (i, 0))],
        out_shape=[jax.ShapeDtypeStruct((n, d), F32), jax.ShapeDtypeStruct((n, d), BF16)],
        compiler_params=_cparams(("parallel",)),
        name="embed_ln",
    )(x2d, g.reshape(1, d), b.reshape(1, d))


def _in_proj_kernel(a_ref, wt_ref, o_ref, wb_ref):
    @pl.when(pl.program_id(1) == 0)
    def _():
        wb_ref[...] = wt_ref[...].astype(BF16)

    o_ref[...] = lax.dot_general(a_ref[...], wb_ref[...], _NT, preferred_element_type=F32).astype(o_ref.dtype)


def _in_proj(hb, w_in_t, layer):
    m, k = hb.shape
    tm = 1024
    tn = IN_TILE
    return pl.pallas_call(
        _in_proj_kernel,
        grid=(MAIN_COLS // tn, m // tm),
        in_specs=[pl.BlockSpec((tm, k), lambda j, i: (i, 0)),
                  pl.BlockSpec((None, tn, k), lambda j, i: (layer, j, 0))],
        out_specs=pl.BlockSpec((tm, tn), lambda j, i: (i, j)),
        out_shape=jax.ShapeDtypeStruct((m, MAIN_COLS), BF16),
        scratch_shapes=[pltpu.VMEM((tn, k), BF16)],
        compiler_params=_cparams(("parallel", "arbitrary")),
        name="in_proj",
    )(hb, w_in_t)


def _store_dilated(acc_ref, dst_ref, d):
    nc, t, _ = acc_ref.shape
    u = t // d
    for c in range(nc):
        cols = slice(c * LANES, (c + 1) * LANES)
        if d == 1:
            dst_ref[:, cols] = acc_ref[c].astype(dst_ref.dtype)
            continue
        for r in range(d):
            dst_ref[r * u:(r + 1) * u, cols] = acc_ref[c, pl.ds(r, u, stride=d), :].astype(dst_ref.dtype)


def _load_weight_rows(wt_hbm, wf_ref, wb_ref, sem, layer, row0):
    cp = pltpu.make_async_copy(wt_hbm.at[layer, pl.ds(row0, wf_ref.shape[0]), :], wf_ref, sem)
    cp.start()
    cp.wait()
    wb_ref[...] = wf_ref[...].astype(BF16)


def _dot_to_lane_tiles(a_ref, wb_ref, acc_ref):
    res = lax.dot_general(a_ref[...], wb_ref[...], _NT, preferred_element_type=F32)
    for c in range(acc_ref.shape[0]):
        acc_ref[c] = res[:, c * LANES:(c + 1) * LANES]


def _q_proj_kernel(a_ref, wt_hbm, o_ref, acc_ref, wf_ref, wb_ref, sem, *, layer, row0, tiles_per_group):
    j = pl.program_id(0)

    @pl.when(pl.program_id(1) == 0)
    def _():
        _load_weight_rows(wt_hbm, wf_ref, wb_ref, sem, layer, pl.multiple_of(row0 + j * wf_ref.shape[0], SUBLANES))

    _dot_to_lane_tiles(a_ref, wb_ref, acc_ref)
    for gi, (_, d) in enumerate(ATT_PATTERNS):
        @pl.when(j // tiles_per_group == gi)
        def _(d=d):
            _store_dilated(acc_ref, o_ref, d)


def _kv_proj_kernel(a_ref, wt_hbm, o_ref, acc_ref, wf_ref, wb_ref, sem, *, layer, row0):
    j = pl.program_id(0)

    @pl.when(pl.program_id(1) == 0)
    def _():
        _load_weight_rows(wt_hbm, wf_ref, wb_ref, sem, layer, pl.multiple_of(row0 + j * wf_ref.shape[0], SUBLANES))

    _dot_to_lane_tiles(a_ref, wb_ref, acc_ref)
    for gi, (_, d) in enumerate(ATT_PATTERNS):
        _store_dilated(acc_ref, o_ref.at[gi], d)


def _qkv_proj(hb, w_in_t, layer, batch, seq):
    n, k = hb.shape
    tn = 512
    nq = ATT_Q_HEADS * ATT_HEAD_DIM
    nkv = 2 * ATT_D_OUT
    scratch = [pltpu.VMEM((tn // LANES, seq, LANES), F32),
               pltpu.VMEM((tn, k), F32),
               pltpu.VMEM((tn, k), BF16),
               pltpu.SemaphoreType.DMA(())]
    q = pl.pallas_call(
        functools.partial(_q_proj_kernel, layer=layer, row0=COL_Q, tiles_per_group=(nq // ATT_GROUPS) // tn),
        grid=(nq // tn, batch),
        in_specs=[pl.BlockSpec((seq, k), lambda j, b: (b, 0)),
                  pl.BlockSpec(memory_space=pl.ANY)],
        out_specs=pl.BlockSpec((seq, tn), lambda j, b: (b, j)),
        out_shape=jax.ShapeDtypeStruct((n, nq), BF16),
        scratch_shapes=scratch,
        compiler_params=_cparams(("parallel", "arbitrary")),
        name="q_proj",
    )(hb, w_in_t)
    kv = pl.pallas_call(
        functools.partial(_kv_proj_kernel, layer=layer, row0=COL_Q + nq),
        grid=(nkv // tn, batch),
        in_specs=[pl.BlockSpec((seq, k), lambda j, b: (b, 0)),
                  pl.BlockSpec(memory_space=pl.ANY)],
        out_specs=pl.BlockSpec((ATT_GROUPS, seq, tn), lambda j, b: (0, b, j)),
        out_shape=jax.ShapeDtypeStruct((ATT_GROUPS, n, nkv), BF16),
        scratch_shapes=scratch,
        compiler_params=_cparams(("parallel", "arbitrary")),
        name="kv_proj",
    )(hb, w_in_t)
    return q, kv


ATT_BATCH = 8


def _attn_kernel(slopes_ref, q0_ref, q1_ref, q2_ref, k0_ref, k1_ref, k2_ref,
                 v0_ref, v1_ref, v2_ref, o_ref, res_scr, lse_scr):
    h = pl.program_id(1)
    t = o_ref.shape[0]
    blk = ATT_BLOCK
    scale = ATT_HEAD_DIM ** -0.5
    qi = lax.broadcasted_iota(jnp.int32, (blk, 2 * blk), 0)
    kj = lax.broadcasted_iota(jnp.int32, (blk, 2 * blk), 1)
    dist = blk + qi - kj
    q_refs = (q0_ref, q1_ref, q2_ref)
    k_refs = (k0_ref, k1_ref, k2_ref)
    v_refs = (v0_ref, v1_ref, v2_ref)
    for g, (window, d) in enumerate(ATT_PATTERNS):
        reach = window // d
        assert reach <= blk
        valid = (dist >= 0) & (dist <= reach)
        slope = slopes_ref[g, h]
        bias = jnp.where(valid, -(slope * d) * dist.astype(F32), -jnp.inf)
        bias_cur = bias[:, blk:]
        nb = (t // d) // blk
        q_ref, k_ref, v_ref = q_refs[g], k_refs[g], v_refs[g]
        for f0 in range(0, t // blk, ATT_BATCH):
            fs = list(range(f0, f0 + ATT_BATCH))
            scores, values = [], []
            for f in fs:
                qb = q_ref[f * blk:(f + 1) * blk, :]
                if f % nb > 0:
                    rows, b = slice((f - 1) * blk, (f + 1) * blk), bias
                else:
                    rows, b = slice(f * blk, (f + 1) * blk), bias_cur
                scores.append(lax.dot_general(qb, k_ref[rows, :], _NT, preferred_element_type=F32) * scale + b)
                values.append(v_ref[rows, :])
            ms = [jnp.max(s, axis=-1, keepdims=True) for s in scores]
            ps = [jnp.exp(s - m) for s, m in zip(scores, ms)]
            ls = [jnp.sum(p, axis=-1, keepdims=True) for p in ps]
            pbs = [p.astype(BF16) for p in ps]
            for f, pb, vc, m, l in zip(fs, pbs, values, ms, ls):
                r, i = divmod(f, nb)
                rows = slice(f * blk, (f + 1) * blk) if d == 1 else pl.ds(r + d * blk * i, blk, stride=d)
                res_scr[g, rows, :] = jnp.dot(pb, vc, preferred_element_type=F32) / l
                lse_scr[g, rows, :] = jnp.broadcast_to(m + jnp.log(l), (blk, LANES))
    ch = 256
    for c in range(t // ch):
        rows = slice(c * ch, (c + 1) * ch)
        lses = [lse_scr[g, rows, :] for g in range(ATT_GROUPS)]
        mx = functools.reduce(jnp.maximum, lses)
        ws = [jnp.exp(x - mx) for x in lses]
        num = sum(w * res_scr[g, rows, :] for g, w in enumerate(ws))
        o_ref[rows, :] = (num / sum(ws)).astype(o_ref.dtype)


def _attention(q, kv, slopes, batch, seq):
    n = q.shape[0]
    hd = ATT_HEAD_DIM
    nh = ATT_KV_HEADS
    q_specs = [pl.BlockSpec((seq, hd), lambda b, h, g=g: (b, g * nh + h)) for g in range(ATT_GROUPS)]
    k_specs = [pl.BlockSpec((None, seq, hd), lambda b, h, g=g: (g, b, h)) for g in range(ATT_GROUPS)]
    v_specs = [pl.BlockSpec((None, seq, hd), lambda b, h, g=g: (g, b, nh + h)) for g in range(ATT_GROUPS)]
    return pl.pallas_call(
        _attn_kernel,
        grid=(batch, nh),
        in_specs=[pl.BlockSpec(memory_space=pltpu.SMEM)] + q_specs + k_specs + v_specs,
        out_specs=pl.BlockSpec((seq, hd), lambda b, h: (b, h)),
        out_shape=jax.ShapeDtypeStruct((n, nh * hd), BF16),
        scratch_shapes=[pltpu.VMEM((ATT_GROUPS, seq, hd), F32),
                        pltpu.VMEM((ATT_GROUPS, seq, LANES), F32)],
        compiler_params=_cparams(("parallel", "parallel")),
        name="dilated_attention",
    )(slopes, q, q, q, kv, kv, kv, kv, kv, kv)


def _scan8(a, u, carry, row):
    for s in (1, 2, 4):
        a_sh = pltpu.roll(a, s, axis=0)
        u_sh = pltpu.roll(u, s, axis=0)
        m = row >= s
        u = jnp.where(m, a * u_sh + u, u)
        a = jnp.where(m, a * a_sh, a)
    return u + a * carry


def _lru_kernel(x_ref, g_ref, cw_ref, cb_ref, wr_ref, br_ref, wi_ref, bi_ref, lam_ref, o_ref,
                xpad_scr, a_scr, u_scr):
    t, c = o_ref.shape
    nb = LRU_BLOCKS
    bs = c // nb
    ch = 256
    pad = SUBLANES
    xpad_scr[0:pad, :] = jnp.zeros((pad, c), F32)
    for k in range(t // ch):
        xpad_scr[pad + k * ch:pad + (k + 1) * ch, :] = x_ref[k * ch:(k + 1) * ch, :].astype(F32)
    neg_lam = -lam_ref[...]
    sp = jnp.maximum(neg_lam, 0.0) + jnp.log1p(jnp.exp(-jnp.abs(neg_lam)))
    for k in range(t // ch):
        base = k * ch
        xc = cb_ref[...] + sum(
            cw_ref[j:j + 1, :] * xpad_scr[base + pad - (LRU_CONV - 1) + j:base + pad - (LRU_CONV - 1) + j + ch, :]
            for j in range(LRU_CONV))
        xcb = xc.astype(BF16)
        for n in range(nb):
            cols = slice(n * bs, (n + 1) * bs)
            xn = xcb[:, cols]
            r = jax.nn.sigmoid(jnp.dot(xn, wr_ref[n], preferred_element_type=F32) + br_ref[:, cols])
            ig = jax.nn.sigmoid(jnp.dot(xn, wi_ref[n], preferred_element_type=F32) + bi_ref[:, cols])
            log_a = -LRU_C * r * sp[:, cols]
            th = jnp.tanh(log_a)
            a_scr[base:base + ch, cols] = jnp.exp(log_a)
            u_scr[base:base + ch, cols] = jnp.sqrt(-2.0 * th / (1.0 - th)) * ig * xc[:, cols]
    row = lax.broadcasted_iota(jnp.int32, (SUBLANES, c), 0)

    def step(j, carry):
        rows = pl.ds(pl.multiple_of(j * SUBLANES, SUBLANES), SUBLANES)
        h8 = _scan8(a_scr[rows, :], u_scr[rows, :], carry, row)
        u_scr[rows, :] = h8
        return jnp.broadcast_to(h8[SUBLANES - 1:SUBLANES, :], (SUBLANES, c))

    lax.fori_loop(0, t // SUBLANES, step, jnp.zeros((SUBLANES, c), F32))
    for k in range(t // ch):
        rows = slice(k * ch, (k + 1) * ch)
        o_ref[rows, :] = (u_scr[rows, :] * jax.nn.gelu(g_ref[rows, :].astype(F32))).astype(o_ref.dtype)


def _rglru(proj, cw, cb, wr, br, wi, bi, lam, batch, seq):
    n = proj.shape[0]
    c = D_RNN
    bs = c // LRU_BLOCKS
    row = lambda b: (0, 0)
    return pl.pallas_call(
        _lru_kernel,
        grid=(batch,),
        in_specs=[pl.BlockSpec((seq, c), lambda b: (b, COL_LRU_X // c)),
                  pl.BlockSpec((seq, c), lambda b: (b, COL_LRU_G // c)),
                  pl.BlockSpec((LRU_CONV, c), row),
                  pl.BlockSpec((1, c), row),
                  pl.BlockSpec((LRU_BLOCKS, bs, bs), lambda b: (0, 0, 0)),
                  pl.BlockSpec((1, c), row),
                  pl.BlockSpec((LRU_BLOCKS, bs, bs), lambda b: (0, 0, 0)),
                  pl.BlockSpec((1, c), row),
                  pl.BlockSpec((1, c), row)],
        out_specs=pl.BlockSpec((seq, c), lambda b: (b, 0)),
        out_shape=jax.ShapeDtypeStruct((n, c), BF16),
        scratch_shapes=[pltpu.VMEM((seq + SUBLANES, c), F32),
                        pltpu.VMEM((seq, c), F32),
                        pltpu.VMEM((seq, c), F32)],
        compiler_params=_cparams(("parallel",)),
        name="rglru",
    )(proj, proj, cw, cb.reshape(1, c), wr, br.reshape(1, c), wi, bi.reshape(1, c), lam.reshape(1, c))


def _ssd_kernel(z_ref, xbc_ref, dtr_ref, cw_ref, cb_ref, dtb_ref, alog_ref, dskip_ref, ng_ref, o_ref,
                xpad_scr, st_scr):
    L = o_ref.shape[0]
    di = SSM_D_INNER
    ns = SSM_D_STATE
    pad = SUBLANES
    c = pl.program_id(1)

    @pl.when(c == 0)
    def _():
        xpad_scr[0:pad, :] = jnp.zeros((pad, SSM_XBC), F32)
        st_scr[...] = jnp.zeros(st_scr.shape, F32)

    xpad_scr[pad:pad + L, :] = xbc_ref[...].astype(F32)
    xc = cb_ref[...] + sum(
        cw_ref[j:j + 1, :] * xpad_scr[pad - (SSM_CONV - 1) + j:pad - (SSM_CONV - 1) + j + L, :]
        for j in range(SSM_CONV))
    xpad_scr[0:pad, :] = xpad_scr[L:L + pad, :]
    xc = xc * jax.nn.sigmoid(xc)
    xs = xc[:, :di]

    dt_in = dtr_ref[...].astype(F32) + dtb_ref[...]
    dt = jnp.maximum(dt_in, 0.0) + jnp.log1p(jnp.exp(-jnp.abs(dt_in)))
    adt = dt * (-jnp.exp(alog_ref[...]))
    ri = lax.broadcasted_iota(jnp.int32, (L, L), 0)
    ci = lax.broadcasted_iota(jnp.int32, (L, L), 1)
    causal = ri >= ci
    acum = jnp.dot(causal.astype(F32), adt, precision=HIGHEST, preferred_element_type=F32)
    acum_t = acum.T
    a_last = acum[L - 1:L, :]
    hl = lax.broadcasted_iota(jnp.int32, (LANES, di), 0)
    cl = lax.broadcasted_iota(jnp.int32, (LANES, di), 1)
    expand = (cl // SSM_HEAD_DIM == hl).astype(F32)
    dt_c = jnp.dot(dt, expand, precision=HIGHEST, preferred_element_type=F32)
    ea_c = jnp.dot(jnp.exp(acum), expand, precision=HIGHEST, preferred_element_type=F32)
    ds_c = jnp.dot(jnp.exp(a_last - acum), expand, precision=HIGHEST, preferred_element_type=F32)
    xdt = xs * dt_c
    xdt_b = xdt.astype(BF16)
    xw_b = (xdt * ds_c).astype(BF16)
    lane = lax.broadcasted_iota(jnp.int32, (L, LANES), 1)
    lo = lane < SSM_HEAD_DIM
    nt = (((1,), (1,)), ((), ()))
    heads_per_group = SSM_HEADS // SSM_GROUPS
    ys = []
    for g in range(SSM_GROUPS):
        bm = xc[:, di + g * ns:di + (g + 1) * ns]
        cm = xc[:, di + SSM_GROUPS * ns + g * ns:di + SSM_GROUPS * ns + (g + 1) * ns]
        bm_b = bm.astype(BF16)
        cm_b = cm.astype(BF16)
        bm_t = bm.T.astype(BF16)
        cb = lax.dot_general(cm_b, bm_b, nt, preferred_element_type=F32)
        for jp in range(heads_per_group // 2):
            j = g * (heads_per_group // 2) + jp
            cols = slice(j * LANES, (j + 1) * LANES)
            ms = []
            for hh in (2 * j, 2 * j + 1):
                seg = acum[:, hh:hh + 1] - acum_t[hh:hh + 1, :]
                decay = jnp.exp(jnp.where(causal, seg, -jnp.inf))
                ms.append((cb * decay).astype(BF16))
            mcat = jnp.concatenate(ms, axis=1)
            xp = xdt_b[:, cols]
            zero = jnp.zeros_like(xp)
            xcat = jnp.concatenate([jnp.where(lo, xp, zero), jnp.where(lo, zero, xp)], axis=0)
            y_diag = jnp.dot(mcat, xcat, preferred_element_type=F32)
            ent = st_scr[j]
            y_off = jnp.dot(cm_b, ent.astype(BF16), preferred_element_type=F32) * ea_c[:, cols]
            st_new = jnp.dot(bm_t, xw_b[:, cols], preferred_element_type=F32)
            st_scr[j] = st_new + ea_c[L - 1:L, cols] * ent
            ys.append(y_diag + y_off)
    y = jnp.concatenate(ys, axis=1) + dskip_ref[...] * xs
    zf = z_ref[...].astype(F32)
    y = y * (zf * jax.nn.sigmoid(zf))
    gw = di // SSM_GROUPS
    outs = []
    for g in range(SSM_GROUPS):
        yg = y[:, g * gw:(g + 1) * gw]
        outs.append(yg * lax.rsqrt(jnp.mean(yg * yg, axis=-1, keepdims=True) + RMS_EPS))
    o_ref[...] = (jnp.concatenate(outs, axis=1) * ng_ref[...]).astype(o_ref.dtype)


def _ssd(proj, cw, cb, dt_bias, a_log, d_skip, norm_g, batch, seq):
    n = proj.shape[0]
    L = SSM_CHUNK
    nc = seq // L
    di = SSM_D_INNER
    pad_h = LANES - SSM_HEADS
    dtb = jnp.pad(dt_bias, (0, pad_h)).reshape(1, LANES)
    alog = jnp.pad(a_log, (0, pad_h)).reshape(1, LANES)
    dskip = jnp.repeat(d_skip, SSM_HEAD_DIM).reshape(1, di)
    const = lambda b, c: (0, 0)
    return pl.pallas_call(
        _ssd_kernel,
        grid=(batch, nc),
        in_specs=[pl.BlockSpec((L, di), lambda b, c: (b * nc + c, COL_SSM_Z // di)),
                  pl.BlockSpec((L, SSM_XBC), lambda b, c: (b * nc + c, COL_XBC // SSM_XBC)),
                  pl.BlockSpec((L, LANES), lambda b, c: (b * nc + c, COL_DT // LANES)),
                  pl.BlockSpec((SSM_CONV, SSM_XBC), const),
                  pl.BlockSpec((1, SSM_XBC), const),
                  pl.BlockSpec((1, LANES), const),
                  pl.BlockSpec((1, LANES), const),
                  pl.BlockSpec((1, di), const),
                  pl.BlockSpec((1, di), const)],
        out_specs=pl.BlockSpec((L, di), lambda b, c: (b * nc + c, 0)),
        out_shape=jax.ShapeDtypeStruct((n, di), BF16),
        scratch_shapes=[pltpu.VMEM((L + SUBLANES, SSM_XBC), F32),
                        pltpu.VMEM((SSM_HEADS // 2, SSM_D_STATE, LANES), F32)],
        compiler_params=_cparams(("parallel", "arbitrary")),
        name="ssd",
    )(proj, proj, proj, cw, cb.reshape(1, SSM_XBC), dtb, alog, dskip, norm_g.reshape(1, di))


def _merge_kernel(g_ref, bg_ref, yl_ref, ys_ref, ya_ref, wl_ref, ws_ref, wa_ref, wo_ref,
                  h_ref, lg_ref, lb_ref, hn_ref, hb_ref, hrows_ref):
    d = h_ref.shape[1]
    merged = None
    for i, (y_ref, w_ref) in enumerate(((yl_ref, wl_ref), (ys_ref, ws_ref), (ya_ref, wa_ref))):
        gate = jax.nn.sigmoid(g_ref[:, i * d:(i + 1) * d].astype(F32) + bg_ref[:, i * d:(i + 1) * d])
        term = gate * jnp.dot(y_ref[...], w_ref[...], preferred_element_type=F32)
        merged = term if merged is None else merged + term
    mix = jnp.dot(merged.astype(BF16), wo_ref[...], preferred_element_type=F32)
    hn = _layer_norm(DEEPNORM_ALPHA * h_ref[...] + mix, lg_ref[...], lb_ref[...])
    hn_ref[...] = hn
    hb_ref[...] = hn.astype(BF16)
    _store_token_major(hrows_ref, hn)


def _merge_outproj_ln(proj, b_gate, y_lru, y_ssm, y_att, wl, ws, wa, wo, h, ln_g, ln_b):
    n, d = h.shape
    tm = 256
    gw = N_BRANCH * d
    kb = y_lru.shape[1]
    const = lambda i: (0, 0)
    once = pl.Buffered(1)
    return pl.pallas_call(
        _merge_kernel,
        grid=(n // tm,),
        in_specs=[pl.BlockSpec((tm, gw), lambda i: (i, 0)),
                  pl.BlockSpec((1, gw), const),
                  pl.BlockSpec((tm, kb), lambda i: (i, 0)),
                  pl.BlockSpec((tm, kb), lambda i: (i, 0)),
                  pl.BlockSpec((tm, kb), lambda i: (i, 0)),
                  pl.BlockSpec((kb, d), const, pipeline_mode=once),
                  pl.BlockSpec((kb, d), const, pipeline_mode=once),
                  pl.BlockSpec((kb, d), const, pipeline_mode=once),
                  pl.BlockSpec((d, d), const, pipeline_mode=once),
                  pl.BlockSpec((tm, d), lambda i: (i, 0)),
                  pl.BlockSpec((1, d), const),
                  pl.BlockSpec((1, d), const)],
        out_specs=[pl.BlockSpec((tm, d), lambda i: (i, 0)),
                   pl.BlockSpec((tm, d), lambda i: (i, 0)),
                   pl.BlockSpec((tm * (d // LANES), LANES), lambda i: (i, 0))],
        out_shape=[jax.ShapeDtypeStruct((n, d), F32), jax.ShapeDtypeStruct((n, d), BF16),
                   jax.ShapeDtypeStruct((n * (d // LANES), LANES), F32)],
        compiler_params=_cparams(("parallel",)),
        name="merge_outproj_ln",
    )(proj, b_gate.reshape(1, gw), y_lru, y_ssm, y_att, wl, ws, wa, wo, h,
      ln_g.reshape(1, d), ln_b.reshape(1, d))


def _seg_reduce(v, lane, op):
    for s in (1, 2, 4):
        up = pltpu.roll(v, LANES - s, axis=1)
        dn = pltpu.roll(v, s, axis=1)
        v = op(v, jnp.where((lane & s) == 0, up, dn))
    return v


def _router_kernel(h_ref, w_ref, b_ref, idx_ref, wt_ref, cnt_ref):
    tm = h_ref.shape[0]
    logits = jnp.dot(h_ref[...], w_ref[...], precision=HIGHEST, preferred_element_type=F32)
    scores = jax.nn.sigmoid(logits)
    lane = lax.broadcasted_iota(jnp.int32, (tm, LANES), 1)
    lane_f = lane.astype(F32)
    real = lane < N_EXPERTS
    neg = -jnp.inf
    choice = jnp.where(real, scores + b_ref[...], neg)
    per_group = N_EXPERTS // N_EXPERT_GROUPS
    assert per_group == 8
    m1 = _seg_reduce(choice, lane, jnp.maximum)
    first = _seg_reduce(jnp.where(choice == m1, lane_f, float(LANES)), lane, jnp.minimum)
    m2 = _seg_reduce(jnp.where(lane_f == first, neg, choice), lane, jnp.maximum)
    gs = m1 + m2
    gidx = lane // per_group
    n_slots = LANES // per_group
    beaten = jnp.zeros((tm, LANES), jnp.int32)
    for k in range(1, n_slots):
        other = pltpu.roll(gs, per_group * k, axis=1)
        og = (gidx - k) & (n_slots - 1)
        wins = (other > gs) | ((other == gs) & (og < gidx))
        beaten = beaten + wins.astype(jnp.int32)
    masked = jnp.where((beaten < TOPK_GROUPS) & real, choice, neg)
    sel_i = jnp.zeros((tm, LANES), F32)
    sel_w = jnp.zeros((tm, LANES), F32)
    picked = jnp.zeros((tm, LANES), F32)
    for k in range(TOP_K):
        m = jnp.max(masked, axis=1, keepdims=True)
        am = jnp.min(jnp.where(masked == m, lane_f, float(LANES)), axis=1, keepdims=True)
        hit = lane_f == am
        wk = jnp.sum(jnp.where(hit, scores, 0.0), axis=1, keepdims=True)
        sel_i = jnp.where(lane == k, am, sel_i)
        sel_w = jnp.where(lane == k, wk, sel_w)
        picked = picked + jnp.where(hit, 1.0, 0.0)
        masked = jnp.where(hit, neg, masked)
    wsum = jnp.sum(sel_w, axis=1, keepdims=True)
    idx_ref[...] = sel_i.astype(jnp.int32)
    wt_ref[...] = sel_w / wsum * ROUTED_SCALE
    cnt_ref[...] = jnp.broadcast_to(jnp.sum(picked, axis=0, keepdims=True), cnt_ref.shape)


def _router(h, router_w, router_bias):
    n, d = h.shape
    tm = 512
    pad_e = LANES - N_EXPERTS
    w = jnp.pad(router_w, ((0, 0), (0, pad_e)))
    b = jnp.pad(router_bias, (0, pad_e)).reshape(1, LANES)
    return pl.pallas_call(
        _router_kernel,
        grid=(n // tm,),
        in_specs=[pl.BlockSpec((tm, d), lambda i: (i, 0)),
                  pl.BlockSpec((d, LANES), lambda i: (0, 0)),
                  pl.BlockSpec((1, LANES), lambda i: (0, 0))],
        out_specs=[pl.BlockSpec((tm, LANES), lambda i: (i, 0)),
                   pl.BlockSpec((tm, LANES), lambda i: (i, 0)),
                   pl.BlockSpec((SUBLANES, LANES), lambda i: (i, 0))],
        out_shape=[jax.ShapeDtypeStruct((n, LANES), jnp.int32), jax.ShapeDtypeStruct((n, LANES), F32),
                   jax.ShapeDtypeStruct((n // tm * SUBLANES, LANES), F32)],
        compiler_params=_cparams(("parallel",)),
        name="moe_router",
    )(h, w, b)


def _pack_bf16_pair(lo, hi):
    lo_bits = lax.bitcast_convert_type(lo.astype(BF16).astype(F32), jnp.uint32)
    hi_bits = lax.bitcast_convert_type(hi.astype(BF16).astype(F32), jnp.uint32)
    return (hi_bits & jnp.uint32(0xFFFF0000)) | (lo_bits >> 16)


def _unpack_bf16_pair(w):
    lo = lax.bitcast_convert_type(w << 16, F32)
    hi = lax.bitcast_convert_type(w & jnp.uint32(0xFFFF0000), F32)
    return lo, hi


def _expert_kernel(be_ref, nv_ref, first_ref, wslot_ref, nexte_ref, src0_ref, src1_ref, src2_ref, dst_ref,
                   h_hbm, w1_hbm, w3_hbm, w2_hbm, slots_hbm,
                   xbuf, ybuf, wf1, wf3, wf2, w1_ref, w3_ref, w2_ref, gsem, ssem, wsem, *, layer):
    i = pl.program_id(0)
    n_valid = nv_ref[0]
    src_refs = (src0_ref, src1_ref, src2_ref)

    def weight_copies(e, ws):
        return (pltpu.make_async_copy(w1_hbm.at[layer, e], wf1.at[ws], wsem.at[ws]),
                pltpu.make_async_copy(w3_hbm.at[layer, e], wf3.at[ws], wsem.at[ws]),
                pltpu.make_async_copy(w2_hbm.at[layer, e], wf2.at[ws], wsem.at[ws]))

    @pl.when(i == 0)
    def _():
        for c in weight_copies(be_ref[0], 0):
            c.start(priority=1)

    for ws in range(2):
        @pl.when((i < n_valid) & (first_ref[i] == 1) & (wslot_ref[i] == ws))
        def _(ws=ws):
            for c in weight_copies(0, ws):
                c.wait()

            @pl.when(nexte_ref[i] >= 0)
            def _():
                for c in weight_copies(nexte_ref[i], 1 - ws):
                    c.start(priority=1)

            w1_ref[...] = wf1[ws].astype(BF16)
            w3_ref[...] = wf3[ws].astype(BF16)
            w2_ref[...] = wf2[ws].astype(BF16)

    rows = dst_ref.shape[2]
    kx = xbuf.shape[1] // rows
    ky = ybuf.shape[1] // rows
    n_real = slots_hbm.shape[0] - 2 * rows * ky
    depth = xbuf.shape[0]
    slot = lax.rem(i, depth)

    def gather(idx_ref, s, r):
        tok = idx_ref[0, 0, r]
        return pltpu.make_async_copy(h_hbm.at[pl.ds(pl.multiple_of(tok * kx, kx), kx), :],
                                     xbuf.at[s, pl.ds(pl.multiple_of(r * kx, kx), kx), :], gsem.at[s])

    def scatter(s, r):
        dst = dst_ref[0, 0, r]
        return pltpu.make_async_copy(ybuf.at[s, pl.ds(pl.multiple_of(r * ky, ky), ky), :],
                                     slots_hbm.at[pl.ds(pl.multiple_of(dst * ky, ky), ky), :], ssem.at[s])

    def wait_gathers(s):
        pltpu.make_async_copy(h_hbm.at[pl.ds(0, rows * kx), :], xbuf.at[s], gsem.at[s]).wait()

    def wait_scatters(s):
        pltpu.make_async_copy(ybuf.at[s], slots_hbm.at[pl.ds(0, rows * ky), :], ssem.at[s]).wait()

    @pl.when(i == 0)
    def _():
        ybuf[0] = jnp.zeros((rows * ky, LANES), jnp.uint32)
        for p in range(2):
            pltpu.make_async_copy(ybuf.at[0], slots_hbm.at[pl.ds(n_real + p * rows * ky, rows * ky), :],
                                  ssem.at[0]).start()
        for p in range(2):
            pltpu.make_async_copy(ybuf.at[0], slots_hbm.at[pl.ds(n_real + p * rows * ky, rows * ky), :],
                                  ssem.at[0]).wait()
        for blk, idx_ref in enumerate(src_refs[:depth - 1]):
            def body(r, c, blk=blk, idx_ref=idx_ref):
                gather(idx_ref, blk, r).start()
                return c
            lax.fori_loop(0, rows, body, 0, unroll=8)

    def step(s):
        ahead = (s + depth - 1) % depth
        wait_gathers(s)
        x = _load_token_major(xbuf.at[s], rows, kx).astype(BF16)
        for r in range(rows):
            gather(src_refs[depth - 1], ahead, r).start()
        a = jnp.dot(x, w1_ref[...], preferred_element_type=F32)
        b = jnp.dot(x, w3_ref[...], preferred_element_type=F32)
        hb = (a * jax.nn.sigmoid(a) * b).astype(BF16)
        y = jnp.dot(hb, w2_ref[...], preferred_element_type=F32)
        half = y.shape[1] // 2
        _store_token_major(ybuf.at[s], _pack_bf16_pair(y[:, :half], y[:, half:]))

        @pl.when(i >= 1)
        def _():
            wait_scatters(ahead)

        for r in range(rows):
            scatter(s, r).start(priority=r % 2)

        @pl.when(i + 1 == n_valid)
        def _():
            wait_scatters(s)
            for k in range(1, depth):
                wait_gathers((s + k) % depth)

    for s in range(depth):
        @pl.when((i < n_valid) & (slot == s))
        def _(s=s):
            step(s)


def _routed_experts(h_rows, plan, w1, w3, w2, layer):
    block_e, n_valid, first, wslot, next_e, row_src, row_dst = plan
    d = w1.shape[2]
    kx = d // LANES
    ky = d // 2 // LANES
    n = h_rows.shape[0] // kx
    nblk = block_e.shape[0]
    rows = MOE_ROWS
    f = w1.shape[3]
    idx_block = (1, 1, rows)
    here = lambda i, *_: (i, 0, 0)
    ahead1 = lambda i, *_: (jnp.minimum(i + 1, nblk - 1), 0, 0)
    ahead2 = lambda i, *_: (jnp.minimum(i + 2, nblk - 1), 0, 0)
    grid_spec = pltpu.PrefetchScalarGridSpec(
        num_scalar_prefetch=5,
        grid=(nblk,),
        in_specs=[pl.BlockSpec(idx_block, here, memory_space=pltpu.SMEM),
                  pl.BlockSpec(idx_block, ahead1, memory_space=pltpu.SMEM),
                  pl.BlockSpec(idx_block, ahead2, memory_space=pltpu.SMEM),
                  pl.BlockSpec(idx_block, here, memory_space=pltpu.SMEM),
                  pl.BlockSpec(memory_space=pl.ANY),
                  pl.BlockSpec(memory_space=pl.ANY),
                  pl.BlockSpec(memory_space=pl.ANY),
                  pl.BlockSpec(memory_space=pl.ANY)],
        out_specs=pl.BlockSpec(memory_space=pl.ANY),
        scratch_shapes=[pltpu.VMEM((3, rows * kx, LANES), F32),
                        pltpu.VMEM((3, rows * ky, LANES), jnp.uint32),
                        pltpu.VMEM((2, d, f), F32),
                        pltpu.VMEM((2, d, f), F32),
                        pltpu.VMEM((2, f, d), F32),
                        pltpu.VMEM((d, f), BF16),
                        pltpu.VMEM((d, f), BF16),
                        pltpu.VMEM((f, d), BF16),
                        pltpu.SemaphoreType.DMA((3,)),
                        pltpu.SemaphoreType.DMA((3,)),
                        pltpu.SemaphoreType.DMA((2,))],
    )
    src3 = row_src.reshape(nblk, 1, rows)
    return pl.pallas_call(
        functools.partial(_expert_kernel, layer=layer),
        grid_spec=grid_spec,
        out_shape=jax.ShapeDtypeStruct(((TOP_K * n + 2 * rows) * ky, LANES), jnp.uint32),
        compiler_params=_cparams(("arbitrary",)),
        name="routed_experts",
    )(block_e, n_valid, first, wslot, next_e, src3, src3, src3, row_dst.reshape(nblk, 1, rows),
      h_rows, w1, w3, w2)


def _dispatch_plan(idx, counts, n):
    rows = MOE_ROWS
    e = N_EXPERTS
    a_total = n * TOP_K
    nblk = (a_total + e * (rows - 1)) // rows
    order = jnp.argsort(idx.reshape(a_total)).astype(jnp.int32)
    ar = jnp.arange(e, dtype=jnp.int32)
    blocks_e = (counts + rows - 1) // rows
    blk_end = jnp.cumsum(blocks_e)
    blk_start = blk_end - blocks_e
    start = jnp.cumsum(counts) - counts
    run_of = jnp.cumsum((counts > 0).astype(jnp.int32)) - 1
    later = (ar[None, :] > ar[:, None]) & (counts[None, :] > 0)
    next_of = jnp.min(jnp.where(later, ar[None, :], e), axis=1)
    next_of = jnp.where(next_of < e, next_of, -1)
    bi = jnp.arange(nblk, dtype=jnp.int32)
    n_valid = blk_end[-1:].astype(jnp.int32)
    live = bi < n_valid[0]
    block_e = jnp.minimum(jnp.sum((bi[:, None] >= blk_end[None, :]).astype(jnp.int32), axis=1), e - 1)
    onehot = block_e[:, None] == ar[None, :]
    pick = lambda table: jnp.sum(jnp.where(onehot, table[None, :], 0), axis=1)
    r = jnp.arange(rows, dtype=jnp.int32)[None, :]
    j = (bi - pick(blk_start))[:, None] * rows + r
    valid = (j < pick(counts)[:, None]) & live[:, None]
    a = order[jnp.clip(pick(start)[:, None] + j, 0, a_total - 1)]
    tok = a // TOP_K
    slot = a % TOP_K
    row_src = jnp.where(valid, tok, 0).astype(jnp.int32)
    dump = TOP_K * n + (bi % 2)[:, None] * rows + r
    row_dst = jnp.where(valid, slot * n + tok, dump).astype(jnp.int32)
    first = (live & (bi == pick(blk_start))).astype(jnp.int32)
    wslot = (pick(run_of) % 2).astype(jnp.int32)
    next_e = jnp.where(live, pick(next_of), -1).astype(jnp.int32)
    return block_e, n_valid, first, wslot, next_e, row_src, row_dst


def _combine_kernel(*refs):
    slot_refs = refs[:TOP_K]
    wt_ref, hb_ref, h_ref, w1_ref, w3_ref, w2_ref, lg_ref, lb_ref, hn_ref, hbn_ref = refs[TOP_K:]
    tm, d = h_ref.shape
    ky = d // 2 // LANES
    lo = None
    hi = None
    for k, s_ref in enumerate(slot_refs):
        l, u = _unpack_bf16_pair(_load_token_major(s_ref, tm, ky))
        wk = wt_ref[:, k:k + 1]
        lo = wk * l if lo is None else lo + wk * l
        hi = wk * u if hi is None else hi + wk * u
    routed = jnp.concatenate([lo, hi], axis=1)
    x = hb_ref[...]
    a = jnp.dot(x, w1_ref[...], preferred_element_type=F32)
    b = jnp.dot(x, w3_ref[...], preferred_element_type=F32)
    shared = jnp.dot((a * jax.nn.sigmoid(a) * b).astype(BF16), w2_ref[...], preferred_element_type=F32)
    hn = _layer_norm(DEEPNORM_ALPHA * h_ref[...] + routed + shared, lg_ref[...], lb_ref[...])
    hn_ref[...] = hn
    hbn_ref[...] = hn.astype(BF16)


def _combine_shared_ln(slots, wts, hb, h, ws1, ws3, ws2, ln_g, ln_b):
    n, d = h.shape
    tm = 256
    ky = d // 2 // LANES
    f = ws1.shape[1]
    nt = n // tm
    const = lambda i: (0, 0)
    slot_specs = [pl.BlockSpec((tm * ky, LANES), lambda i, k=k: (k * nt + i, 0)) for k in range(TOP_K)]
    return pl.pallas_call(
        _combine_kernel,
        grid=(nt,),
        in_specs=slot_specs + [pl.BlockSpec((tm, LANES), lambda i: (i, 0)),
                               pl.BlockSpec((tm, d), lambda i: (i, 0)),
                               pl.BlockSpec((tm, d), lambda i: (i, 0)),
                               pl.BlockSpec((d, f), const),
                               pl.BlockSpec((d, f), const),
                               pl.BlockSpec((f, d), const),
                               pl.BlockSpec((1, d), const),
                               pl.BlockSpec((1, d), const)],
        out_specs=[pl.BlockSpec((tm, d), lambda i: (i, 0)),
                   pl.BlockSpec((tm, d), lambda i: (i, 0))],
        out_shape=[jax.ShapeDtypeStruct((n, d), F32), jax.ShapeDtypeStruct((n, d), BF16)],
        compiler_params=_cparams(("parallel",)),
        name="combine_shared_ln",
    )(*([slots] * TOP_K), wts, hb, h, ws1, ws3, ws2, ln_g.reshape(1, d), ln_b.reshape(1, d))


def kernel(x, emb_ln_g, emb_ln_b, w_in, b_gate, lru_conv_w, lru_conv_b, lru_wr, lru_br, lru_wi, lru_bi, lru_lambda, ssm_conv_w, ssm_conv_b, ssm_dt_bias, ssm_a_log, ssm_d, ssm_norm_g, w_proj_lru, w_proj_ssm, w_proj_att, w_out, ln1_g, ln1_b, router_w, router_bias, w1, w3, w2, ws1, ws3, ws2, ln2_g, ln2_b):
    batch, seq, d = x.shape
    n = batch * seq
    slopes = (2.0 ** (-8.0 * jnp.arange(1, ATT_Q_HEADS + 1, dtype=F32) / ATT_Q_HEADS)).reshape(ATT_GROUPS, ATT_KV_HEADS)
    h, hb = _embed_ln(x.reshape(n, d), emb_ln_g, emb_ln_b)
    w_in_t = jnp.swapaxes(w_in, 1, 2)
    for l in range(DEPTH):
        proj = _in_proj(hb, w_in_t, l)
        q, kv = _qkv_proj(hb, w_in_t, l, batch, seq)
        y_lru = _rglru(proj, lru_conv_w[l], lru_conv_b[l], lru_wr[l].astype(BF16), lru_br[l],
                       lru_wi[l].astype(BF16), lru_bi[l], lru_lambda[l], batch, seq)
        y_ssm = _ssd(proj, ssm_conv_w[l], ssm_conv_b[l], ssm_dt_bias[l], ssm_a_log[l], ssm_d[l],
                     ssm_norm_g[l], batch, seq)
        y_att = _attention(q, kv, slopes, batch, seq)
        h, hb, h_rows = _merge_outproj_ln(proj, b_gate[l], y_lru, y_ssm, y_att,
                                          w_proj_lru[l].astype(BF16), w_proj_ssm[l].astype(BF16),
                                          w_proj_att[l].astype(BF16), w_out[l].astype(BF16), h, ln1_g[l], ln1_b[l])
        idx, wts, tile_counts = _router(h, router_w[l], router_bias[l])
        counts = jnp.sum(tile_counts.reshape(-1, SUBLANES, LANES)[:, 0, :N_EXPERTS], axis=0).astype(jnp.int32)
        plan = _dispatch_plan(idx[:, :TOP_K], counts, n)
        slots = _routed_experts(h_rows, plan, w1, w3, w2, l)
        h, hb = _combine_shared_ln(slots, wts, hb, h, ws1[l].astype(BF16), ws3[l].astype(BF16),
                                   ws2[l].astype(BF16), ln2_g[l], ln2_b[l])
    return h.reshape(batch, seq, d)
```

```python
import functools

import jax
import jax.numpy as jnp
from jax import lax
from jax.experimental import pallas as pl
from jax.experimental.pallas import tpu as pltpu

F32 = jnp.float32
BF16 = jnp.bfloat16
HIGHEST = lax.Precision.HIGHEST

D_MODEL = 2048
DEPTH = 2
D_RNN = 1024
LRU_BLOCKS = 8
LRU_CONV = 4
LRU_C = 8.0
SSM_D_INNER = 1024
SSM_HEAD_DIM = 64
SSM_HEADS = SSM_D_INNER // SSM_HEAD_DIM
SSM_GROUPS = 2
SSM_D_STATE = 128
SSM_CONV = 4
SSM_CHUNK = 128
SSM_XBC = SSM_D_INNER + 2 * SSM_GROUPS * SSM_D_STATE
ATT_HEAD_DIM = 128
ATT_KV_HEADS = 8
ATT_PATTERNS = ((128, 1), (512, 4), (2048, 16))
ATT_GROUPS = len(ATT_PATTERNS)
ATT_Q_HEADS = ATT_GROUPS * ATT_KV_HEADS
ATT_BLOCK = 128
ATT_D_OUT = ATT_KV_HEADS * ATT_HEAD_DIM
N_BRANCH = 3
N_EXPERTS = 64
EXPERT_DIM = 512
TOP_K = 8
N_EXPERT_GROUPS = 8
TOPK_GROUPS = 4
ROUTED_SCALE = 2.5
DEEPNORM_ALPHA = (2 * DEPTH) ** 0.25
LN_EPS = 1e-5
RMS_EPS = 1e-6

LANES = 128
SUBLANES = 8
VMEM_LIMIT_BYTES = 56 * 1024 * 1024

IN_TILE = 1024
COL_LRU_X = N_BRANCH * D_MODEL
COL_LRU_G = COL_LRU_X + D_RNN
COL_SSM_Z = COL_LRU_G + D_RNN
COL_XBC = COL_SSM_Z + SSM_D_INNER
COL_DT = COL_XBC + SSM_XBC
COL_Q = COL_DT + SSM_HEADS
MAIN_COLS = -(-COL_Q // IN_TILE) * IN_TILE

MOE_ROWS = 256


def _cparams(sem):
    return pltpu.CompilerParams(dimension_semantics=sem, vmem_limit_bytes=VMEM_LIMIT_BYTES)


_NT = (((1,), (1,)), ((), ()))


def _bf16_terms(x, n):
    terms = []
    for _ in range(n):
        t = x.astype(BF16)
        terms.append(t)
        x = x - t.astype(F32)
    return terms


def _layer_norm(x, g, b):
    mu = jnp.mean(x, axis=-1, keepdims=True)
    xc = x - mu
    var = jnp.mean(xc * xc, axis=-1, keepdims=True)
    return xc * lax.rsqrt(var + LN_EPS) * g + b


def _store_token_major(ref, val):
    rows, w = val.shape
    k = w // LANES
    for j in range(k):
        ref[pl.ds(j, rows, stride=k), :] = val[:, j * LANES:(j + 1) * LANES]


def _load_token_major(ref, rows, k):
    return jnp.concatenate([ref[pl.ds(j, rows, stride=k), :] for j in range(k)], axis=1)


def _ln_kernel(x_ref, g_ref, b_ref, h_ref, hb_ref):
    y = _layer_norm(x_ref[...], g_ref[...], b_ref[...])
    h_ref[...] = y
    hb_ref[...] = y.astype(BF16)


def _embed_ln(x2d, g, b):
    n, d = x2d.shape
    tm = 512
    return pl.pallas_call(
        _ln_kernel,
        grid=(n // tm,),
        in_specs=[pl.BlockSpec((tm, d), lambda i: (i, 0)),
                  pl.BlockSpec((1, d), lambda i: (0, 0)),
                  pl.BlockSpec((1, d), lambda i: (0, 0))],
        out_specs=[pl.BlockSpec((tm, d), lambda i: (i, 0)),
                   pl.BlockSpec((tm, d), lambda i: (i, 0))],
        out_shape=[jax.ShapeDtypeStruct((n, d), F32), jax.ShapeDtypeStruct((n, d), BF16)],
        compiler_params=_cparams(("parallel",)),
        name="embed_ln",
    )(x2d, g.reshape(1, d), b.reshape(1, d))


def _in_proj_kernel(a_ref, wt_ref, o_ref, wb_ref):
    @pl.when(pl.program_id(1) == 0)
    def _():
        wb_ref[...] = wt_ref[...].astype(BF16)

    o_ref[...] = lax.dot_general(a_ref[...], wb_ref[...], _NT, preferred_element_type=F32).astype(o_ref.dtype)


def _in_proj(hb, w_in_t, layer):
    m, k = hb.shape
    tm = 1024
    tn = IN_TILE
    return pl.pallas_call(
        _in_proj_kernel,
        grid=(MAIN_COLS // tn, m // tm),
        in_specs=[pl.BlockSpec((tm, k), lambda j, i: (i, 0)),
                  pl.BlockSpec((None, tn, k), lambda j, i: (layer, j, 0))],
        out_specs=pl.BlockSpec((tm, tn), lambda j, i: (i, j)),
        out_shape=jax.ShapeDtypeStruct((m, MAIN_COLS), BF16),
        scratch_shapes=[pltpu.VMEM((tn, k), BF16)],
        compiler_params=_cparams(("parallel", "arbitrary")),
        name="in_proj",
    )(hb, w_in_t)


def _qkv_proj_kernel(a_ref, wt_hbm, o_ref, wf_ref, wb_ref, sem, *, layer, row0):
    @pl.when(pl.program_id(1) == 0)
    def _():
        rows = pl.ds(pl.multiple_of(row0 + pl.program_id(0) * wf_ref.shape[0], SUBLANES), wf_ref.shape[0])
        cp = pltpu.make_async_copy(wt_hbm.at[layer, rows, :], wf_ref, sem)
        cp.start()
        cp.wait()
        wb_ref[...] = wf_ref[...].astype(BF16)

    o_ref[...] = lax.dot_general(a_ref[...], wb_ref[...], _NT, preferred_element_type=F32)


def _qkv_proj(hb, w_in_t, layer):
    n, k = hb.shape
    tm = 1024
    tn = 1024
    ncols = ATT_Q_HEADS * ATT_HEAD_DIM + 2 * ATT_D_OUT
    return pl.pallas_call(
        functools.partial(_qkv_proj_kernel, layer=layer, row0=COL_Q),
        grid=(ncols // tn, n // tm),
        in_specs=[pl.BlockSpec((tm, k), lambda j, i: (i, 0)),
                  pl.BlockSpec(memory_space=pl.ANY)],
        out_specs=pl.BlockSpec((tm, tn), lambda j, i: (i, j)),
        out_shape=jax.ShapeDtypeStruct((n, ncols), F32),
        scratch_shapes=[pltpu.VMEM((tn, k), F32), pltpu.VMEM((tn, k), BF16), pltpu.SemaphoreType.DMA(())],
        compiler_params=_cparams(("parallel", "arbitrary")),
        name="qkv_proj",
    )(hb, w_in_t)


ATT_BATCH = 8


def _attn_kernel(slopes_ref, q0_ref, q1_ref, q2_ref, k_ref, v_ref, o_ref, kb_scr, vb_scr, res_scr, lse_scr):
    h = pl.program_id(1)
    t = o_ref.shape[0]
    blk = ATT_BLOCK
    scale = ATT_HEAD_DIM ** -0.5
    qi = lax.broadcasted_iota(jnp.int32, (blk, 2 * blk), 0)
    kj = lax.broadcasted_iota(jnp.int32, (blk, 2 * blk), 1)
    dist = blk + qi - kj
    q_refs = (q0_ref, q1_ref, q2_ref)

    def rows_of(d, r, u0, count):
        return slice(u0, u0 + count) if d == 1 else pl.ds(r + d * u0, count, stride=d)

    for g, (_, d) in enumerate(ATT_PATTERNS):
        u = t // d
        for r in range(d):
            kb_scr[g, r * u:(r + 1) * u, :] = k_ref[rows_of(d, r, 0, u), :].astype(BF16)
            vb_scr[g, r * u:(r + 1) * u, :] = v_ref[rows_of(d, r, 0, u), :].astype(BF16)

    for g, (window, d) in enumerate(ATT_PATTERNS):
        reach = window // d
        assert reach <= blk
        valid = (dist >= 0) & (dist <= reach)
        slope = slopes_ref[g, h]
        bias = jnp.where(valid, -(slope * d) * dist.astype(F32), -jnp.inf)
        bias_cur = bias[:, blk:]
        nb = (t // d) // blk
        q_ref = q_refs[g]
        for f0 in range(0, t // blk, ATT_BATCH):
            fs = list(range(f0, f0 + ATT_BATCH))
            scores, values = [], []
            for f in fs:
                r, i = divmod(f, nb)
                qb = q_ref[rows_of(d, r, i * blk, blk), :].astype(BF16)
                if i > 0:
                    rows, b = slice((f - 1) * blk, (f + 1) * blk), bias
                else:
                    rows, b = slice(f * blk, (f + 1) * blk), bias_cur
                scores.append(lax.dot_general(qb, kb_scr[g, rows, :], _NT, preferred_element_type=F32) * scale + b)
                values.append(vb_scr[g, rows, :])
            ms = [jnp.max(s, axis=-1, keepdims=True) for s in scores]
            ps = [jnp.exp(s - m) for s, m in zip(scores, ms)]
            ls = [jnp.sum(p, axis=-1, keepdims=True) for p in ps]
            pbs = [p.astype(BF16) for p in ps]
            for f, pb, vc, m, l in zip(fs, pbs, values, ms, ls):
                r, i = divmod(f, nb)
                rows = rows_of(d, r, i * blk, blk)
                res_scr[g, rows, :] = jnp.dot(pb, vc, preferred_element_type=F32) / l
                lse_scr[g, rows, :] = jnp.broadcast_to(m + jnp.log(l), (blk, LANES))
    ch = 256
    for c in range(t // ch):
        rows = slice(c * ch, (c + 1) * ch)
        lses = [lse_scr[g, rows, :] for g in range(ATT_GROUPS)]
        mx = functools.reduce(jnp.maximum, lses)
        ws = [jnp.exp(x - mx) for x in lses]
        num = sum(w * res_scr[g, rows, :] for g, w in enumerate(ws))
        o_ref[rows, :] = (num / sum(ws)).astype(o_ref.dtype)


def _attention(qkv, slopes, batch, seq):
    n = qkv.shape[0]
    hd = ATT_HEAD_DIM
    nh = ATT_KV_HEADS
    q_specs = [pl.BlockSpec((seq, hd), lambda b, h, g=g: (b, g * nh + h)) for g in range(ATT_GROUPS)]
    k_spec = pl.BlockSpec((seq, hd), lambda b, h: (b, ATT_Q_HEADS + h))
    v_spec = pl.BlockSpec((seq, hd), lambda b, h: (b, ATT_Q_HEADS + nh + h))
    return pl.pallas_call(
        _attn_kernel,
        grid=(batch, nh),
        in_specs=[pl.BlockSpec(memory_space=pltpu.SMEM)] + q_specs + [k_spec, v_spec],
        out_specs=pl.BlockSpec((seq, hd), lambda b, h: (b, h)),
        out_shape=jax.ShapeDtypeStruct((n, nh * hd), BF16),
        scratch_shapes=[pltpu.VMEM((ATT_GROUPS, seq, hd), BF16),
                        pltpu.VMEM((ATT_GROUPS, seq, hd), BF16),
                        pltpu.VMEM((ATT_GROUPS, seq, hd), F32),
                        pltpu.VMEM((ATT_GROUPS, seq, LANES), F32)],
        compiler_params=_cparams(("parallel", "parallel")),
        name="dilated_attention",
    )(slopes, qkv, qkv, qkv, qkv, qkv)


def _scan8(a, u, carry, row):
    for s in (1, 2, 4):
        a_sh = pltpu.roll(a, s, axis=0)
        u_sh = pltpu.roll(u, s, axis=0)
        m = row >= s
        u = jnp.where(m, a * u_sh + u, u)
        a = jnp.where(m, a * a_sh, a)
    return u + a * carry


def _lru_kernel(x_ref, g_ref, cw_ref, cb_ref, wr_ref, br_ref, wi_ref, bi_ref, lam_ref, o_ref,
                xpad_scr, a_scr, u_scr):
    t, c = o_ref.shape
    nb = LRU_BLOCKS
    bs = c // nb
    ch = 256
    pad = SUBLANES
    xpad_scr[0:pad, :] = jnp.zeros((pad, c), F32)
    for k in range(t // ch):
        xpad_scr[pad + k * ch:pad + (k + 1) * ch, :] = x_ref[k * ch:(k + 1) * ch, :].astype(F32)
    neg_lam = -lam_ref[...]
    sp = jnp.maximum(neg_lam, 0.0) + jnp.log1p(jnp.exp(-jnp.abs(neg_lam)))
    for k in range(t // ch):
        base = k * ch
        xc = cb_ref[...] + sum(
            cw_ref[j:j + 1, :] * xpad_scr[base + pad - (LRU_CONV - 1) + j:base + pad - (LRU_CONV - 1) + j + ch, :]
            for j in range(LRU_CONV))
        xcb = xc.astype(BF16)
        for n in range(nb):
            cols = slice(n * bs, (n + 1) * bs)
            xn = xcb[:, cols]
            r = jax.nn.sigmoid(jnp.dot(xn, wr_ref[n], preferred_element_type=F32) + br_ref[:, cols])
            ig = jax.nn.sigmoid(jnp.dot(xn, wi_ref[n], preferred_element_type=F32) + bi_ref[:, cols])
            log_a = -LRU_C * r * sp[:, cols]
            th = jnp.tanh(log_a)
            a_scr[base:base + ch, cols] = jnp.exp(log_a)
            u_scr[base:base + ch, cols] = jnp.sqrt(-2.0 * th / (1.0 - th)) * ig * xc[:, cols]
    row = lax.broadcasted_iota(jnp.int32, (SUBLANES, c), 0)

    def step(j, carry):
        rows = pl.ds(pl.multiple_of(j * SUBLANES, SUBLANES), SUBLANES)
        h8 = _scan8(a_scr[rows, :], u_scr[rows, :], carry, row)
        u_scr[rows, :] = h8
        return jnp.broadcast_to(h8[SUBLANES - 1:SUBLANES, :], (SUBLANES, c))

    lax.fori_loop(0, t // SUBLANES, step, jnp.zeros((SUBLANES, c), F32))
    for k in range(t // ch):
        rows = slice(k * ch, (k + 1) * ch)
        o_ref[rows, :] = (u_scr[rows, :] * jax.nn.gelu(g_ref[rows, :].astype(F32))).astype(o_ref.dtype)


def _rglru(proj, cw, cb, wr, br, wi, bi, lam, batch, seq):
    n = proj.shape[0]
    c = D_RNN
    bs = c // LRU_BLOCKS
    row = lambda b: (0, 0)
    return pl.pallas_call(
        _lru_kernel,
        grid=(batch,),
        in_specs=[pl.BlockSpec((seq, c), lambda b: (b, COL_LRU_X // c)),
                  pl.BlockSpec((seq, c), lambda b: (b, COL_LRU_G // c)),
                  pl.BlockSpec((LRU_CONV, c), row),
                  pl.BlockSpec((1, c), row),
                  pl.BlockSpec((LRU_BLOCKS, bs, bs), lambda b: (0, 0, 0)),
                  pl.BlockSpec((1, c), row),
                  pl.BlockSpec((LRU_BLOCKS, bs, bs), lambda b: (0, 0, 0)),
                  pl.BlockSpec((1, c), row),
                  pl.BlockSpec((1, c), row)],
        out_specs=pl.BlockSpec((seq, c), lambda b: (b, 0)),
        out_shape=jax.ShapeDtypeStruct((n, c), BF16),
        scratch_shapes=[pltpu.VMEM((seq + SUBLANES, c), F32),
                        pltpu.VMEM((seq, c), F32),
                        pltpu.VMEM((seq, c), F32)],
        compiler_params=_cparams(("parallel",)),
        name="rglru",
    )(proj, proj, cw, cb.reshape(1, c), wr, br.reshape(1, c), wi, bi.reshape(1, c), lam.reshape(1, c))


def _ssd_kernel(z_ref, xbc_ref, dtr_ref, cw_ref, cb_ref, dtb_ref, alog_ref, dskip_ref, ng_ref, o_ref,
                xpad_scr, st_scr):
    L = o_ref.shape[0]
    di = SSM_D_INNER
    ns = SSM_D_STATE
    pad = SUBLANES
    c = pl.program_id(1)

    @pl.when(c == 0)
    def _():
        xpad_scr[0:pad, :] = jnp.zeros((pad, SSM_XBC), F32)
        st_scr[...] = jnp.zeros(st_scr.shape, F32)

    xpad_scr[pad:pad + L, :] = xbc_ref[...].astype(F32)
    xc = cb_ref[...] + sum(
        cw_ref[j:j + 1, :] * xpad_scr[pad - (SSM_CONV - 1) + j:pad - (SSM_CONV - 1) + j + L, :]
        for j in range(SSM_CONV))
    xpad_scr[0:pad, :] = xpad_scr[L:L + pad, :]
    xc = xc * jax.nn.sigmoid(xc)
    xs = xc[:, :di]

    dt_in = dtr_ref[...].astype(F32) + dtb_ref[...]
    dt = jnp.maximum(dt_in, 0.0) + jnp.log1p(jnp.exp(-jnp.abs(dt_in)))
    adt = dt * (-jnp.exp(alog_ref[...]))
    ri = lax.broadcasted_iota(jnp.int32, (L, L), 0)
    ci = lax.broadcasted_iota(jnp.int32, (L, L), 1)
    causal = ri >= ci
    acum = jnp.dot(causal.astype(F32), adt, precision=HIGHEST, preferred_element_type=F32)
    acum_t = acum.T
    a_last = acum[L - 1:L, :]
    hl = lax.broadcasted_iota(jnp.int32, (LANES, di), 0)
    cl = lax.broadcasted_iota(jnp.int32, (LANES, di), 1)
    expand = (cl // SSM_HEAD_DIM == hl).astype(F32)
    dt_c = jnp.dot(dt, expand, precision=HIGHEST, preferred_element_type=F32)
    ea_c = jnp.dot(jnp.exp(acum), expand, precision=HIGHEST, preferred_element_type=F32)
    ds_c = jnp.dot(jnp.exp(a_last - acum), expand, precision=HIGHEST, preferred_element_type=F32)
    xdt = xs * dt_c
    xdt_b = xdt.astype(BF16)
    xw_b = (xdt * ds_c).astype(BF16)
    lane = lax.broadcasted_iota(jnp.int32, (L, LANES), 1)
    lo = lane < SSM_HEAD_DIM
    heads_per_group = SSM_HEADS // SSM_GROUPS
    ys = []
    for g in range(SSM_GROUPS):
        bm = xc[:, di + g * ns:di + (g + 1) * ns]
        cm = xc[:, di + SSM_GROUPS * ns + g * ns:di + SSM_GROUPS * ns + (g + 1) * ns]
        bm_b = bm.astype(BF16)
        cm_b = cm.astype(BF16)
        bm_t = bm.T.astype(BF16)
        cb = lax.dot_general(cm_b, bm_b, _NT, preferred_element_type=F32)
        for jp in range(heads_per_group // 2):
            j = g * (heads_per_group // 2) + jp
            cols = slice(j * LANES, (j + 1) * LANES)
            ms = []
            for hh in (2 * j, 2 * j + 1):
                seg = acum[:, hh:hh + 1] - acum_t[hh:hh + 1, :]
                decay = jnp.exp(jnp.where(causal, seg, -jnp.inf))
                ms.append((cb * decay).astype(BF16))
            mcat = jnp.concatenate(ms, axis=1)
            xp = xdt_b[:, cols]
            zero = jnp.zeros_like(xp)
            xcat = jnp.concatenate([jnp.where(lo, xp, zero), jnp.where(lo, zero, xp)], axis=0)
            y_diag = jnp.dot(mcat, xcat, preferred_element_type=F32)
            ent = st_scr[j]
            y_off = jnp.dot(cm_b, ent.astype(BF16), preferred_element_type=F32) * ea_c[:, cols]
            st_new = jnp.dot(bm_t, xw_b[:, cols], preferred_element_type=F32)
            st_scr[j] = st_new + ea_c[L - 1:L, cols] * ent
            ys.append(y_diag + y_off)
    y = jnp.concatenate(ys, axis=1) + dskip_ref[...] * xs
    zf = z_ref[...].astype(F32)
    y = y * (zf * jax.nn.sigmoid(zf))
    gw = di // SSM_GROUPS
    outs = []
    for g in range(SSM_GROUPS):
        yg = y[:, g * gw:(g + 1) * gw]
        outs.append(yg * lax.rsqrt(jnp.mean(yg * yg, axis=-1, keepdims=True) + RMS_EPS))
    o_ref[...] = (jnp.concatenate(outs, axis=1) * ng_ref[...]).astype(o_ref.dtype)


def _ssd(proj, cw, cb, dt_bias, a_log, d_skip, norm_g, batch, seq):
    n = proj.shape[0]
    L = SSM_CHUNK
    nc = seq // L
    di = SSM_D_INNER
    pad_h = LANES - SSM_HEADS
    dtb = jnp.pad(dt_bias, (0, pad_h)).reshape(1, LANES)
    alog = jnp.pad(a_log, (0, pad_h)).reshape(1, LANES)
    dskip = jnp.repeat(d_skip, SSM_HEAD_DIM).reshape(1, di)
    const = lambda b, c: (0, 0)
    return pl.pallas_call(
        _ssd_kernel,
        grid=(batch, nc),
        in_specs=[pl.BlockSpec((L, di), lambda b, c: (b * nc + c, COL_SSM_Z // di)),
                  pl.BlockSpec((L, SSM_XBC), lambda b, c: (b * nc + c, COL_XBC // SSM_XBC)),
                  pl.BlockSpec((L, LANES), lambda b, c: (b * nc + c, COL_DT // LANES)),
                  pl.BlockSpec((SSM_CONV, SSM_XBC), const),
                  pl.BlockSpec((1, SSM_XBC), const),
                  pl.BlockSpec((1, LANES), const),
                  pl.BlockSpec((1, LANES), const),
                  pl.BlockSpec((1, di), const),
                  pl.BlockSpec((1, di), const)],
        out_specs=pl.BlockSpec((L, di), lambda b, c: (b * nc + c, 0)),
        out_shape=jax.ShapeDtypeStruct((n, di), BF16),
        scratch_shapes=[pltpu.VMEM((L + SUBLANES, SSM_XBC), F32),
                        pltpu.VMEM((SSM_HEADS // 2, SSM_D_STATE, LANES), F32)],
        compiler_params=_cparams(("parallel", "arbitrary")),
        name="ssd",
    )(proj, proj, proj, cw, cb.reshape(1, SSM_XBC), dtb, alog, dskip, norm_g.reshape(1, di))


def _merge_kernel(g_ref, bg_ref, yl_ref, ys_ref, ya_ref, wl_ref, ws_ref, wa_ref, wo_ref,
                  h_ref, lg_ref, lb_ref, hn_ref, hb_ref, hrows_ref):
    d = h_ref.shape[1]
    merged = None
    for i, (y_ref, w_ref) in enumerate(((yl_ref, wl_ref), (ys_ref, ws_ref), (ya_ref, wa_ref))):
        gate = jax.nn.sigmoid(g_ref[:, i * d:(i + 1) * d].astype(F32) + bg_ref[:, i * d:(i + 1) * d])
        term = gate * jnp.dot(y_ref[...], w_ref[...], preferred_element_type=F32)
        merged = term if merged is None else merged + term
    mix = jnp.dot(merged.astype(BF16), wo_ref[...], preferred_element_type=F32)
    hn = _layer_norm(DEEPNORM_ALPHA * h_ref[...] + mix, lg_ref[...], lb_ref[...])
    hn_ref[...] = hn
    hb_ref[...] = hn.astype(BF16)
    _store_token_major(hrows_ref, hn)


def _merge_outproj_ln(proj, b_gate, y_lru, y_ssm, y_att, wl, ws, wa, wo, h, ln_g, ln_b):
    n, d = h.shape
    tm = 256
    gw = N_BRANCH * d
    kb = y_lru.shape[1]
    const = lambda i: (0, 0)
    once = pl.Buffered(1)
    return pl.pallas_call(
        _merge_kernel,
        grid=(n // tm,),
        in_specs=[pl.BlockSpec((tm, gw), lambda i: (i, 0)),
                  pl.BlockSpec((1, gw), const),
                  pl.BlockSpec((tm, kb), lambda i: (i, 0)),
                  pl.BlockSpec((tm, kb), lambda i: (i, 0)),
                  pl.BlockSpec((tm, kb), lambda i: (i, 0)),
                  pl.BlockSpec((kb, d), const, pipeline_mode=once),
                  pl.BlockSpec((kb, d), const, pipeline_mode=once),
                  pl.BlockSpec((kb, d), const, pipeline_mode=once),
                  pl.BlockSpec((d, d), const, pipeline_mode=once),
                  pl.BlockSpec((tm, d), lambda i: (i, 0)),
                  pl.BlockSpec((1, d), const),
                  pl.BlockSpec((1, d), const)],
        out_specs=[pl.BlockSpec((tm, d), lambda i: (i, 0)),
                   pl.BlockSpec((tm, d), lambda i: (i, 0)),
                   pl.BlockSpec((tm * (d // LANES), LANES), lambda i: (i, 0))],
        out_shape=[jax.ShapeDtypeStruct((n, d), F32), jax.ShapeDtypeStruct((n, d), BF16),
                   jax.ShapeDtypeStruct((n * (d // LANES), LANES), F32)],
        compiler_params=_cparams(("parallel",)),
        name="merge_outproj_ln",
    )(proj, b_gate.reshape(1, gw), y_lru, y_ssm, y_att, wl, ws, wa, wo, h,
      ln_g.reshape(1, d), ln_b.reshape(1, d))


def _seg_reduce(v, lane, op):
    for s in (1, 2, 4):
        up = pltpu.roll(v, LANES - s, axis=1)
        dn = pltpu.roll(v, s, axis=1)
        v = op(v, jnp.where((lane & s) == 0, up, dn))
    return v


def _router_kernel(h_ref, w_ref, b_ref, idx_ref, wt_ref, cnt_ref):
    tm = h_ref.shape[0]
    h_hi, h_lo = _bf16_terms(h_ref[...], 2)
    w_hi, w_lo = _bf16_terms(w_ref[...], 2)
    logits = (jnp.dot(h_hi, w_hi, preferred_element_type=F32) + jnp.dot(h_hi, w_lo, preferred_element_type=F32)
              + jnp.dot(h_lo, w_hi, preferred_element_type=F32))
    scores = jax.nn.sigmoid(logits)
    lane = lax.broadcasted_iota(jnp.int32, (tm, LANES), 1)
    lane_f = lane.astype(F32)
    real = lane < N_EXPERTS
    neg = -jnp.inf
    choice = jnp.where(real, scores + b_ref[...], neg)
    per_group = N_EXPERTS // N_EXPERT_GROUPS
    assert per_group == 8
    m1 = _seg_reduce(choice, lane, jnp.maximum)
    first = _seg_reduce(jnp.where(choice == m1, lane_f, float(LANES)), lane, jnp.minimum)
    m2 = _seg_reduce(jnp.where(lane_f == first, neg, choice), lane, jnp.maximum)
    gs = m1 + m2
    gidx = lane // per_group
    n_slots = LANES // per_group
    beaten = jnp.zeros((tm, LANES), jnp.int32)
    for k in range(1, n_slots):
        other = pltpu.roll(gs, per_group * k, axis=1)
        og = (gidx - k) & (n_slots - 1)
        wins = (other > gs) | ((other == gs) & (og < gidx))
        beaten = beaten + wins.astype(jnp.int32)
    masked = jnp.where((beaten < TOPK_GROUPS) & real, choice, neg)
    sel_i = jnp.zeros((tm, LANES), F32)
    sel_w = jnp.zeros((tm, LANES), F32)
    picked = jnp.zeros((tm, LANES), F32)
    for k in range(TOP_K):
        m = jnp.max(masked, axis=1, keepdims=True)
        am = jnp.min(jnp.where(masked == m, lane_f, float(LANES)), axis=1, keepdims=True)
        hit = lane_f == am
        wk = jnp.sum(jnp.where(hit, scores, 0.0), axis=1, keepdims=True)
        sel_i = jnp.where(lane == k, am, sel_i)
        sel_w = jnp.where(lane == k, wk, sel_w)
        picked = picked + jnp.where(hit, 1.0, 0.0)
        masked = jnp.where(hit, neg, masked)
    wsum = jnp.sum(sel_w, axis=1, keepdims=True)
    idx_ref[...] = sel_i.astype(jnp.int32)
    wt_ref[...] = sel_w / wsum * ROUTED_SCALE
    cnt_ref[...] = jnp.broadcast_to(jnp.sum(picked, axis=0, keepdims=True), cnt_ref.shape)


def _router(h, router_w, router_bias):
    n, d = h.shape
    tm = 512
    pad_e = LANES - N_EXPERTS
    w = jnp.pad(router_w, ((0, 0), (0, pad_e)))
    b = jnp.pad(router_bias, (0, pad_e)).reshape(1, LANES)
    return pl.pallas_call(
        _router_kernel,
        grid=(n // tm,),
        in_specs=[pl.BlockSpec((tm, d), lambda i: (i, 0)),
                  pl.BlockSpec((d, LANES), lambda i: (0, 0)),
                  pl.BlockSpec((1, LANES), lambda i: (0, 0))],
        out_specs=[pl.BlockSpec((tm, LANES), lambda i: (i, 0)),
                   pl.BlockSpec((tm, LANES), lambda i: (i, 0)),
                   pl.BlockSpec((SUBLANES, LANES), lambda i: (i, 0))],
        out_shape=[jax.ShapeDtypeStruct((n, LANES), jnp.int32), jax.ShapeDtypeStruct((n, LANES), F32),
                   jax.ShapeDtypeStruct((n // tm * SUBLANES, LANES), F32)],
        compiler_params=_cparams(("parallel",)),
        name="moe_router",
    )(h, w, b)


def _pack_bf16_pair(lo, hi):
    lo_bits = lax.bitcast_convert_type(lo.astype(BF16).astype(F32), jnp.uint32)
    hi_bits = lax.bitcast_convert_type(hi.astype(BF16).astype(F32), jnp.uint32)
    return (hi_bits & jnp.uint32(0xFFFF0000)) | (lo_bits >> 16)


def _unpack_bf16_pair(w):
    lo = lax.bitcast_convert_type(w << 16, F32)
    hi = lax.bitcast_convert_type(w & jnp.uint32(0xFFFF0000), F32)
    return lo, hi


def _expert_kernel(be_ref, nv_ref, first_ref, wslot_ref, nexte_ref, src0_ref, src1_ref, src2_ref, dst_ref,
                   h_hbm, w1_hbm, w3_hbm, w2_hbm, slots_hbm,
                   xbuf, ybuf, wf1, wf3, wf2, w1_ref, w3_ref, w2_ref, gsem, ssem, wsem, *, layer):
    i = pl.program_id(0)
    n_valid = nv_ref[0]
    src_refs = (src0_ref, src1_ref, src2_ref)

    def weight_copies(e, ws):
        return (pltpu.make_async_copy(w1_hbm.at[layer, e], wf1.at[ws], wsem.at[ws]),
                pltpu.make_async_copy(w3_hbm.at[layer, e], wf3.at[ws], wsem.at[ws]),
                pltpu.make_async_copy(w2_hbm.at[layer, e], wf2.at[ws], wsem.at[ws]))

    @pl.when(i == 0)
    def _():
        for c in weight_copies(be_ref[0], 0):
            c.start(priority=1)

    for ws in range(2):
        @pl.when((i < n_valid) & (first_ref[i] == 1) & (wslot_ref[i] == ws))
        def _(ws=ws):
            for c in weight_copies(0, ws):
                c.wait()

            @pl.when(nexte_ref[i] >= 0)
            def _():
                for c in weight_copies(nexte_ref[i], 1 - ws):
                    c.start(priority=1)

            w1_ref[...] = wf1[ws].astype(BF16)
            w3_ref[...] = wf3[ws].astype(BF16)
            w2_ref[...] = wf2[ws].astype(BF16)

    rows = dst_ref.shape[2]
    kx = xbuf.shape[1] // rows
    ky = ybuf.shape[1] // rows
    n_real = slots_hbm.shape[0] - 2 * rows * ky
    depth = xbuf.shape[0]
    slot = lax.rem(i, depth)

    def gather(idx_ref, s, r):
        tok = idx_ref[0, 0, r]
        return pltpu.make_async_copy(h_hbm.at[pl.ds(pl.multiple_of(tok * kx, kx), kx), :],
                                     xbuf.at[s, pl.ds(pl.multiple_of(r * kx, kx), kx), :], gsem.at[s])

    def scatter(s, r):
        dst = dst_ref[0, 0, r]
        return pltpu.make_async_copy(ybuf.at[s, pl.ds(pl.multiple_of(r * ky, ky), ky), :],
                                     slots_hbm.at[pl.ds(pl.multiple_of(dst * ky, ky), ky), :], ssem.at[s])

    def wait_gathers(s):
        pltpu.make_async_copy(h_hbm.at[pl.ds(0, rows * kx), :], xbuf.at[s], gsem.at[s]).wait()

    def wait_scatters(s):
        pltpu.make_async_copy(ybuf.at[s], slots_hbm.at[pl.ds(0, rows * ky), :], ssem.at[s]).wait()

    @pl.when(i == 0)
    def _():
        ybuf[0] = jnp.zeros((rows * ky, LANES), jnp.uint32)
        for p in range(2):
            pltpu.make_async_copy(ybuf.at[0], slots_hbm.at[pl.ds(n_real + p * rows * ky, rows * ky), :],
                                  ssem.at[0]).start()
        for p in range(2):
            pltpu.make_async_copy(ybuf.at[0], slots_hbm.at[pl.ds(n_real + p * rows * ky, rows * ky), :],
                                  ssem.at[0]).wait()
        for blk, idx_ref in enumerate(src_refs[:depth - 1]):
            def body(r, c, blk=blk, idx_ref=idx_ref):
                gather(idx_ref, blk, r).start()
                return c
            lax.fori_loop(0, rows, body, 0, unroll=8)

    def step(s):
        ahead = (s + depth - 1) % depth
        wait_gathers(s)
        x = _load_token_major(xbuf.at[s], rows, kx).astype(BF16)
        for r in range(rows):
            gather(src_refs[depth - 1], ahead, r).start()
        a = jnp.dot(x, w1_ref[...], preferred_element_type=F32)
        b = jnp.dot(x, w3_ref[...], preferred_element_type=F32)
        hb = (a * jax.nn.sigmoid(a) * b).astype(BF16)
        y = jnp.dot(hb, w2_ref[...], preferred_element_type=F32)
        half = y.shape[1] // 2
        _store_token_major(ybuf.at[s], _pack_bf16_pair(y[:, :half], y[:, half:]))

        @pl.when(i >= 1)
        def _():
            wait_scatters(ahead)

        for r in range(rows):
            scatter(s, r).start(priority=r % 2)

        @pl.when(i + 1 == n_valid)
        def _():
            wait_scatters(s)
            for k in range(1, depth):
                wait_gathers((s + k) % depth)

    for s in range(depth):
        @pl.when((i < n_valid) & (slot == s))
        def _(s=s):
            step(s)


def _routed_experts(h_rows, plan, w1, w3, w2, layer):
    block_e, n_valid, first, wslot, next_e, row_src, row_dst = plan
    d = w1.shape[2]
    kx = d // LANES
    ky = d // 2 // LANES
    n = h_rows.shape[0] // kx
    nblk = block_e.shape[0]
    rows = MOE_ROWS
    f = w1.shape[3]
    idx_block = (1, 1, rows)
    here = lambda i, *_: (i, 0, 0)
    ahead1 = lambda i, *_: (jnp.minimum(i + 1, nblk - 1), 0, 0)
    ahead2 = lambda i, *_: (jnp.minimum(i + 2, nblk - 1), 0, 0)
    grid_spec = pltpu.PrefetchScalarGridSpec(
        num_scalar_prefetch=5,
        grid=(nblk,),
        in_specs=[pl.BlockSpec(idx_block, here, memory_space=pltpu.SMEM),
                  pl.BlockSpec(idx_block, ahead1, memory_space=pltpu.SMEM),
                  pl.BlockSpec(idx_block, ahead2, memory_space=pltpu.SMEM),
                  pl.BlockSpec(idx_block, here, memory_space=pltpu.SMEM),
                  pl.BlockSpec(memory_space=pl.ANY),
                  pl.BlockSpec(memory_space=pl.ANY),
                  pl.BlockSpec(memory_space=pl.ANY),
                  pl.BlockSpec(memory_space=pl.ANY)],
        out_specs=pl.BlockSpec(memory_space=pl.ANY),
        scratch_shapes=[pltpu.VMEM((3, rows * kx, LANES), F32),
                        pltpu.VMEM((3, rows * ky, LANES), jnp.uint32),
                        pltpu.VMEM((2, d, f), F32),
                        pltpu.VMEM((2, d, f), F32),
                        pltpu.VMEM((2, f, d), F32),
                        pltpu.VMEM((d, f), BF16),
                        pltpu.VMEM((d, f), BF16),
                        pltpu.VMEM((f, d), BF16),
                        pltpu.SemaphoreType.DMA((3,)),
                        pltpu.SemaphoreType.DMA((3,)),
                        pltpu.SemaphoreType.DMA((2,))],
    )
    src3 = row_src.reshape(nblk, 1, rows)
    return pl.pallas_call(
        functools.partial(_expert_kernel, layer=layer),
        grid_spec=grid_spec,
        out_shape=jax.ShapeDtypeStruct(((TOP_K * n + 2 * rows) * ky, LANES), jnp.uint32),
        compiler_params=_cparams(("arbitrary",)),
        name="routed_experts",
    )(block_e, n_valid, first, wslot, next_e, src3, src3, src3, row_dst.reshape(nblk, 1, rows),
      h_rows, w1, w3, w2)


def _dispatch_plan(idx, counts, n):
    rows = MOE_ROWS
    e = N_EXPERTS
    a_total = n * TOP_K
    nblk = (a_total + e * (rows - 1)) // rows
    packed = idx.reshape(a_total) * a_total + jnp.arange(a_total, dtype=jnp.int32)
    order = jnp.sort(packed) % a_total
    ar = jnp.arange(e, dtype=jnp.int32)
    blocks_e = (counts + rows - 1) // rows
    blk_end = jnp.cumsum(blocks_e)
    blk_start = blk_end - blocks_e
    start = jnp.cumsum(counts) - counts
    run_of = jnp.cumsum((counts > 0).astype(jnp.int32)) - 1
    later = (ar[None, :] > ar[:, None]) & (counts[None, :] > 0)
    next_of = jnp.min(jnp.where(later, ar[None, :], e), axis=1)
    next_of = jnp.where(next_of < e, next_of, -1)
    bi = jnp.arange(nblk, dtype=jnp.int32)
    n_valid = blk_end[-1:].astype(jnp.int32)
    live = bi < n_valid[0]
    block_e = jnp.minimum(jnp.sum((bi[:, None] >= blk_end[None, :]).astype(jnp.int32), axis=1), e - 1)
    onehot = block_e[:, None] == ar[None, :]
    pick = lambda table: jnp.sum(jnp.where(onehot, table[None, :], 0), axis=1)
    r = jnp.arange(rows, dtype=jnp.int32)[None, :]
    j = (bi - pick(blk_start))[:, None] * rows + r
    valid = (j < pick(counts)[:, None]) & live[:, None]
    a = order[jnp.clip(pick(start)[:, None] + j, 0, a_total - 1)]
    tok = a // TOP_K
    slot = a % TOP_K
    row_src = jnp.where(valid, tok, 0).astype(jnp.int32)
    dump = TOP_K * n + (bi % 2)[:, None] * rows + r
    row_dst = jnp.where(valid, slot * n + tok, dump).astype(jnp.int32)
    first = (live & (bi == pick(blk_start))).astype(jnp.int32)
    wslot = (pick(run_of) % 2).astype(jnp.int32)
    next_e = jnp.where(live, pick(next_of), -1).astype(jnp.int32)
    return block_e, n_valid, first, wslot, next_e, row_src, row_dst


def _combine_kernel(*refs):
    slot_refs = refs[:TOP_K]
    wt_ref, hb_ref, h_ref, w1_ref, w3_ref, w2_ref, lg_ref, lb_ref, hn_ref, hbn_ref = refs[TOP_K:]
    tm, d = h_ref.shape
    ky = d // 2 // LANES
    lo = None
    hi = None
    for k, s_ref in enumerate(slot_refs):
        l, u = _unpack_bf16_pair(_load_token_major(s_ref, tm, ky))
        wk = wt_ref[:, k:k + 1]
        lo = wk * l if lo is None else lo + wk * l
        hi = wk * u if hi is None else hi + wk * u
    routed = jnp.concatenate([lo, hi], axis=1)
    x = hb_ref[...]
    a = jnp.dot(x, w1_ref[...], preferred_element_type=F32)
    b = jnp.dot(x, w3_ref[...], preferred_element_type=F32)
    shared = jnp.dot((a * jax.nn.sigmoid(a) * b).astype(BF16), w2_ref[...], preferred_element_type=F32)
    hn = _layer_norm(DEEPNORM_ALPHA * h_ref[...] + routed + shared, lg_ref[...], lb_ref[...])
    hn_ref[...] = hn
    hbn_ref[...] = hn.astype(BF16)


def _combine_shared_ln(slots, wts, hb, h, ws1, ws3, ws2, ln_g, ln_b):
    n, d = h.shape
    tm = 256
    ky = d // 2 // LANES
    f = ws1.shape[1]
    nt = n // tm
    const = lambda i: (0, 0)
    slot_specs = [pl.BlockSpec((tm * ky, LANES), lambda i, k=k: (k * nt + i, 0)) for k in range(TOP_K)]
    return pl.pallas_call(
        _combine_kernel,
        grid=(nt,),
        in_specs=slot_specs + [pl.BlockSpec((tm, LANES), lambda i: (i, 0)),
                               pl.BlockSpec((tm, d), lambda i: (i, 0)),
                               pl.BlockSpec((tm, d), lambda i: (i, 0)),
                               pl.BlockSpec((d, f), const),
                               pl.BlockSpec((d, f), const),
                               pl.BlockSpec((f, d), const),
                               pl.BlockSpec((1, d), const),
                               pl.BlockSpec((1, d), const)],
        out_specs=[pl.BlockSpec((tm, d), lambda i: (i, 0)),
                   pl.BlockSpec((tm, d), lambda i: (i, 0))],
        out_shape=[jax.ShapeDtypeStruct((n, d), F32), jax.ShapeDtypeStruct((n, d), BF16)],
        compiler_params=_cparams(("parallel",)),
        name="combine_shared_ln",
    )(*([slots] * TOP_K), wts, hb, h, ws1, ws3, ws2, ln_g.reshape(1, d), ln_b.reshape(1, d))


def kernel(x, emb_ln_g, emb_ln_b, w_in, b_gate, lru_conv_w, lru_conv_b, lru_wr, lru_br, lru_wi, lru_bi, lru_lambda, ssm_conv_w, ssm_conv_b, ssm_dt_bias, ssm_a_log, ssm_d, ssm_norm_g, w_proj_lru, w_proj_ssm, w_proj_att, w_out, ln1_g, ln1_b, router_w, router_bias, w1, w3, w2, ws1, ws3, ws2, ln2_g, ln2_b):
    batch, seq, d = x.shape
    n = batch * seq
    slopes = (2.0 ** (-8.0 * jnp.arange(1, ATT_Q_HEADS + 1, dtype=F32) / ATT_Q_HEADS)).reshape(ATT_GROUPS, ATT_KV_HEADS)
    h, hb = _embed_ln(x.reshape(n, d), emb_ln_g, emb_ln_b)
    w_in_t = jnp.swapaxes(w_in, 1, 2)
    for l in range(DEPTH):
        proj = _in_proj(hb, w_in_t, l)
        qkv = _qkv_proj(hb, w_in_t, l)
        y_lru = _rglru(proj, lru_conv_w[l], lru_conv_b[l], lru_wr[l].astype(BF16), lru_br[l],
                       lru_wi[l].astype(BF16), lru_bi[l], lru_lambda[l], batch, seq)
        y_ssm = _ssd(proj, ssm_conv_w[l], ssm_conv_b[l], ssm_dt_bias[l], ssm_a_log[l], ssm_d[l],
                     ssm_norm_g[l], batch, seq)
        y_att = _attention(qkv, slopes, batch, seq)
        h, hb, h_rows = _merge_outproj_ln(proj, b_gate[l], y_lru, y_ssm, y_att,
                                          w_proj_lru[l].astype(BF16), w_proj_ssm[l].astype(BF16),
                                          w_proj_att[l].astype(BF16), w_out[l].astype(BF16), h, ln1_g[l], ln1_b[l])
        idx, wts, tile_counts = _router(h, router_w[l], router_bias[l])
        counts = jnp.sum(tile_counts.reshape(-1, SUBLANES, LANES)[:, 0, :N_EXPERTS], axis=0).astype(jnp.int32)
        plan = _dispatch_plan(idx[:, :TOP_K], counts, n)
        slots = _routed_experts(h_rows, plan, w1, w3, w2, l)
        h, hb = _combine_shared_ln(slots, wts, hb, h, ws1[l].astype(BF16), ws3[l].astype(BF16),
                                   ws2[l].astype(BF16), ln2_g[l], ln2_b[l])
    return h.reshape(batch, seq, d)
```

```python
import functools

import jax
import jax.numpy as jnp
from jax import lax
from jax.experimental import pallas as pl
from jax.experimental.pallas import tpu as pltpu

F32 = jnp.float32
BF16 = jnp.bfloat16
HIGHEST = lax.Precision.HIGHEST

D_MODEL = 2048
DEPTH = 2
D_RNN = 1024
LRU_BLOCKS = 8
LRU_CONV = 4
LRU_C = 8.0
SSM_D_INNER = 1024
SSM_HEAD_DIM = 64
SSM_HEADS = SSM_D_INNER // SSM_HEAD_DIM
SSM_GROUPS = 2
SSM_D_STATE = 128
SSM_CONV = 4
SSM_CHUNK = 128
SSM_XBC = SSM_D_INNER + 2 * SSM_GROUPS * SSM_D_STATE
ATT_HEAD_DIM = 128
ATT_KV_HEADS = 8
ATT_PATTERNS = ((128, 1), (512, 4), (2048, 16))
ATT_GROUPS = len(ATT_PATTERNS)
ATT_Q_HEADS = ATT_GROUPS * ATT_KV_HEADS
ATT_BLOCK = 128
ATT_D_OUT = ATT_KV_HEADS * ATT_HEAD_DIM
N_BRANCH = 3
N_EXPERTS = 64
EXPERT_DIM = 512
TOP_K = 8
N_EXPERT_GROUPS = 8
TOPK_GROUPS = 4
ROUTED_SCALE = 2.5
DEEPNORM_ALPHA = (2 * DEPTH) ** 0.25
LN_EPS = 1e-5
RMS_EPS = 1e-6

LANES = 128
SUBLANES = 8
VMEM_LIMIT_BYTES = 56 * 1024 * 1024

IN_TILE = 1024
COL_LRU_X = N_BRANCH * D_MODEL
COL_LRU_G = COL_LRU_X + D_RNN
COL_SSM_Z = COL_LRU_G + D_RNN
COL_XBC = COL_SSM_Z + SSM_D_INNER
COL_DT = COL_XBC + SSM_XBC
COL_Q = COL_DT + SSM_HEADS
MAIN_COLS = -(-COL_Q // IN_TILE) * IN_TILE

MOE_ROWS = 256


def _cparams(sem):
    return pltpu.CompilerParams(dimension_semantics=sem, vmem_limit_bytes=VMEM_LIMIT_BYTES)


_NT = (((1,), (1,)), ((), ()))


def _bf16_terms(x, n):
    terms = []
    for _ in range(n):
        t = x.astype(BF16)
        terms.append(t)
        x = x - t.astype(F32)
    return terms


def _layer_norm(x, g, b):
    mu = jnp.mean(x, axis=-1, keepdims=True)
    xc = x - mu
    var = jnp.mean(xc * xc, axis=-1, keepdims=True)
    return xc * lax.rsqrt(var + LN_EPS) * g + b


def _store_token_major(ref, val):
    rows, w = val.shape
    k = w // LANES
    for j in range(k):
        ref[pl.ds(j, rows, stride=k), :] = val[:, j * LANES:(j + 1) * LANES]


def _load_token_major(ref, rows, k):
    return jnp.concatenate([ref[pl.ds(j, rows, stride=k), :] for j in range(k)], axis=1)


def _ln_kernel(x_ref, g_ref, b_ref, h_ref, hb_ref):
    y = _layer_norm(x_ref[...], g_ref[...], b_ref[...])
    h_ref[...] = y
    hb_ref[...] = y.astype(BF16)


def _embed_ln(x2d, g, b):
    n, d = x2d.shape
    tm = 512
    return pl.pallas_call(
        _ln_kernel,
        grid=(n // tm,),
        in_specs=[pl.BlockSpec((tm, d), lambda i: (i, 0)),
                  pl.BlockSpec((1, d), lambda i: (0, 0)),
                  pl.BlockSpec((1, d), lambda i: (0, 0))],
        out_specs=[pl.BlockSpec((tm, d), lambda i: (i, 0)),
                   pl.BlockSpec((tm, d), lambda i: (i, 0))],
        out_shape=[jax.ShapeDtypeStruct((n, d), F32), jax.ShapeDtypeStruct((n, d), BF16)],
        compiler_params=_cparams(("parallel",)),
        name="embed_ln",
    )(x2d, g.reshape(1, d), b.reshape(1, d))


def _in_proj_kernel(a_ref, wt_ref, o_ref, wb_ref):
    @pl.when(pl.program_id(1) == 0)
    def _():
        wb_ref[...] = wt_ref[...].astype(BF16)

    o_ref[...] = lax.dot_general(a_ref[...], wb_ref[...], _NT, preferred_element_type=F32).astype(o_ref.dtype)


def _in_proj(hb, w_in_t, layer):
    m, k = hb.shape
    tm = 2048
    tn = IN_TILE
    return pl.pallas_call(
        _in_proj_kernel,
        grid=(MAIN_COLS // tn, m // tm),
        in_specs=[pl.BlockSpec((tm, k), lambda j, i: (i, 0)),
                  pl.BlockSpec((None, tn, k), lambda j, i: (layer, j, 0))],
        out_specs=pl.BlockSpec((tm, tn), lambda j, i: (i, j)),
        out_shape=jax.ShapeDtypeStruct((m, MAIN_COLS), BF16),
        scratch_shapes=[pltpu.VMEM((tn, k), BF16)],
        compiler_params=_cparams(("parallel", "arbitrary")),
        name="in_proj",
    )(hb, w_in_t)


def _qkv_proj_kernel(a_ref, wt_hbm, o_ref, wf_ref, wb_ref, sem, *, layer, row0):
    @pl.when(pl.program_id(1) == 0)
    def _():
        rows = pl.ds(pl.multiple_of(row0 + pl.program_id(0) * wf_ref.shape[0], SUBLANES), wf_ref.shape[0])
        cp = pltpu.make_async_copy(wt_hbm.at[layer, rows, :], wf_ref, sem)
        cp.start()
        cp.wait()
        wb_ref[...] = wf_ref[...].astype(BF16)

    o_ref[...] = lax.dot_general(a_ref[...], wb_ref[...], _NT, preferred_element_type=F32)


def _qkv_proj(hb, w_in_t, layer):
    n, k = hb.shape
    tm = 1024
    tn = 1024
    ncols = ATT_Q_HEADS * ATT_HEAD_DIM + 2 * ATT_D_OUT
    return pl.pallas_call(
        functools.partial(_qkv_proj_kernel, layer=layer, row0=COL_Q),
        grid=(ncols // tn, n // tm),
        in_specs=[pl.BlockSpec((tm, k), lambda j, i: (i, 0)),
                  pl.BlockSpec(memory_space=pl.ANY)],
        out_specs=pl.BlockSpec((tm, tn), lambda j, i: (i, j)),
        out_shape=jax.ShapeDtypeStruct((n, ncols), F32),
        scratch_shapes=[pltpu.VMEM((tn, k), F32), pltpu.VMEM((tn, k), BF16), pltpu.SemaphoreType.DMA(())],
        compiler_params=_cparams(("parallel", "arbitrary")),
        name="qkv_proj",
    )(hb, w_in_t)


ATT_BATCH = 8


def _attn_kernel(slopes_ref, q0_ref, q1_ref, q2_ref, k_ref, v_ref, o_ref, kb_scr, vb_scr, res_scr, lse_scr):
    h = pl.program_id(1)
    t = o_ref.shape[0]
    blk = ATT_BLOCK
    scale = ATT_HEAD_DIM ** -0.5
    qi = lax.broadcasted_iota(jnp.int32, (blk, 2 * blk), 0)
    kj = lax.broadcasted_iota(jnp.int32, (blk, 2 * blk), 1)
    dist = blk + qi - kj
    q_refs = (q0_ref, q1_ref, q2_ref)

    def rows_of(d, r, u0, count):
        return slice(u0, u0 + count) if d == 1 else pl.ds(r + d * u0, count, stride=d)

    for g, (_, d) in enumerate(ATT_PATTERNS):
        u = t // d
        for r in range(d):
            kb_scr[g, r * u:(r + 1) * u, :] = k_ref[rows_of(d, r, 0, u), :].astype(BF16)
            vb_scr[g, r * u:(r + 1) * u, :] = v_ref[rows_of(d, r, 0, u), :].astype(BF16)

    for g, (window, d) in enumerate(ATT_PATTERNS):
        reach = window // d
        assert reach <= blk
        valid = (dist >= 0) & (dist <= reach)
        slope = slopes_ref[g, h]
        bias = jnp.where(valid, -(slope * d) * dist.astype(F32), -jnp.inf)
        bias_cur = bias[:, blk:]
        nb = (t // d) // blk
        q_ref = q_refs[g]
        for f0 in range(0, t // blk, ATT_BATCH):
            fs = list(range(f0, f0 + ATT_BATCH))
            scores, values = [], []
            for f in fs:
                r, i = divmod(f, nb)
                qb = q_ref[rows_of(d, r, i * blk, blk), :].astype(BF16)
                if i > 0:
                    rows, b = slice((f - 1) * blk, (f + 1) * blk), bias
                else:
                    rows, b = slice(f * blk, (f + 1) * blk), bias_cur
                scores.append(lax.dot_general(qb, kb_scr[g, rows, :], _NT, preferred_element_type=F32) * scale + b)
                values.append(vb_scr[g, rows, :])
            ms = [jnp.max(s, axis=-1, keepdims=True) for s in scores]
            ps = [jnp.exp(s - m) for s, m in zip(scores, ms)]
            ls = [jnp.sum(p, axis=-1, keepdims=True) for p in ps]
            pbs = [p.astype(BF16) for p in ps]
            for f, pb, vc, m, l in zip(fs, pbs, values, ms, ls):
                r, i = divmod(f, nb)
                rows = rows_of(d, r, i * blk, blk)
                res_scr[g, rows, :] = jnp.dot(pb, vc, preferred_element_type=F32) / l
                lse_scr[g, rows, :] = jnp.broadcast_to(m + jnp.log(l), (blk, LANES))
    ch = 256
    for c in range(t // ch):
        rows = slice(c * ch, (c + 1) * ch)
        lses = [lse_scr[g, rows, :] for g in range(ATT_GROUPS)]
        mx = functools.reduce(jnp.maximum, lses)
        ws = [jnp.exp(x - mx) for x in lses]
        num = sum(w * res_scr[g, rows, :] for g, w in enumerate(ws))
        o_ref[rows, :] = (num / sum(ws)).astype(o_ref.dtype)


def _attention(qkv, slopes, batch, seq):
    n = qkv.shape[0]
    hd = ATT_HEAD_DIM
    nh = ATT_KV_HEADS
    q_specs = [pl.BlockSpec((seq, hd), lambda b, h, g=g: (b, g * nh + h)) for g in range(ATT_GROUPS)]
    k_spec = pl.BlockSpec((seq, hd), lambda b, h: (b, ATT_Q_HEADS + h))
    v_spec = pl.BlockSpec((seq, hd), lambda b, h: (b, ATT_Q_HEADS + nh + h))
    return pl.pallas_call(
        _attn_kernel,
        grid=(batch, nh),
        in_specs=[pl.BlockSpec(memory_space=pltpu.SMEM)] + q_specs + [k_spec, v_spec],
        out_specs=pl.BlockSpec((seq, hd), lambda b, h: (b, h)),
        out_shape=jax.ShapeDtypeStruct((n, nh * hd), BF16),
        scratch_shapes=[pltpu.VMEM((ATT_GROUPS, seq, hd), BF16),
                        pltpu.VMEM((ATT_GROUPS, seq, hd), BF16),
                        pltpu.VMEM((ATT_GROUPS, seq, hd), F32),
                        pltpu.VMEM((ATT_GROUPS, seq, LANES), F32)],
        compiler_params=_cparams(("parallel", "parallel")),
        name="dilated_attention",
    )(slopes, qkv, qkv, qkv, qkv, qkv)


def _scan8(a, u, carry, row):
    for s in (1, 2, 4):
        a_sh = pltpu.roll(a, s, axis=0)
        u_sh = pltpu.roll(u, s, axis=0)
        m = row >= s
        u = jnp.where(m, a * u_sh + u, u)
        a = jnp.where(m, a * a_sh, a)
    return u + a * carry


def _lru_kernel(x_ref, g_ref, cw_ref, cb_ref, wr_ref, br_ref, wi_ref, bi_ref, lam_ref, o_ref,
                xpad_scr, a_scr, u_scr):
    t, c = o_ref.shape
    nb = LRU_BLOCKS
    bs = c // nb
    ch = 256
    pad = SUBLANES
    xpad_scr[0:pad, :] = jnp.zeros((pad, c), F32)
    for k in range(t // ch):
        xpad_scr[pad + k * ch:pad + (k + 1) * ch, :] = x_ref[k * ch:(k + 1) * ch, :].astype(F32)
    neg_lam = -lam_ref[...]
    sp = jnp.maximum(neg_lam, 0.0) + jnp.log1p(jnp.exp(-jnp.abs(neg_lam)))
    for k in range(t // ch):
        base = k * ch
        xc = cb_ref[...] + sum(
            cw_ref[j:j + 1, :] * xpad_scr[base + pad - (LRU_CONV - 1) + j:base + pad - (LRU_CONV - 1) + j + ch, :]
            for j in range(LRU_CONV))
        xcb = xc.astype(BF16)
        for n in range(nb):
            cols = slice(n * bs, (n + 1) * bs)
            xn = xcb[:, cols]
            r = jax.nn.sigmoid(jnp.dot(xn, wr_ref[n], preferred_element_type=F32) + br_ref[:, cols])
            ig = jax.nn.sigmoid(jnp.dot(xn, wi_ref[n], preferred_element_type=F32) + bi_ref[:, cols])
            log_a = -LRU_C * r * sp[:, cols]
            th = jnp.tanh(log_a)
            a_scr[base:base + ch, cols] = jnp.exp(log_a)
            u_scr[base:base + ch, cols] = jnp.sqrt(-2.0 * th / (1.0 - th)) * ig * xc[:, cols]
    row = lax.broadcasted_iota(jnp.int32, (SUBLANES, c), 0)

    def step(j, carry):
        rows = pl.ds(pl.multiple_of(j * SUBLANES, SUBLANES), SUBLANES)
        h8 = _scan8(a_scr[rows, :], u_scr[rows, :], carry, row)
        u_scr[rows, :] = h8
        return jnp.broadcast_to(h8[SUBLANES - 1:SUBLANES, :], (SUBLANES, c))

    lax.fori_loop(0, t // SUBLANES, step, jnp.zeros((SUBLANES, c), F32))
    for k in range(t // ch):
        rows = slice(k * ch, (k + 1) * ch)
        o_ref[rows, :] = (u_scr[rows, :] * jax.nn.gelu(g_ref[rows, :].astype(F32))).astype(o_ref.dtype)


def _rglru(proj, cw, cb, wr, br, wi, bi, lam, batch, seq):
    n = proj.shape[0]
    c = D_RNN
    bs = c // LRU_BLOCKS
    row = lambda b: (0, 0)
    return pl.pallas_call(
        _lru_kernel,
        grid=(batch,),
        in_specs=[pl.BlockSpec((seq, c), lambda b: (b, COL_LRU_X // c)),
                  pl.BlockSpec((seq, c), lambda b: (b, COL_LRU_G // c)),
                  pl.BlockSpec((LRU_CONV, c), row),
                  pl.BlockSpec((1, c), row),
                  pl.BlockSpec((LRU_BLOCKS, bs, bs), lambda b: (0, 0, 0)),
                  pl.BlockSpec((1, c), row),
                  pl.BlockSpec((LRU_BLOCKS, bs, bs), lambda b: (0, 0, 0)),
                  pl.BlockSpec((1, c), row),
                  pl.BlockSpec((1, c), row)],
        out_specs=pl.BlockSpec((seq, c), lambda b: (b, 0)),
        out_shape=jax.ShapeDtypeStruct((n, c), BF16),
        scratch_shapes=[pltpu.VMEM((seq + SUBLANES, c), F32),
                        pltpu.VMEM((seq, c), F32),
                        pltpu.VMEM((seq, c), F32)],
        compiler_params=_cparams(("parallel",)),
        name="rglru",
    )(proj, proj, cw, cb.reshape(1, c), wr, br.reshape(1, c), wi, bi.reshape(1, c), lam.reshape(1, c))


def _ssd_kernel(z_ref, xbc_ref, dtr_ref, cw_ref, cb_ref, dtb_ref, alog_ref, dskip_ref, ng_ref, o_ref,
                xpad_scr, st_scr):
    L = o_ref.shape[0]
    di = SSM_D_INNER
    ns = SSM_D_STATE
    pad = SUBLANES
    c = pl.program_id(1)

    @pl.when(c == 0)
    def _():
        xpad_scr[0:pad, :] = jnp.zeros((pad, SSM_XBC), F32)
        st_scr[...] = jnp.zeros(st_scr.shape, F32)

    xpad_scr[pad:pad + L, :] = xbc_ref[...].astype(F32)
    xc = cb_ref[...] + sum(
        cw_ref[j:j + 1, :] * xpad_scr[pad - (SSM_CONV - 1) + j:pad - (SSM_CONV - 1) + j + L, :]
        for j in range(SSM_CONV))
    xpad_scr[0:pad, :] = xpad_scr[L:L + pad, :]
    xc = xc * jax.nn.sigmoid(xc)
    xs = xc[:, :di]

    dt_in = dtr_ref[...].astype(F32) + dtb_ref[...]
    dt = jnp.maximum(dt_in, 0.0) + jnp.log1p(jnp.exp(-jnp.abs(dt_in)))
    adt = dt * (-jnp.exp(alog_ref[...]))
    ri = lax.broadcasted_iota(jnp.int32, (L, L), 0)
    ci = lax.broadcasted_iota(jnp.int32, (L, L), 1)
    causal = ri >= ci
    acum = jnp.dot(causal.astype(F32), adt, precision=HIGHEST, preferred_element_type=F32)
    acum_t = acum.T
    a_last = acum[L - 1:L, :]
    hl = lax.broadcasted_iota(jnp.int32, (LANES, di), 0)
    cl = lax.broadcasted_iota(jnp.int32, (LANES, di), 1)
    expand = (cl // SSM_HEAD_DIM == hl).astype(F32)
    dt_c = jnp.dot(dt, expand, precision=HIGHEST, preferred_element_type=F32)
    ea_c = jnp.dot(jnp.exp(acum), expand, precision=HIGHEST, preferred_element_type=F32)
    ds_c = jnp.dot(jnp.exp(a_last - acum), expand, precision=HIGHEST, preferred_element_type=F32)
    xdt = xs * dt_c
    xdt_b = xdt.astype(BF16)
    xw_b = (xdt * ds_c).astype(BF16)
    lane = lax.broadcasted_iota(jnp.int32, (L, LANES), 1)
    lo = lane < SSM_HEAD_DIM
    heads_per_group = SSM_HEADS // SSM_GROUPS
    ys = []
    for g in range(SSM_GROUPS):
        bm = xc[:, di + g * ns:di + (g + 1) * ns]
        cm = xc[:, di + SSM_GROUPS * ns + g * ns:di + SSM_GROUPS * ns + (g + 1) * ns]
        bm_b = bm.astype(BF16)
        cm_b = cm.astype(BF16)
        bm_t = bm.T.astype(BF16)
        cb = lax.dot_general(cm_b, bm_b, _NT, preferred_element_type=F32)
        for jp in range(heads_per_group // 2):
            j = g * (heads_per_group // 2) + jp
            cols = slice(j * LANES, (j + 1) * LANES)
            ms = []
            for hh in (2 * j, 2 * j + 1):
                seg = acum[:, hh:hh + 1] - acum_t[hh:hh + 1, :]
                decay = jnp.exp(jnp.where(causal, seg, -jnp.inf))
                ms.append((cb * decay).astype(BF16))
            mcat = jnp.concatenate(ms, axis=1)
            xp = xdt_b[:, cols]
            zero = jnp.zeros_like(xp)
            xcat = jnp.concatenate([jnp.where(lo, xp, zero), jnp.where(lo, zero, xp)], axis=0)
            y_diag = jnp.dot(mcat, xcat, preferred_element_type=F32)
            ent = st_scr[j]
            y_off = jnp.dot(cm_b, ent.astype(BF16), preferred_element_type=F32) * ea_c[:, cols]
            st_new = jnp.dot(bm_t, xw_b[:, cols], preferred_element_type=F32)
            st_scr[j] = st_new + ea_c[L - 1:L, cols] * ent
            ys.append(y_diag + y_off)
    y = jnp.concatenate(ys, axis=1) + dskip_ref[...] * xs
    zf = z_ref[...].astype(F32)
    y = y * (zf * jax.nn.sigmoid(zf))
    gw = di // SSM_GROUPS
    outs = []
    for g in range(SSM_GROUPS):
        yg = y[:, g * gw:(g + 1) * gw]
        outs.append(yg * lax.rsqrt(jnp.mean(yg * yg, axis=-1, keepdims=True) + RMS_EPS))
    o_ref[...] = (jnp.concatenate(outs, axis=1) * ng_ref[...]).astype(o_ref.dtype)


def _ssd(proj, cw, cb, dt_bias, a_log, d_skip, norm_g, batch, seq):
    n = proj.shape[0]
    L = SSM_CHUNK
    nc = seq // L
    di = SSM_D_INNER
    pad_h = LANES - SSM_HEADS
    dtb = jnp.pad(dt_bias, (0, pad_h)).reshape(1, LANES)
    alog = jnp.pad(a_log, (0, pad_h)).reshape(1, LANES)
    dskip = jnp.repeat(d_skip, SSM_HEAD_DIM).reshape(1, di)
    const = lambda b, c: (0, 0)
    return pl.pallas_call(
        _ssd_kernel,
        grid=(batch, nc),
        in_specs=[pl.BlockSpec((L, di), lambda b, c: (b * nc + c, COL_SSM_Z // di)),
                  pl.BlockSpec((L, SSM_XBC), lambda b, c: (b * nc + c, COL_XBC // SSM_XBC)),
                  pl.BlockSpec((L, LANES), lambda b, c: (b * nc + c, COL_DT // LANES)),
                  pl.BlockSpec((SSM_CONV, SSM_XBC), const),
                  pl.BlockSpec((1, SSM_XBC), const),
                  pl.BlockSpec((1, LANES), const),
                  pl.BlockSpec((1, LANES), const),
                  pl.BlockSpec((1, di), const),
                  pl.BlockSpec((1, di), const)],
        out_specs=pl.BlockSpec((L, di), lambda b, c: (b * nc + c, 0)),
        out_shape=jax.ShapeDtypeStruct((n, di), BF16),
        scratch_shapes=[pltpu.VMEM((L + SUBLANES, SSM_XBC), F32),
                        pltpu.VMEM((SSM_HEADS // 2, SSM_D_STATE, LANES), F32)],
        compiler_params=_cparams(("parallel", "arbitrary")),
        name="ssd",
    )(proj, proj, proj, cw, cb.reshape(1, SSM_XBC), dtb, alog, dskip, norm_g.reshape(1, di))


def _merge_kernel(g_ref, bg_ref, yl_ref, ys_ref, ya_ref, wl_ref, ws_ref, wa_ref, wo_ref,
                  h_ref, lg_ref, lb_ref, hn_ref, hb_ref, hrows_ref):
    d = h_ref.shape[1]
    merged = None
    for i, (y_ref, w_ref) in enumerate(((yl_ref, wl_ref), (ys_ref, ws_ref), (ya_ref, wa_ref))):
        gate = jax.nn.sigmoid(g_ref[:, i * d:(i + 1) * d].astype(F32) + bg_ref[:, i * d:(i + 1) * d])
        term = gate * jnp.dot(y_ref[...], w_ref[...], preferred_element_type=F32)
        merged = term if merged is None else merged + term
    mix = jnp.dot(merged.astype(BF16), wo_ref[...], preferred_element_type=F32)
    hn = _layer_norm(DEEPNORM_ALPHA * h_ref[...] + mix, lg_ref[...], lb_ref[...])
    hn_ref[...] = hn
    hb_ref[...] = hn.astype(BF16)
    _store_token_major(hrows_ref, hn)


def _merge_outproj_ln(proj, b_gate, y_lru, y_ssm, y_att, wl, ws, wa, wo, h, ln_g, ln_b):
    n, d = h.shape
    tm = 256
    gw = N_BRANCH * d
    kb = y_lru.shape[1]
    const = lambda i: (0, 0)
    once = pl.Buffered(1)
    return pl.pallas_call(
        _merge_kernel,
        grid=(n // tm,),
        in_specs=[pl.BlockSpec((tm, gw), lambda i: (i, 0)),
                  pl.BlockSpec((1, gw), const),
                  pl.BlockSpec((tm, kb), lambda i: (i, 0)),
                  pl.BlockSpec((tm, kb), lambda i: (i, 0)),
                  pl.BlockSpec((tm, kb), lambda i: (i, 0)),
                  pl.BlockSpec((kb, d), const, pipeline_mode=once),
                  pl.BlockSpec((kb, d), const, pipeline_mode=once),
                  pl.BlockSpec((kb, d), const, pipeline_mode=once),
                  pl.BlockSpec((d, d), const, pipeline_mode=once),
                  pl.BlockSpec((tm, d), lambda i: (i, 0)),
                  pl.BlockSpec((1, d), const),
                  pl.BlockSpec((1, d), const)],
        out_specs=[pl.BlockSpec((tm, d), lambda i: (i, 0)),
                   pl.BlockSpec((tm, d), lambda i: (i, 0)),
                   pl.BlockSpec((tm * (d // LANES), LANES), lambda i: (i, 0))],
        out_shape=[jax.ShapeDtypeStruct((n, d), F32), jax.ShapeDtypeStruct((n, d), BF16),
                   jax.ShapeDtypeStruct((n * (d // LANES), LANES), F32)],
        compiler_params=_cparams(("parallel",)),
        name="merge_outproj_ln",
    )(proj, b_gate.reshape(1, gw), y_lru, y_ssm, y_att, wl, ws, wa, wo, h,
      ln_g.reshape(1, d), ln_b.reshape(1, d))


def _seg_reduce(v, lane, op):
    for s in (1, 2, 4):
        up = pltpu.roll(v, LANES - s, axis=1)
        dn = pltpu.roll(v, s, axis=1)
        v = op(v, jnp.where((lane & s) == 0, up, dn))
    return v


def _router_kernel(h_ref, w_ref, b_ref, idx_ref, wt_ref, cnt_ref):
    tm = h_ref.shape[0]
    h_hi, h_lo = _bf16_terms(h_ref[...], 2)
    w_hi, w_lo = _bf16_terms(w_ref[...], 2)
    logits = (jnp.dot(h_hi, w_hi, preferred_element_type=F32) + jnp.dot(h_hi, w_lo, preferred_element_type=F32)
              + jnp.dot(h_lo, w_hi, preferred_element_type=F32))
    scores = jax.nn.sigmoid(logits)
    lane = lax.broadcasted_iota(jnp.int32, (tm, LANES), 1)
    lane_f = lane.astype(F32)
    real = lane < N_EXPERTS
    neg = -jnp.inf
    choice = jnp.where(real, scores + b_ref[...], neg)
    per_group = N_EXPERTS // N_EXPERT_GROUPS
    assert per_group == 8
    m1 = _seg_reduce(choice, lane, jnp.maximum)
    first = _seg_reduce(jnp.where(choice == m1, lane_f, float(LANES)), lane, jnp.minimum)
    m2 = _seg_reduce(jnp.where(lane_f == first, neg, choice), lane, jnp.maximum)
    gs = m1 + m2
    gidx = lane // per_group
    n_slots = LANES // per_group
    beaten = jnp.zeros((tm, LANES), jnp.int32)
    for k in range(1, n_slots):
        other = pltpu.roll(gs, per_group * k, axis=1)
        og = (gidx - k) & (n_slots - 1)
        wins = (other > gs) | ((other == gs) & (og < gidx))
        beaten = beaten + wins.astype(jnp.int32)
    masked = jnp.where((beaten < TOPK_GROUPS) & real, choice, neg)
    sel_i = jnp.zeros((tm, LANES), F32)
    sel_w = jnp.zeros((tm, LANES), F32)
    picked = jnp.zeros((tm, LANES), F32)
    for k in range(TOP_K):
        m = jnp.max(masked, axis=1, keepdims=True)
        am = jnp.min(jnp.where(masked == m, lane_f, float(LANES)), axis=1, keepdims=True)
        hit = lane_f == am
        wk = jnp.sum(jnp.where(hit, scores, 0.0), axis=1, keepdims=True)
        sel_i = jnp.where(lane == k, am, sel_i)
        sel_w = jnp.where(lane == k, wk, sel_w)
        picked = picked + jnp.where(hit, 1.0, 0.0)
        masked = jnp.where(hit, neg, masked)
    wsum = jnp.sum(sel_w, axis=1, keepdims=True)
    idx_ref[...] = sel_i.astype(jnp.int32)
    wt_ref[...] = sel_w / wsum * ROUTED_SCALE
    cnt_ref[...] = jnp.broadcast_to(jnp.sum(picked, axis=0, keepdims=True), cnt_ref.shape)


def _router(h, router_w, router_bias):
    n, d = h.shape
    tm = 512
    pad_e = LANES - N_EXPERTS
    w = jnp.pad(router_w, ((0, 0), (0, pad_e)))
    b = jnp.pad(router_bias, (0, pad_e)).reshape(1, LANES)
    return pl.pallas_call(
        _router_kernel,
        grid=(n // tm,),
        in_specs=[pl.BlockSpec((tm, d), lambda i: (i, 0)),
                  pl.BlockSpec((d, LANES), lambda i: (0, 0)),
                  pl.BlockSpec((1, LANES), lambda i: (0, 0))],
        out_specs=[pl.BlockSpec((tm, LANES), lambda i: (i, 0)),
                   pl.BlockSpec((tm, LANES), lambda i: (i, 0)),
                   pl.BlockSpec((SUBLANES, LANES), lambda i: (i, 0))],
        out_shape=[jax.ShapeDtypeStruct((n, LANES), jnp.int32), jax.ShapeDtypeStruct((n, LANES), F32),
                   jax.ShapeDtypeStruct((n // tm * SUBLANES, LANES), F32)],
        compiler_params=_cparams(("parallel",)),
        name="moe_router",
    )(h, w, b)


def _pack_bf16_pair(lo, hi):
    lo_bits = lax.bitcast_convert_type(lo.astype(BF16).astype(F32), jnp.uint32)
    hi_bits = lax.bitcast_convert_type(hi.astype(BF16).astype(F32), jnp.uint32)
    return (hi_bits & jnp.uint32(0xFFFF0000)) | (lo_bits >> 16)


def _unpack_bf16_pair(w):
    lo = lax.bitcast_convert_type(w << 16, F32)
    hi = lax.bitcast_convert_type(w & jnp.uint32(0xFFFF0000), F32)
    return lo, hi


def _expert_kernel(be_ref, nv_ref, first_ref, wslot_ref, nexte_ref, tab_ref,
                   h_hbm, w1_hbm, w3_hbm, w2_hbm, slots_hbm,
                   xbuf, ybuf, wf1, wf3, wf2, w1_ref, w3_ref, w2_ref, gsem, ssem, wsem, *, layer):
    i = pl.program_id(0)
    n_valid = nv_ref[0]
    rows = tab_ref.shape[2] // 4

    def weight_copies(e, ws):
        return (pltpu.make_async_copy(w1_hbm.at[layer, e], wf1.at[ws], wsem.at[ws]),
                pltpu.make_async_copy(w3_hbm.at[layer, e], wf3.at[ws], wsem.at[ws]),
                pltpu.make_async_copy(w2_hbm.at[layer, e], wf2.at[ws], wsem.at[ws]))

    @pl.when((i == 0) & (n_valid > 0))
    def _():
        for c in weight_copies(be_ref[0], 0):
            c.start(priority=1)

    for ws in range(2):
        @pl.when((i < n_valid) & (first_ref[i] == 1) & (wslot_ref[i] == ws))
        def _(ws=ws):
            for c in weight_copies(0, ws):
                c.wait()

            @pl.when(nexte_ref[i] >= 0)
            def _():
                for c in weight_copies(nexte_ref[i], 1 - ws):
                    c.start(priority=1)

            w1_ref[...] = wf1[ws].astype(BF16)
            w3_ref[...] = wf3[ws].astype(BF16)
            w2_ref[...] = wf2[ws].astype(BF16)

    kx = xbuf.shape[1] // rows
    ky = ybuf.shape[1] // rows
    n_real = slots_hbm.shape[0] - 2 * rows * ky
    depth = xbuf.shape[0]
    slot = lax.rem(i, depth)

    def gather(ahead_blocks, s, r):
        tok = tab_ref[0, 0, ahead_blocks * rows + r]
        return pltpu.make_async_copy(h_hbm.at[pl.ds(pl.multiple_of(tok * kx, kx), kx), :],
                                     xbuf.at[s, pl.ds(pl.multiple_of(r * kx, kx), kx), :], gsem.at[s])

    def scatter(s, r):
        dst = tab_ref[0, 0, 3 * rows + r]
        return pltpu.make_async_copy(ybuf.at[s, pl.ds(pl.multiple_of(r * ky, ky), ky), :],
                                     slots_hbm.at[pl.ds(pl.multiple_of(dst * ky, ky), ky), :], ssem.at[s])

    def wait_gathers(s):
        pltpu.make_async_copy(h_hbm.at[pl.ds(0, rows * kx), :], xbuf.at[s], gsem.at[s]).wait()

    def wait_scatters(s):
        pltpu.make_async_copy(ybuf.at[s], slots_hbm.at[pl.ds(0, rows * ky), :], ssem.at[s]).wait()

    @pl.when((i == 0) & (n_valid > 0))
    def _():
        ybuf[0] = jnp.zeros((rows * ky, LANES), jnp.uint32)
        for p in range(2):
            pltpu.make_async_copy(ybuf.at[0], slots_hbm.at[pl.ds(n_real + p * rows * ky, rows * ky), :],
                                  ssem.at[0]).start()
        for p in range(2):
            pltpu.make_async_copy(ybuf.at[0], slots_hbm.at[pl.ds(n_real + p * rows * ky, rows * ky), :],
                                  ssem.at[0]).wait()
        for blk in range(depth - 1):
            def body(r, c, blk=blk):
                gather(blk, blk, r).start()
                return c
            lax.fori_loop(0, rows, body, 0, unroll=8)

    def step(s):
        ahead = (s + depth - 1) % depth
        wait_gathers(s)
        x = _load_token_major(xbuf.at[s], rows, kx).astype(BF16)
        for r in range(rows):
            gather(depth - 1, ahead, r).start()
        a = jnp.dot(x, w1_ref[...], preferred_element_type=F32)
        b = jnp.dot(x, w3_ref[...], preferred_element_type=F32)
        hb = (a * jax.nn.sigmoid(a) * b).astype(BF16)
        half = w2_ref.shape[1] // 2
        cw = 4 * LANES
        for c in range(half // cw):
            lo = jnp.dot(hb, w2_ref[:, c * cw:(c + 1) * cw], preferred_element_type=F32)
            hi = jnp.dot(hb, w2_ref[:, half + c * cw:half + (c + 1) * cw], preferred_element_type=F32)
            packed = _pack_bf16_pair(lo, hi)
            for j in range(cw // LANES):
                ybuf[s, pl.ds(c * (cw // LANES) + j, rows, stride=ky), :] = packed[:, j * LANES:(j + 1) * LANES]

        @pl.when(i >= 1)
        def _():
            wait_scatters(ahead)

        for r in range(rows):
            scatter(s, r).start(priority=r % 2)

        @pl.when(i + 1 == n_valid)
        def _():
            wait_scatters(s)
            for k in range(1, depth):
                wait_gathers((s + k) % depth)

    for s in range(depth):
        @pl.when((i < n_valid) & (slot == s))
        def _(s=s):
            step(s)


def _routed_experts(h_rows, plan, w1, w3, w2, layer):
    block_e, n_valid, first, wslot, next_e, row_src, row_dst = plan
    d = w1.shape[2]
    kx = d // LANES
    ky = d // 2 // LANES
    n = h_rows.shape[0] // kx
    nblk = block_e.shape[0]
    rows = MOE_ROWS
    f = w1.shape[3]
    shifted = [jnp.concatenate([row_src[k:], jnp.broadcast_to(row_src[-1:], (k, rows))]) for k in range(3)]
    table = jnp.concatenate(shifted + [row_dst], axis=1).reshape(nblk, 1, 4 * rows)
    grid_spec = pltpu.PrefetchScalarGridSpec(
        num_scalar_prefetch=5,
        grid=(nblk,),
        in_specs=[pl.BlockSpec((1, 1, 4 * rows), lambda i, *_: (i, 0, 0), memory_space=pltpu.SMEM),
                  pl.BlockSpec(memory_space=pl.ANY),
                  pl.BlockSpec(memory_space=pl.ANY),
                  pl.BlockSpec(memory_space=pl.ANY),
                  pl.BlockSpec(memory_space=pl.ANY)],
        out_specs=pl.BlockSpec(memory_space=pl.ANY),
        scratch_shapes=[pltpu.VMEM((3, rows * kx, LANES), F32),
                        pltpu.VMEM((3, rows * ky, LANES), jnp.uint32),
                        pltpu.VMEM((2, d, f), F32),
                        pltpu.VMEM((2, d, f), F32),
                        pltpu.VMEM((2, f, d), F32),
                        pltpu.VMEM((d, f), BF16),
                        pltpu.VMEM((d, f), BF16),
                        pltpu.VMEM((f, d), BF16),
                        pltpu.SemaphoreType.DMA((3,)),
                        pltpu.SemaphoreType.DMA((3,)),
                        pltpu.SemaphoreType.DMA((2,))],
    )
    return pl.pallas_call(
        functools.partial(_expert_kernel, layer=layer),
        grid_spec=grid_spec,
        out_shape=jax.ShapeDtypeStruct(((TOP_K * n + 2 * rows) * ky, LANES), jnp.uint32),
        compiler_params=_cparams(("arbitrary",)),
        name="routed_experts",
    )(block_e, n_valid, first, wslot, next_e, table, h_rows, w1, w3, w2)


def _dispatch_plan(idx, counts, n):
    rows = MOE_ROWS
    e = N_EXPERTS
    a_total = n * TOP_K
    nblk = (a_total + e * (rows - 1)) // rows
    packed = idx.reshape(a_total) * a_total + jnp.arange(a_total, dtype=jnp.int32)
    order = jnp.sort(packed) % a_total
    ar = jnp.arange(e, dtype=jnp.int32)
    blocks_e = (counts + rows - 1) // rows
    blk_end = jnp.cumsum(blocks_e)
    blk_start = blk_end - blocks_e
    start = jnp.cumsum(counts) - counts
    run_of = jnp.cumsum((counts > 0).astype(jnp.int32)) - 1
    later = (ar[None, :] > ar[:, None]) & (counts[None, :] > 0)
    next_of = jnp.min(jnp.where(later, ar[None, :], e), axis=1)
    next_of = jnp.where(next_of < e, next_of, -1)
    bi = jnp.arange(nblk, dtype=jnp.int32)
    n_valid = blk_end[-1:].astype(jnp.int32)
    live = bi < n_valid[0]
    block_e = jnp.minimum(jnp.sum((bi[:, None] >= blk_end[None, :]).astype(jnp.int32), axis=1), e - 1)
    onehot = block_e[:, None] == ar[None, :]
    pick = lambda table: jnp.sum(jnp.where(onehot, table[None, :], 0), axis=1)
    r = jnp.arange(rows, dtype=jnp.int32)[None, :]
    j = (bi - pick(blk_start))[:, None] * rows + r
    valid = (j < pick(counts)[:, None]) & live[:, None]
    a = order[jnp.clip(pick(start)[:, None] + j, 0, a_total - 1)]
    tok = a // TOP_K
    slot = a % TOP_K
    row_src = jnp.where(valid, tok, 0).astype(jnp.int32)
    dump = TOP_K * n + (bi % 2)[:, None] * rows + r
    row_dst = jnp.where(valid, slot * n + tok, dump).astype(jnp.int32)
    first = (live & (bi == pick(blk_start))).astype(jnp.int32)
    wslot = (pick(run_of) % 2).astype(jnp.int32)
    next_e = jnp.where(live, pick(next_of), -1).astype(jnp.int32)
    return block_e, n_valid, first, wslot, next_e, row_src, row_dst


def _combine_kernel(*refs):
    slot_refs = refs[:TOP_K]
    wt_ref, hb_ref, h_ref, w1_ref, w3_ref, w2_ref, lg_ref, lb_ref, hn_ref, hbn_ref = refs[TOP_K:]
    tm, d = h_ref.shape
    ky = d // 2 // LANES
    lo = None
    hi = None
    for k, s_ref in enumerate(slot_refs):
        l, u = _unpack_bf16_pair(_load_token_major(s_ref, tm, ky))
        wk = wt_ref[:, k:k + 1]
        lo = wk * l if lo is None else lo + wk * l
        hi = wk * u if hi is None else hi + wk * u
    routed = jnp.concatenate([lo, hi], axis=1)
    x = hb_ref[...]
    a = jnp.dot(x, w1_ref[...], preferred_element_type=F32)
    b = jnp.dot(x, w3_ref[...], preferred_element_type=F32)
    shared = jnp.dot((a * jax.nn.sigmoid(a) * b).astype(BF16), w2_ref[...], preferred_element_type=F32)
    hn = _layer_norm(DEEPNORM_ALPHA * h_ref[...] + routed + shared, lg_ref[...], lb_ref[...])
    hn_ref[...] = hn
    hbn_ref[...] = hn.astype(BF16)


def _combine_shared_ln(slots, wts, hb, h, ws1, ws3, ws2, ln_g, ln_b):
    n, d = h.shape
    tm = 256
    ky = d // 2 // LANES
    f = ws1.shape[1]
    nt = n // tm
    const = lambda i: (0, 0)
    slot_specs = [pl.BlockSpec((tm * ky, LANES), lambda i, k=k: (k * nt + i, 0)) for k in range(TOP_K)]
    return pl.pallas_call(
        _combine_kernel,
        grid=(nt,),
        in_specs=slot_specs + [pl.BlockSpec((tm, LANES), lambda i: (i, 0)),
                               pl.BlockSpec((tm, d), lambda i: (i, 0)),
                               pl.BlockSpec((tm, d), lambda i: (i, 0)),
                               pl.BlockSpec((d, f), const),
                               pl.BlockSpec((d, f), const),
                               pl.BlockSpec((f, d), const),
                               pl.BlockSpec((1, d), const),
                               pl.BlockSpec((1, d), const)],
        out_specs=[pl.BlockSpec((tm, d), lambda i: (i, 0)),
                   pl.BlockSpec((tm, d), lambda i: (i, 0))],
        out_shape=[jax.ShapeDtypeStruct((n, d), F32), jax.ShapeDtypeStruct((n, d), BF16)],
        compiler_params=_cparams(("parallel",)),
        name="combine_shared_ln",
    )(*([slots] * TOP_K), wts, hb, h, ws1, ws3, ws2, ln_g.reshape(1, d), ln_b.reshape(1, d))


def kernel(x, emb_ln_g, emb_ln_b, w_in, b_gate, lru_conv_w, lru_conv_b, lru_wr, lru_br, lru_wi, lru_bi, lru_lambda, ssm_conv_w, ssm_conv_b, ssm_dt_bias, ssm_a_log, ssm_d, ssm_norm_g, w_proj_lru, w_proj_ssm, w_proj_att, w_out, ln1_g, ln1_b, router_w, router_bias, w1, w3, w2, ws1, ws3, ws2, ln2_g, ln2_b):
    batch, seq, d = x.shape
    n = batch * seq
    slopes = (2.0 ** (-8.0 * jnp.arange(1, ATT_Q_HEADS + 1, dtype=F32) / ATT_Q_HEADS)).reshape(ATT_GROUPS, ATT_KV_HEADS)
    h, hb = _embed_ln(x.reshape(n, d), emb_ln_g, emb_ln_b)
    w_in_t = jnp.swapaxes(w_in, 1, 2)
    for l in range(DEPTH):
        proj = _in_proj(hb, w_in_t, l)
        qkv = _qkv_proj(hb, w_in_t, l)
        y_lru = _rglru(proj, lru_conv_w[l], lru_conv_b[l], lru_wr[l].astype(BF16), lru_br[l],
                       lru_wi[l].astype(BF16), lru_bi[l], lru_lambda[l], batch, seq)
        y_ssm = _ssd(proj, ssm_conv_w[l], ssm_conv_b[l], ssm_dt_bias[l], ssm_a_log[l], ssm_d[l],
                     ssm_norm_g[l], batch, seq)
        y_att = _attention(qkv, slopes, batch, seq)
        h, hb, h_rows = _merge_outproj_ln(proj, b_gate[l], y_lru, y_ssm, y_att,
                                          w_proj_lru[l].astype(BF16), w_proj_ssm[l].astype(BF16),
                                          w_proj_att[l].astype(BF16), w_out[l].astype(BF16), h, ln1_g[l], ln1_b[l])
        idx, wts, tile_counts = _router(h, router_w[l], router_bias[l])
        counts = jnp.sum(tile_counts.reshape(-1, SUBLANES, LANES)[:, 0, :N_EXPERTS], axis=0).astype(jnp.int32)
        plan = _dispatch_plan(idx[:, :TOP_K], counts, n)
        slots = _routed_experts(h_rows, plan, w1, w3, w2, l)
        h, hb = _combine_shared_ln(slots, wts, hb, h, ws1[l].astype(BF16), ws3[l].astype(BF16),
                                   ws2[l].astype(BF16), ln2_g[l], ln2_b[l])
    return h.reshape(batch, seq, d)
```

```python
import functools

import jax
import jax.numpy as jnp
from jax import lax
from jax.experimental import pallas as pl
from jax.experimental.pallas import tpu as pltpu

F32 = jnp.float32
BF16 = jnp.bfloat16
HIGHEST = lax.Precision.HIGHEST

D_MODEL = 2048
DEPTH = 2
D_RNN = 1024
LRU_BLOCKS = 8
LRU_CONV = 4
LRU_C = 8.0
SSM_D_INNER = 1024
SSM_HEAD_DIM = 64
SSM_HEADS = SSM_D_INNER // SSM_HEAD_DIM
SSM_GROUPS = 2
SSM_D_STATE = 128
SSM_CONV = 4
SSM_CHUNK = 128
SSM_XBC = SSM_D_INNER + 2 * SSM_GROUPS * SSM_D_STATE
ATT_HEAD_DIM = 128
ATT_KV_HEADS = 8
ATT_PATTERNS = ((128, 1), (512, 4), (2048, 16))
ATT_GROUPS = len(ATT_PATTERNS)
ATT_Q_HEADS = ATT_GROUPS * ATT_KV_HEADS
ATT_BLOCK = 128
ATT_D_OUT = ATT_KV_HEADS * ATT_HEAD_DIM
N_BRANCH = 3
N_EXPERTS = 64
EXPERT_DIM = 512
TOP_K = 8
N_EXPERT_GROUPS = 8
TOPK_GROUPS = 4
ROUTED_SCALE = 2.5
DEEPNORM_ALPHA = (2 * DEPTH) ** 0.25
LN_EPS = 1e-5
RMS_EPS = 1e-6

LANES = 128
SUBLANES = 8
VMEM_LIMIT_BYTES = 56 * 1024 * 1024

IN_TILE = 1024
COL_LRU_X = N_BRANCH * D_MODEL
COL_LRU_G = COL_LRU_X + D_RNN
COL_SSM_Z = COL_LRU_G + D_RNN
COL_XBC = COL_SSM_Z + SSM_D_INNER
COL_DT = COL_XBC + SSM_XBC
COL_Q = COL_DT + SSM_HEADS
MAIN_COLS = -(-COL_Q // IN_TILE) * IN_TILE

MOE_ROWS = 256


def _cparams(sem):
    return pltpu.CompilerParams(dimension_semantics=sem, vmem_limit_bytes=VMEM_LIMIT_BYTES)


_NT = (((1,), (1,)), ((), ()))


def _bf16_terms(x, n):
    terms = []
    for _ in range(n):
        t = x.astype(BF16)
        terms.append(t)
        x = x - t.astype(F32)
    return terms


def _layer_norm(x, g, b):
    mu = jnp.mean(x, axis=-1, keepdims=True)
    xc = x - mu
    var = jnp.mean(xc * xc, axis=-1, keepdims=True)
    return xc * lax.rsqrt(var + LN_EPS) * g + b


def _store_token_major(ref, val):
    rows, w = val.shape
    k = w // LANES
    for j in range(k):
        ref[pl.ds(j, rows, stride=k), :] = val[:, j * LANES:(j + 1) * LANES]


def _load_token_major(ref, rows, k):
    return jnp.concatenate([ref[pl.ds(j, rows, stride=k), :] for j in range(k)], axis=1)


def _ln_kernel(x_ref, g_ref, b_ref, h_ref, hb_ref):
    y = _layer_norm(x_ref[...], g_ref[...], b_ref[...])
    h_ref[...] = y
    hb_ref[...] = y.astype(BF16)


def _embed_ln(x2d, g, b):
    n, d = x2d.shape
    tm = 512
    return pl.pallas_call(
        _ln_kernel,
        grid=(n // tm,),
        in_specs=[pl.BlockSpec((tm, d), lambda i: (i, 0)),
                  pl.BlockSpec((1, d), lambda i: (0, 0)),
                  pl.BlockSpec((1, d), lambda i: (0, 0))],
        out_specs=[pl.BlockSpec((tm, d), lambda i: (i, 0)),
                   pl.BlockSpec((tm, d), lambda i: (i, 0))],
        out_shape=[jax.ShapeDtypeStruct((n, d), F32), jax.ShapeDtypeStruct((n, d), BF16)],
        compiler_params=_cparams(("parallel",)),
        name="embed_ln",
    )(x2d, g.reshape(1, d), b.reshape(1, d))


def _in_proj_kernel(a_ref, wt_ref, o_ref, wb_ref):
    @pl.when(pl.program_id(1) == 0)
    def _():
        wb_ref[...] = wt_ref[...].astype(BF16)

    o_ref[...] = lax.dot_general(a_ref[...], wb_ref[...], _NT, preferred_element_type=F32).astype(o_ref.dtype)


def _in_proj(hb, w_in_t, layer):
    m, k = hb.shape
    tm = 2048
    tn = IN_TILE
    return pl.pallas_call(
        _in_proj_kernel,
        grid=(MAIN_COLS // tn, m // tm),
        in_specs=[pl.BlockSpec((tm, k), lambda j, i: (i, 0)),
                  pl.BlockSpec((None, tn, k), lambda j, i: (layer, j, 0))],
        out_specs=pl.BlockSpec((tm, tn), lambda j, i: (i, j)),
        out_shape=jax.ShapeDtypeStruct((m, MAIN_COLS), BF16),
        scratch_shapes=[pltpu.VMEM((tn, k), BF16)],
        compiler_params=_cparams(("parallel", "arbitrary")),
        name="in_proj",
    )(hb, w_in_t)


def _qkv_proj_kernel(a_ref, wt_hbm, o_ref, wf_ref, wb_ref, sem, *, layer, row0):
    @pl.when(pl.program_id(1) == 0)
    def _():
        rows = pl.ds(pl.multiple_of(row0 + pl.program_id(0) * wf_ref.shape[0], SUBLANES), wf_ref.shape[0])
        cp = pltpu.make_async_copy(wt_hbm.at[layer, rows, :], wf_ref, sem)
        cp.start()
        cp.wait()
        wb_ref[...] = wf_ref[...].astype(BF16)

    o_ref[...] = lax.dot_general(a_ref[...], wb_ref[...], _NT, preferred_element_type=F32)


def _qkv_proj(hb, w_in_t, layer):
    n, k = hb.shape
    tm = 1024
    tn = 1024
    ncols = ATT_Q_HEADS * ATT_HEAD_DIM + 2 * ATT_D_OUT
    return pl.pallas_call(
        functools.partial(_qkv_proj_kernel, layer=layer, row0=COL_Q),
        grid=(ncols // tn, n // tm),
        in_specs=[pl.BlockSpec((tm, k), lambda j, i: (i, 0)),
                  pl.BlockSpec(memory_space=pl.ANY)],
        out_specs=pl.BlockSpec((tm, tn), lambda j, i: (i, j)),
        out_shape=jax.ShapeDtypeStruct((n, ncols), F32),
        scratch_shapes=[pltpu.VMEM((tn, k), F32), pltpu.VMEM((tn, k), BF16), pltpu.SemaphoreType.DMA(())],
        compiler_params=_cparams(("parallel", "arbitrary")),
        name="qkv_proj",
    )(hb, w_in_t)


ATT_BATCH = 8


def _attn_kernel(slopes_ref, q0_ref, q1_ref, q2_ref, k_ref, v_ref, o_ref, kb_scr, vb_scr, res_scr, lse_scr):
    h = pl.program_id(1)
    t = o_ref.shape[0]
    blk = ATT_BLOCK
    scale = ATT_HEAD_DIM ** -0.5
    qi = lax.broadcasted_iota(jnp.int32, (blk, 2 * blk), 0)
    kj = lax.broadcasted_iota(jnp.int32, (blk, 2 * blk), 1)
    dist = blk + qi - kj
    q_refs = (q0_ref, q1_ref, q2_ref)

    def rows_of(d, r, u0, count):
        return slice(u0, u0 + count) if d == 1 else pl.ds(r + d * u0, count, stride=d)

    for g, (_, d) in enumerate(ATT_PATTERNS):
        u = t // d
        for r in range(d):
            kb_scr[g, r * u:(r + 1) * u, :] = k_ref[rows_of(d, r, 0, u), :].astype(BF16)
            vb_scr[g, r * u:(r + 1) * u, :] = v_ref[rows_of(d, r, 0, u), :].astype(BF16)

    for g, (window, d) in enumerate(ATT_PATTERNS):
        reach = window // d
        assert reach <= blk
        valid = (dist >= 0) & (dist <= reach)
        slope = slopes_ref[g, h]
        bias = jnp.where(valid, -(slope * d) * dist.astype(F32), -jnp.inf)
        bias_cur = bias[:, blk:]
        nb = (t // d) // blk
        q_ref = q_refs[g]
        for f0 in range(0, t // blk, ATT_BATCH):
            fs = list(range(f0, f0 + ATT_BATCH))
            scores, values = [], []
            for f in fs:
                r, i = divmod(f, nb)
                qb = q_ref[rows_of(d, r, i * blk, blk), :].astype(BF16)
                if i > 0:
                    rows, b = slice((f - 1) * blk, (f + 1) * blk), bias
                else:
                    rows, b = slice(f * blk, (f + 1) * blk), bias_cur
                scores.append(lax.dot_general(qb, kb_scr[g, rows, :], _NT, preferred_element_type=F32) * scale + b)
                values.append(vb_scr[g, rows, :])
            ms = [jnp.max(s, axis=-1, keepdims=True) for s in scores]
            ps = [jnp.exp(s - m) for s, m in zip(scores, ms)]
            ls = [jnp.sum(p, axis=-1, keepdims=True) for p in ps]
            pbs = [p.astype(BF16) for p in ps]
            for f, pb, vc, m, l in zip(fs, pbs, values, ms, ls):
                r, i = divmod(f, nb)
                rows = rows_of(d, r, i * blk, blk)
                res_scr[g, rows, :] = jnp.dot(pb, vc, preferred_element_type=F32) / l
                lse_scr[g, rows, :] = jnp.broadcast_to(m + jnp.log(l), (blk, LANES))
    ch = 256
    for c in range(t // ch):
        rows = slice(c * ch, (c + 1) * ch)
        lses = [lse_scr[g, rows, :] for g in range(ATT_GROUPS)]
        mx = functools.reduce(jnp.maximum, lses)
        ws = [jnp.exp(x - mx) for x in lses]
        num = sum(w * res_scr[g, rows, :] for g, w in enumerate(ws))
        o_ref[rows, :] = (num / sum(ws)).astype(o_ref.dtype)


def _attention(qkv, slopes, batch, seq):
    n = qkv.shape[0]
    hd = ATT_HEAD_DIM
    nh = ATT_KV_HEADS
    q_specs = [pl.BlockSpec((seq, hd), lambda b, h, g=g: (b, g * nh + h)) for g in range(ATT_GROUPS)]
    k_spec = pl.BlockSpec((seq, hd), lambda b, h: (b, ATT_Q_HEADS + h))
    v_spec = pl.BlockSpec((seq, hd), lambda b, h: (b, ATT_Q_HEADS + nh + h))
    return pl.pallas_call(
        _attn_kernel,
        grid=(batch, nh),
        in_specs=[pl.BlockSpec(memory_space=pltpu.SMEM)] + q_specs + [k_spec, v_spec],
        out_specs=pl.BlockSpec((seq, hd), lambda b, h: (b, h)),
        out_shape=jax.ShapeDtypeStruct((n, nh * hd), BF16),
        scratch_shapes=[pltpu.VMEM((ATT_GROUPS, seq, hd), BF16),
                        pltpu.VMEM((ATT_GROUPS, seq, hd), BF16),
                        pltpu.VMEM((ATT_GROUPS, seq, hd), F32),
                        pltpu.VMEM((ATT_GROUPS, seq, LANES), F32)],
        compiler_params=_cparams(("parallel", "parallel")),
        name="dilated_attention",
    )(slopes, qkv, qkv, qkv, qkv, qkv)


def _scan8(a, u, carry, row):
    for s in (1, 2, 4):
        a_sh = pltpu.roll(a, s, axis=0)
        u_sh = pltpu.roll(u, s, axis=0)
        m = row >= s
        u = jnp.where(m, a * u_sh + u, u)
        a = jnp.where(m, a * a_sh, a)
    return u + a * carry


def _lru_kernel(x_ref, g_ref, cw_ref, cb_ref, wr_ref, br_ref, wi_ref, bi_ref, lam_ref, o_ref,
                xpad_scr, a_scr, u_scr):
    t, c = o_ref.shape
    nb = LRU_BLOCKS
    bs = c // nb
    ch = 256
    pad = SUBLANES
    xpad_scr[0:pad, :] = jnp.zeros((pad, c), F32)
    for k in range(t // ch):
        xpad_scr[pad + k * ch:pad + (k + 1) * ch, :] = x_ref[k * ch:(k + 1) * ch, :].astype(F32)
    neg_lam = -lam_ref[...]
    sp = jnp.maximum(neg_lam, 0.0) + jnp.log1p(jnp.exp(-jnp.abs(neg_lam)))
    for k in range(t // ch):
        base = k * ch
        xc = cb_ref[...] + sum(
            cw_ref[j:j + 1, :] * xpad_scr[base + pad - (LRU_CONV - 1) + j:base + pad - (LRU_CONV - 1) + j + ch, :]
            for j in range(LRU_CONV))
        xcb = xc.astype(BF16)
        for n in range(nb):
            cols = slice(n * bs, (n + 1) * bs)
            xn = xcb[:, cols]
            r = jax.nn.sigmoid(jnp.dot(xn, wr_ref[n], preferred_element_type=F32) + br_ref[:, cols])
            ig = jax.nn.sigmoid(jnp.dot(xn, wi_ref[n], preferred_element_type=F32) + bi_ref[:, cols])
            log_a = -LRU_C * r * sp[:, cols]
            th = jnp.tanh(log_a)
            a_scr[base:base + ch, cols] = jnp.exp(log_a)
            u_scr[base:base + ch, cols] = jnp.sqrt(-2.0 * th / (1.0 - th)) * ig * xc[:, cols]
    row = lax.broadcasted_iota(jnp.int32, (SUBLANES, c), 0)

    def step(j, carry):
        rows = pl.ds(pl.multiple_of(j * SUBLANES, SUBLANES), SUBLANES)
        h8 = _scan8(a_scr[rows, :], u_scr[rows, :], carry, row)
        u_scr[rows, :] = h8
        return jnp.broadcast_to(h8[SUBLANES - 1:SUBLANES, :], (SUBLANES, c))

    lax.fori_loop(0, t // SUBLANES, step, jnp.zeros((SUBLANES, c), F32))
    for k in range(t // ch):
        rows = slice(k * ch, (k + 1) * ch)
        o_ref[rows, :] = (u_scr[rows, :] * jax.nn.gelu(g_ref[rows, :].astype(F32))).astype(o_ref.dtype)


def _rglru(proj, cw, cb, wr, br, wi, bi, lam, batch, seq):
    n = proj.shape[0]
    c = D_RNN
    bs = c // LRU_BLOCKS
    row = lambda b: (0, 0)
    return pl.pallas_call(
        _lru_kernel,
        grid=(batch,),
        in_specs=[pl.BlockSpec((seq, c), lambda b: (b, COL_LRU_X // c)),
                  pl.BlockSpec((seq, c), lambda b: (b, COL_LRU_G // c)),
                  pl.BlockSpec((LRU_CONV, c), row),
                  pl.BlockSpec((1, c), row),
                  pl.BlockSpec((LRU_BLOCKS, bs, bs), lambda b: (0, 0, 0)),
                  pl.BlockSpec((1, c), row),
                  pl.BlockSpec((LRU_BLOCKS, bs, bs), lambda b: (0, 0, 0)),
                  pl.BlockSpec((1, c), row),
                  pl.BlockSpec((1, c), row)],
        out_specs=pl.BlockSpec((seq, c), lambda b: (b, 0)),
        out_shape=jax.ShapeDtypeStruct((n, c), BF16),
        scratch_shapes=[pltpu.VMEM((seq + SUBLANES, c), F32),
                        pltpu.VMEM((seq, c), F32),
                        pltpu.VMEM((seq, c), F32)],
        compiler_params=_cparams(("parallel",)),
        name="rglru",
    )(proj, proj, cw, cb.reshape(1, c), wr, br.reshape(1, c), wi, bi.reshape(1, c), lam.reshape(1, c))


def _ssd_kernel(z_ref, xbc_ref, dtr_ref, cw_ref, cb_ref, dtb_ref, alog_ref, dskip_ref, ng_ref, o_ref,
                xpad_scr, st_scr):
    L = o_ref.shape[0]
    di = SSM_D_INNER
    ns = SSM_D_STATE
    pad = SUBLANES
    c = pl.program_id(1)

    @pl.when(c == 0)
    def _():
        xpad_scr[0:pad, :] = jnp.zeros((pad, SSM_XBC), F32)
        st_scr[...] = jnp.zeros(st_scr.shape, F32)

    xpad_scr[pad:pad + L, :] = xbc_ref[...].astype(F32)
    xc = cb_ref[...] + sum(
        cw_ref[j:j + 1, :] * xpad_scr[pad - (SSM_CONV - 1) + j:pad - (SSM_CONV - 1) + j + L, :]
        for j in range(SSM_CONV))
    xpad_scr[0:pad, :] = xpad_scr[L:L + pad, :]
    xc = xc * jax.nn.sigmoid(xc)
    xs = xc[:, :di]

    dt_in = dtr_ref[...].astype(F32) + dtb_ref[...]
    dt = jnp.maximum(dt_in, 0.0) + jnp.log1p(jnp.exp(-jnp.abs(dt_in)))
    adt = dt * (-jnp.exp(alog_ref[...]))
    ri = lax.broadcasted_iota(jnp.int32, (L, L), 0)
    ci = lax.broadcasted_iota(jnp.int32, (L, L), 1)
    causal = ri >= ci
    acum = jnp.dot(causal.astype(F32), adt, precision=HIGHEST, preferred_element_type=F32)
    acum_t = acum.T
    a_last = acum[L - 1:L, :]
    hl = lax.broadcasted_iota(jnp.int32, (LANES, di), 0)
    cl = lax.broadcasted_iota(jnp.int32, (LANES, di), 1)
    expand = (cl // SSM_HEAD_DIM == hl).astype(F32)
    dt_c = jnp.dot(dt, expand, precision=HIGHEST, preferred_element_type=F32)
    ea_c = jnp.dot(jnp.exp(acum), expand, precision=HIGHEST, preferred_element_type=F32)
    ds_c = jnp.dot(jnp.exp(a_last - acum), expand, precision=HIGHEST, preferred_element_type=F32)
    xdt = xs * dt_c
    xdt_b = xdt.astype(BF16)
    xw_b = (xdt * ds_c).astype(BF16)
    lane = lax.broadcasted_iota(jnp.int32, (L, LANES), 1)
    lo = lane < SSM_HEAD_DIM
    heads_per_group = SSM_HEADS // SSM_GROUPS
    ys = []
    for g in range(SSM_GROUPS):
        bm = xc[:, di + g * ns:di + (g + 1) * ns]
        cm = xc[:, di + SSM_GROUPS * ns + g * ns:di + SSM_GROUPS * ns + (g + 1) * ns]
        bm_b = bm.astype(BF16)
        cm_b = cm.astype(BF16)
        bm_t = bm.T.astype(BF16)
        cb = lax.dot_general(cm_b, bm_b, _NT, preferred_element_type=F32)
        for jp in range(heads_per_group // 2):
            j = g * (heads_per_group // 2) + jp
            cols = slice(j * LANES, (j + 1) * LANES)
            ms = []
            for hh in (2 * j, 2 * j + 1):
                seg = acum[:, hh:hh + 1] - acum_t[hh:hh + 1, :]
                decay = jnp.exp(jnp.where(causal, seg, -jnp.inf))
                ms.append((cb * decay).astype(BF16))
            mcat = jnp.concatenate(ms, axis=1)
            xp = xdt_b[:, cols]
            zero = jnp.zeros_like(xp)
            xcat = jnp.concatenate([jnp.where(lo, xp, zero), jnp.where(lo, zero, xp)], axis=0)
            y_diag = jnp.dot(mcat, xcat, preferred_element_type=F32)
            ent = st_scr[j]
            y_off = jnp.dot(cm_b, ent.astype(BF16), preferred_element_type=F32) * ea_c[:, cols]
            st_new = jnp.dot(bm_t, xw_b[:, cols], preferred_element_type=F32)
            st_scr[j] = st_new + ea_c[L - 1:L, cols] * ent
            ys.append(y_diag + y_off)
    y = jnp.concatenate(ys, axis=1) + dskip_ref[...] * xs
    zf = z_ref[...].astype(F32)
    y = y * (zf * jax.nn.sigmoid(zf))
    gw = di // SSM_GROUPS
    outs = []
    for g in range(SSM_GROUPS):
        yg = y[:, g * gw:(g + 1) * gw]
        outs.append(yg * lax.rsqrt(jnp.mean(yg * yg, axis=-1, keepdims=True) + RMS_EPS))
    o_ref[...] = (jnp.concatenate(outs, axis=1) * ng_ref[...]).astype(o_ref.dtype)


def _ssd(proj, cw, cb, dt_bias, a_log, d_skip, norm_g, batch, seq):
    n = proj.shape[0]
    L = SSM_CHUNK
    nc = seq // L
    di = SSM_D_INNER
    pad_h = LANES - SSM_HEADS
    dtb = jnp.pad(dt_bias, (0, pad_h)).reshape(1, LANES)
    alog = jnp.pad(a_log, (0, pad_h)).reshape(1, LANES)
    dskip = jnp.repeat(d_skip, SSM_HEAD_DIM).reshape(1, di)
    const = lambda b, c: (0, 0)
    return pl.pallas_call(
        _ssd_kernel,
        grid=(batch, nc),
        in_specs=[pl.BlockSpec((L, di), lambda b, c: (b * nc + c, COL_SSM_Z // di)),
                  pl.BlockSpec((L, SSM_XBC), lambda b, c: (b * nc + c, COL_XBC // SSM_XBC)),
                  pl.BlockSpec((L, LANES), lambda b, c: (b * nc + c, COL_DT // LANES)),
                  pl.BlockSpec((SSM_CONV, SSM_XBC), const),
                  pl.BlockSpec((1, SSM_XBC), const),
                  pl.BlockSpec((1, LANES), const),
                  pl.BlockSpec((1, LANES), const),
                  pl.BlockSpec((1, di), const),
                  pl.BlockSpec((1, di), const)],
        out_specs=pl.BlockSpec((L, di), lambda b, c: (b * nc + c, 0)),
        out_shape=jax.ShapeDtypeStruct((n, di), BF16),
        scratch_shapes=[pltpu.VMEM((L + SUBLANES, SSM_XBC), F32),
                        pltpu.VMEM((SSM_HEADS // 2, SSM_D_STATE, LANES), F32)],
        compiler_params=_cparams(("parallel", "arbitrary")),
        name="ssd",
    )(proj, proj, proj, cw, cb.reshape(1, SSM_XBC), dtb, alog, dskip, norm_g.reshape(1, di))


def _merge_kernel(g_ref, bg_ref, yl_ref, ys_ref, ya_ref, wl_ref, ws_ref, wa_ref, wo_ref,
                  h_ref, lg_ref, lb_ref, hn_ref, hb_ref, hrows_ref):
    d = h_ref.shape[1]
    merged = None
    for i, (y_ref, w_ref) in enumerate(((yl_ref, wl_ref), (ys_ref, ws_ref), (ya_ref, wa_ref))):
        gate = jax.nn.sigmoid(g_ref[:, i * d:(i + 1) * d].astype(F32) + bg_ref[:, i * d:(i + 1) * d])
        term = gate * jnp.dot(y_ref[...], w_ref[...], preferred_element_type=F32)
        merged = term if merged is None else merged + term
    mix = jnp.dot(merged.astype(BF16), wo_ref[...], preferred_element_type=F32)
    hn = _layer_norm(DEEPNORM_ALPHA * h_ref[...] + mix, lg_ref[...], lb_ref[...])
    hn_ref[...] = hn
    hb_ref[...] = hn.astype(BF16)
    half = d // 2
    _store_token_major(hrows_ref, _pack_bf16_pair(hn[:, :half], hn[:, half:]))


def _merge_outproj_ln(proj, b_gate, y_lru, y_ssm, y_att, wl, ws, wa, wo, h, ln_g, ln_b):
    n, d = h.shape
    tm = 256
    gw = N_BRANCH * d
    kb = y_lru.shape[1]
    const = lambda i: (0, 0)
    once = pl.Buffered(1)
    return pl.pallas_call(
        _merge_kernel,
        grid=(n // tm,),
        in_specs=[pl.BlockSpec((tm, gw), lambda i: (i, 0)),
                  pl.BlockSpec((1, gw), const),
                  pl.BlockSpec((tm, kb), lambda i: (i, 0)),
                  pl.BlockSpec((tm, kb), lambda i: (i, 0)),
                  pl.BlockSpec((tm, kb), lambda i: (i, 0)),
                  pl.BlockSpec((kb, d), const, pipeline_mode=once),
                  pl.BlockSpec((kb, d), const, pipeline_mode=once),
                  pl.BlockSpec((kb, d), const, pipeline_mode=once),
                  pl.BlockSpec((d, d), const, pipeline_mode=once),
                  pl.BlockSpec((tm, d), lambda i: (i, 0)),
                  pl.BlockSpec((1, d), const),
                  pl.BlockSpec((1, d), const)],
        out_specs=[pl.BlockSpec((tm, d), lambda i: (i, 0)),
                   pl.BlockSpec((tm, d), lambda i: (i, 0)),
                   pl.BlockSpec((tm * (d // 2 // LANES), LANES), lambda i: (i, 0))],
        out_shape=[jax.ShapeDtypeStruct((n, d), F32), jax.ShapeDtypeStruct((n, d), BF16),
                   jax.ShapeDtypeStruct((n * (d // 2 // LANES), LANES), jnp.uint32)],
        compiler_params=_cparams(("parallel",)),
        name="merge_outproj_ln",
    )(proj, b_gate.reshape(1, gw), y_lru, y_ssm, y_att, wl, ws, wa, wo, h,
      ln_g.reshape(1, d), ln_b.reshape(1, d))


def _seg_reduce(v, lane, op):
    for s in (1, 2, 4):
        up = pltpu.roll(v, LANES - s, axis=1)
        dn = pltpu.roll(v, s, axis=1)
        v = op(v, jnp.where((lane & s) == 0, up, dn))
    return v


def _router_kernel(h_ref, w_ref, b_ref, idx_ref, wt_ref, cnt_ref):
    tm = h_ref.shape[0]
    h_hi, h_lo = _bf16_terms(h_ref[...], 2)
    w_hi, w_lo = _bf16_terms(w_ref[...], 2)
    logits = (jnp.dot(h_hi, w_hi, preferred_element_type=F32) + jnp.dot(h_hi, w_lo, preferred_element_type=F32)
              + jnp.dot(h_lo, w_hi, preferred_element_type=F32))
    scores = jax.nn.sigmoid(logits)
    lane = lax.broadcasted_iota(jnp.int32, (tm, LANES), 1)
    lane_f = lane.astype(F32)
    real = lane < N_EXPERTS
    neg = -jnp.inf
    choice = jnp.where(real, scores + b_ref[...], neg)
    per_group = N_EXPERTS // N_EXPERT_GROUPS
    assert per_group == 8
    m1 = _seg_reduce(choice, lane, jnp.maximum)
    first = _seg_reduce(jnp.where(choice == m1, lane_f, float(LANES)), lane, jnp.minimum)
    m2 = _seg_reduce(jnp.where(lane_f == first, neg, choice), lane, jnp.maximum)
    gs = m1 + m2
    gidx = lane // per_group
    n_slots = LANES // per_group
    beaten = jnp.zeros((tm, LANES), jnp.int32)
    for k in range(1, n_slots):
        other = pltpu.roll(gs, per_group * k, axis=1)
        og = (gidx - k) & (n_slots - 1)
        wins = (other > gs) | ((other == gs) & (og < gidx))
        beaten = beaten + wins.astype(jnp.int32)
    masked = jnp.where((beaten < TOPK_GROUPS) & real, choice, neg)
    sel_i = jnp.zeros((tm, LANES), F32)
    sel_w = jnp.zeros((tm, LANES), F32)
    picked = jnp.zeros((tm, LANES), F32)
    for k in range(TOP_K):
        m = jnp.max(masked, axis=1, keepdims=True)
        am = jnp.min(jnp.where(masked == m, lane_f, float(LANES)), axis=1, keepdims=True)
        hit = lane_f == am
        wk = jnp.sum(jnp.where(hit, scores, 0.0), axis=1, keepdims=True)
        sel_i = jnp.where(lane == k, am, sel_i)
        sel_w = jnp.where(lane == k, wk, sel_w)
        picked = picked + jnp.where(hit, 1.0, 0.0)
        masked = jnp.where(hit, neg, masked)
    wsum = jnp.sum(sel_w, axis=1, keepdims=True)
    idx_ref[...] = sel_i.astype(jnp.int32)
    wt_ref[...] = sel_w / wsum * ROUTED_SCALE
    cnt_ref[...] = jnp.broadcast_to(jnp.sum(picked, axis=0, keepdims=True), cnt_ref.shape)


def _router(h, router_w, router_bias):
    n, d = h.shape
    tm = 512
    pad_e = LANES - N_EXPERTS
    w = jnp.pad(router_w, ((0, 0), (0, pad_e)))
    b = jnp.pad(router_bias, (0, pad_e)).reshape(1, LANES)
    return pl.pallas_call(
        _router_kernel,
        grid=(n // tm,),
        in_specs=[pl.BlockSpec((tm, d), lambda i: (i, 0)),
                  pl.BlockSpec((d, LANES), lambda i: (0, 0)),
                  pl.BlockSpec((1, LANES), lambda i: (0, 0))],
        out_specs=[pl.BlockSpec((tm, LANES), lambda i: (i, 0)),
                   pl.BlockSpec((tm, LANES), lambda i: (i, 0)),
                   pl.BlockSpec((SUBLANES, LANES), lambda i: (i, 0))],
        out_shape=[jax.ShapeDtypeStruct((n, LANES), jnp.int32), jax.ShapeDtypeStruct((n, LANES), F32),
                   jax.ShapeDtypeStruct((n // tm * SUBLANES, LANES), F32)],
        compiler_params=_cparams(("parallel",)),
        name="moe_router",
    )(h, w, b)


def _pack_bf16_pair(lo, hi):
    lo_bits = lax.bitcast_convert_type(lo.astype(BF16).astype(F32), jnp.uint32)
    hi_bits = lax.bitcast_convert_type(hi.astype(BF16).astype(F32), jnp.uint32)
    return (hi_bits & jnp.uint32(0xFFFF0000)) | (lo_bits >> 16)


def _unpack_bf16_pair(w):
    lo = lax.bitcast_convert_type(w << 16, F32)
    hi = lax.bitcast_convert_type(w & jnp.uint32(0xFFFF0000), F32)
    return lo, hi


def _expert_kernel(be_ref, nv_ref, first_ref, wslot_ref, nexte_ref, tab_ref,
                   h_hbm, w1_hbm, w3_hbm, w2_hbm, slots_hbm,
                   xbuf, ybuf, wf1, wf3, wf2, w1_ref, w3_ref, w2_ref, gsem, ssem, wsem, *, layer):
    i = pl.program_id(0)
    n_valid = nv_ref[0]
    rows = tab_ref.shape[2] // 4

    def weight_copies(e, ws):
        return (pltpu.make_async_copy(w1_hbm.at[layer, e], wf1.at[ws], wsem.at[ws]),
                pltpu.make_async_copy(w3_hbm.at[layer, e], wf3.at[ws], wsem.at[ws]),
                pltpu.make_async_copy(w2_hbm.at[layer, e], wf2.at[ws], wsem.at[ws]))

    @pl.when((i == 0) & (n_valid > 0))
    def _():
        for c in weight_copies(be_ref[0], 0):
            c.start(priority=1)

    for ws in range(2):
        @pl.when((i < n_valid) & (first_ref[i] == 1) & (wslot_ref[i] == ws))
        def _(ws=ws):
            for c in weight_copies(0, ws):
                c.wait()

            @pl.when(nexte_ref[i] >= 0)
            def _():
                for c in weight_copies(nexte_ref[i], 1 - ws):
                    c.start(priority=1)

            w1_ref[...] = wf1[ws].astype(BF16)
            w3_ref[...] = wf3[ws].astype(BF16)
            w2_ref[...] = wf2[ws].astype(BF16)

    kx = xbuf.shape[1] // rows
    ky = ybuf.shape[1] // rows
    n_real = slots_hbm.shape[0] - 2 * rows * ky
    depth = xbuf.shape[0]
    slot = lax.rem(i, depth)

    def gather(ahead_blocks, s, r):
        tok = tab_ref[0, 0, ahead_blocks * rows + r]
        return pltpu.make_async_copy(h_hbm.at[pl.ds(pl.multiple_of(tok * kx, kx), kx), :],
                                     xbuf.at[s, pl.ds(pl.multiple_of(r * kx, kx), kx), :], gsem.at[s])

    def scatter(s, r):
        dst = tab_ref[0, 0, 3 * rows + r]
        return pltpu.make_async_copy(ybuf.at[s, pl.ds(pl.multiple_of(r * ky, ky), ky), :],
                                     slots_hbm.at[pl.ds(pl.multiple_of(dst * ky, ky), ky), :], ssem.at[s])

    def wait_gathers(s):
        pltpu.make_async_copy(h_hbm.at[pl.ds(0, rows * kx), :], xbuf.at[s], gsem.at[s]).wait()

    def wait_scatters(s):
        pltpu.make_async_copy(ybuf.at[s], slots_hbm.at[pl.ds(0, rows * ky), :], ssem.at[s]).wait()

    @pl.when((i == 0) & (n_valid > 0))
    def _():
        ybuf[0] = jnp.zeros((rows * ky, LANES), jnp.uint32)
        for p in range(2):
            pltpu.make_async_copy(ybuf.at[0], slots_hbm.at[pl.ds(n_real + p * rows * ky, rows * ky), :],
                                  ssem.at[0]).start()
        for p in range(2):
            pltpu.make_async_copy(ybuf.at[0], slots_hbm.at[pl.ds(n_real + p * rows * ky, rows * ky), :],
                                  ssem.at[0]).wait()
        for blk in range(depth - 1):
            def body(r, c, blk=blk):
                gather(blk, blk, r).start()
                return c
            lax.fori_loop(0, rows, body, 0, unroll=8)

    def step(s):
        ahead = (s + depth - 1) % depth
        wait_gathers(s)
        x_lo, x_hi = _unpack_bf16_pair(_load_token_major(xbuf.at[s], rows, kx))
        x = jnp.concatenate([x_lo, x_hi], axis=1).astype(BF16)
        for r in range(rows):
            gather(depth - 1, ahead, r).start(priority=r % 2)
        a = jnp.dot(x, w1_ref[...], preferred_element_type=F32)
        b = jnp.dot(x, w3_ref[...], preferred_element_type=F32)
        hb = (a * jax.nn.sigmoid(a) * b).astype(BF16)
        half = w2_ref.shape[1] // 2
        cw = 4 * LANES
        for c in range(half // cw):
            lo = jnp.dot(hb, w2_ref[:, c * cw:(c + 1) * cw], preferred_element_type=F32)
            hi = jnp.dot(hb, w2_ref[:, half + c * cw:half + (c + 1) * cw], preferred_element_type=F32)
            packed = _pack_bf16_pair(lo, hi)
            for j in range(cw // LANES):
                ybuf[s, pl.ds(c * (cw // LANES) + j, rows, stride=ky), :] = packed[:, j * LANES:(j + 1) * LANES]

        @pl.when(i >= 1)
        def _():
            wait_scatters(ahead)

        for r in range(rows):
            scatter(s, r).start(priority=r % 2)

        @pl.when(i + 1 == n_valid)
        def _():
            wait_scatters(s)
            for k in range(1, depth):
                wait_gathers((s + k) % depth)

    for s in range(depth):
        @pl.when((i < n_valid) & (slot == s))
        def _(s=s):
            step(s)


def _routed_experts(h_rows, plan, w1, w3, w2, layer):
    block_e, n_valid, first, wslot, next_e, row_src, row_dst = plan
    d = w1.shape[2]
    kx = ky = d // 2 // LANES
    n = h_rows.shape[0] // kx
    nblk = block_e.shape[0]
    rows = MOE_ROWS
    f = w1.shape[3]
    shifted = [jnp.concatenate([row_src[k:], jnp.broadcast_to(row_src[-1:], (k, rows))]) for k in range(3)]
    table = jnp.concatenate(shifted + [row_dst], axis=1).reshape(nblk, 1, 4 * rows)
    grid_spec = pltpu.PrefetchScalarGridSpec(
        num_scalar_prefetch=5,
        grid=(nblk,),
        in_specs=[pl.BlockSpec((1, 1, 4 * rows), lambda i, *_: (i, 0, 0), memory_space=pltpu.SMEM),
                  pl.BlockSpec(memory_space=pl.ANY),
                  pl.BlockSpec(memory_space=pl.ANY),
                  pl.BlockSpec(memory_space=pl.ANY),
                  pl.BlockSpec(memory_space=pl.ANY)],
        out_specs=pl.BlockSpec(memory_space=pl.ANY),
        scratch_shapes=[pltpu.VMEM((3, rows * kx, LANES), jnp.uint32),
                        pltpu.VMEM((3, rows * ky, LANES), jnp.uint32),
                        pltpu.VMEM((2, d, f), F32),
                        pltpu.VMEM((2, d, f), F32),
                        pltpu.VMEM((2, f, d), F32),
                        pltpu.VMEM((d, f), BF16),
                        pltpu.VMEM((d, f), BF16),
                        pltpu.VMEM((f, d), BF16),
                        pltpu.SemaphoreType.DMA((3,)),
                        pltpu.SemaphoreType.DMA((3,)),
                        pltpu.SemaphoreType.DMA((2,))],
    )
    return pl.pallas_call(
        functools.partial(_expert_kernel, layer=layer),
        grid_spec=grid_spec,
        out_shape=jax.ShapeDtypeStruct(((TOP_K * n + 2 * rows) * ky, LANES), jnp.uint32),
        compiler_params=_cparams(("arbitrary",)),
        name="routed_experts",
    )(block_e, n_valid, first, wslot, next_e, table, h_rows, w1, w3, w2)


def _dispatch_plan(idx, counts, n):
    rows = MOE_ROWS
    e = N_EXPERTS
    a_total = n * TOP_K
    nblk = (a_total + e * (rows - 1)) // rows
    packed = idx.reshape(a_total) * a_total + jnp.arange(a_total, dtype=jnp.int32)
    order = jnp.sort(packed) % a_total
    ar = jnp.arange(e, dtype=jnp.int32)
    blocks_e = (counts + rows - 1) // rows
    blk_end = jnp.cumsum(blocks_e)
    blk_start = blk_end - blocks_e
    start = jnp.cumsum(counts) - counts
    run_of = jnp.cumsum((counts > 0).astype(jnp.int32)) - 1
    later = (ar[None, :] > ar[:, None]) & (counts[None, :] > 0)
    next_of = jnp.min(jnp.where(later, ar[None, :], e), axis=1)
    next_of = jnp.where(next_of < e, next_of, -1)
    bi = jnp.arange(nblk, dtype=jnp.int32)
    n_valid = blk_end[-1:].astype(jnp.int32)
    live = bi < n_valid[0]
    block_e = jnp.minimum(jnp.sum((bi[:, None] >= blk_end[None, :]).astype(jnp.int32), axis=1), e - 1)
    onehot = block_e[:, None] == ar[None, :]
    pick = lambda table: jnp.sum(jnp.where(onehot, table[None, :], 0), axis=1)
    r = jnp.arange(rows, dtype=jnp.int32)[None, :]
    j = (bi - pick(blk_start))[:, None] * rows + r
    valid = (j < pick(counts)[:, None]) & live[:, None]
    a = order[jnp.clip(pick(start)[:, None] + j, 0, a_total - 1)]
    tok = a // TOP_K
    slot = a % TOP_K
    row_src = jnp.where(valid, tok, 0).astype(jnp.int32)
    dump = TOP_K * n + (bi % 2)[:, None] * rows + r
    row_dst = jnp.where(valid, slot * n + tok, dump).astype(jnp.int32)
    first = (live & (bi == pick(blk_start))).astype(jnp.int32)
    wslot = (pick(run_of) % 2).astype(jnp.int32)
    next_e = jnp.where(live, pick(next_of), -1).astype(jnp.int32)
    return block_e, n_valid, first, wslot, next_e, row_src, row_dst


def _combine_kernel(*refs):
    slot_refs = refs[:TOP_K]
    wt_ref, hb_ref, h_ref, w1_ref, w3_ref, w2_ref, lg_ref, lb_ref, hn_ref, hbn_ref = refs[TOP_K:]
    tm, d = h_ref.shape
    ky = d // 2 // LANES
    lo = None
    hi = None
    for k, s_ref in enumerate(slot_refs):
        l, u = _unpack_bf16_pair(_load_token_major(s_ref, tm, ky))
        wk = wt_ref[:, k:k + 1]
        lo = wk * l if lo is None else lo + wk * l
        hi = wk * u if hi is None else hi + wk * u
    routed = jnp.concatenate([lo, hi], axis=1)
    x = hb_ref[...]
    a = jnp.dot(x, w1_ref[...], preferred_element_type=F32)
    b = jnp.dot(x, w3_ref[...], preferred_element_type=F32)
    shared = jnp.dot((a * jax.nn.sigmoid(a) * b).astype(BF16), w2_ref[...], preferred_element_type=F32)
    hn = _layer_norm(DEEPNORM_ALPHA * h_ref[...] + routed + shared, lg_ref[...], lb_ref[...])
    hn_ref[...] = hn
    hbn_ref[...] = hn.astype(BF16)


def _combine_shared_ln(slots, wts, hb, h, ws1, ws3, ws2, ln_g, ln_b):
    n, d = h.shape
    tm = 256
    ky = d // 2 // LANES
    f = ws1.shape[1]
    nt = n // tm
    const = lambda i: (0, 0)
    slot_specs = [pl.BlockSpec((tm * ky, LANES), lambda i, k=k: (k * nt + i, 0)) for k in range(TOP_K)]
    return pl.pallas_call(
        _combine_kernel,
        grid=(nt,),
        in_specs=slot_specs + [pl.BlockSpec((tm, LANES), lambda i: (i, 0)),
                               pl.BlockSpec((tm, d), lambda i: (i, 0)),
                               pl.BlockSpec((tm, d), lambda i: (i, 0)),
                               pl.BlockSpec((d, f), const),
                               pl.BlockSpec((d, f), const),
                               pl.BlockSpec((f, d), const),
                               pl.BlockSpec((1, d), const),
                               pl.BlockSpec((1, d), const)],
        out_specs=[pl.BlockSpec((tm, d), lambda i: (i, 0)),
                   pl.BlockSpec((tm, d), lambda i: (i, 0))],
        out_shape=[jax.ShapeDtypeStruct((n, d), F32), jax.ShapeDtypeStruct((n, d), BF16)],
        compiler_params=_cparams(("parallel",)),
        name="combine_shared_ln",
    )(*([slots] * TOP_K), wts, hb, h, ws1, ws3, ws2, ln_g.reshape(1, d), ln_b.reshape(1, d))


def kernel(x, emb_ln_g, emb_ln_b, w_in, b_gate, lru_conv_w, lru_conv_b, lru_wr, lru_br, lru_wi, lru_bi, lru_lambda, ssm_conv_w, ssm_conv_b, ssm_dt_bias, ssm_a_log, ssm_d, ssm_norm_g, w_proj_lru, w_proj_ssm, w_proj_att, w_out, ln1_g, ln1_b, router_w, router_bias, w1, w3, w2, ws1, ws3, ws2, ln2_g, ln2_b):
    batch, seq, d = x.shape
    n = batch * seq
    slopes = (2.0 ** (-8.0 * jnp.arange(1, ATT_Q_HEADS + 1, dtype=F32) / ATT_Q_HEADS)).reshape(ATT_GROUPS, ATT_KV_HEADS)
    h, hb = _embed_ln(x.reshape(n, d), emb_ln_g, emb_ln_b)
    w_in_t = jnp.swapaxes(w_in, 1, 2)
    for l in range(DEPTH):
        proj = _in_proj(hb, w_in_t, l)
        qkv = _qkv_proj(hb, w_in_t, l)
        y_lru = _rglru(proj, lru_conv_w[l], lru_conv_b[l], lru_wr[l].astype(BF16), lru_br[l],
                       lru_wi[l].astype(BF16), lru_bi[l], lru_lambda[l], batch, seq)
        y_ssm = _ssd(proj, ssm_conv_w[l], ssm_conv_b[l], ssm_dt_bias[l], ssm_a_log[l], ssm_d[l],
                     ssm_norm_g[l], batch, seq)
        y_att = _attention(qkv, slopes, batch, seq)
        h, hb, h_rows = _merge_outproj_ln(proj, b_gate[l], y_lru, y_ssm, y_att,
                                          w_proj_lru[l].astype(BF16), w_proj_ssm[l].astype(BF16),
                                          w_proj_att[l].astype(BF16), w_out[l].astype(BF16), h, ln1_g[l], ln1_b[l])
        idx, wts, tile_counts = _router(h, router_w[l], router_bias[l])
        counts = jnp.sum(tile_counts.reshape(-1, SUBLANES, LANES)[:, 0, :N_EXPERTS], axis=0).astype(jnp.int32)
        plan = _dispatch_plan(idx[:, :TOP_K], counts, n)
        slots = _routed_experts(h_rows, plan, w1, w3, w2, l)
        h, hb = _combine_shared_ln(slots, wts, hb, h, ws1[l].astype(BF16), ws3[l].astype(BF16),
                                   ws2[l].astype(BF16), ln2_g[l], ln2_b[l])
    return h.reshape(batch, seq, d)
```

```python
import functools

import jax
import jax.numpy as jnp
from jax import lax
from jax.experimental import pallas as pl
from jax.experimental.pallas import tpu as pltpu

F32 = jnp.float32
BF16 = jnp.bfloat16
HIGHEST = lax.Precision.HIGHEST

D_MODEL = 2048
DEPTH = 2
D_RNN = 1024
LRU_BLOCKS = 8
LRU_CONV = 4
LRU_C = 8.0
SSM_D_INNER = 1024
SSM_HEAD_DIM = 64
SSM_HEADS = SSM_D_INNER // SSM_HEAD_DIM
SSM_GROUPS = 2
SSM_D_STATE = 128
SSM_CONV = 4
SSM_CHUNK = 128
SSM_XBC = SSM_D_INNER + 2 * SSM_GROUPS * SSM_D_STATE
ATT_HEAD_DIM = 128
ATT_KV_HEADS = 8
ATT_PATTERNS = ((128, 1), (512, 4), (2048, 16))
ATT_GROUPS = len(ATT_PATTERNS)
ATT_Q_HEADS = ATT_GROUPS * ATT_KV_HEADS
ATT_BLOCK = 128
ATT_D_OUT = ATT_KV_HEADS * ATT_HEAD_DIM
N_BRANCH = 3
N_EXPERTS = 64
EXPERT_DIM = 512
TOP_K = 8
N_EXPERT_GROUPS = 8
TOPK_GROUPS = 4
ROUTED_SCALE = 2.5
DEEPNORM_ALPHA = (2 * DEPTH) ** 0.25
LN_EPS = 1e-5
RMS_EPS = 1e-6

LANES = 128
SUBLANES = 8
VMEM_LIMIT_BYTES = 56 * 1024 * 1024

IN_TILE = 1024
COL_LRU_X = N_BRANCH * D_MODEL
COL_LRU_G = COL_LRU_X + D_RNN
COL_SSM_Z = COL_LRU_G + D_RNN
COL_XBC = COL_SSM_Z + SSM_D_INNER
COL_DT = COL_XBC + SSM_XBC
COL_Q = COL_DT + SSM_HEADS
MAIN_COLS = -(-COL_Q // IN_TILE) * IN_TILE

MOE_ROWS = 256


def _cparams(sem):
    return pltpu.CompilerParams(dimension_semantics=sem, vmem_limit_bytes=VMEM_LIMIT_BYTES)


_NT = (((1,), (1,)), ((), ()))


def _bf16_terms(x, n):
    terms = []
    for _ in range(n):
        t = x.astype(BF16)
        terms.append(t)
        x = x - t.astype(F32)
    return terms


def _layer_norm(x, g, b):
    mu = jnp.mean(x, axis=-1, keepdims=True)
    xc = x - mu
    var = jnp.mean(xc * xc, axis=-1, keepdims=True)
    return xc * lax.rsqrt(var + LN_EPS) * g + b


def _store_token_major(ref, val):
    rows, w = val.shape
    k = w // LANES
    for j in range(k):
        ref[pl.ds(j, rows, stride=k), :] = val[:, j * LANES:(j + 1) * LANES]


def _load_token_major(ref, rows, k):
    return jnp.concatenate([ref[pl.ds(j, rows, stride=k), :] for j in range(k)], axis=1)


def _ln_kernel(x_ref, g_ref, b_ref, h_ref, hb_ref):
    y = _layer_norm(x_ref[...], g_ref[...], b_ref[...])
    h_ref[...] = y
    hb_ref[...] = y.astype(BF16)


def _embed_ln(x2d, g, b):
    n, d = x2d.shape
    tm = 512
    return pl.pallas_call(
        _ln_kernel,
        grid=(n // tm,),
        in_specs=[pl.BlockSpec((tm, d), lambda i: (i, 0)),
                  pl.BlockSpec((1, d), lambda i: (0, 0)),
                  pl.BlockSpec((1, d), lambda i: (0, 0))],
        out_specs=[pl.BlockSpec((tm, d), lambda i: (i, 0)),
                   pl.BlockSpec((tm, d), lambda i: (i, 0))],
        out_shape=[jax.ShapeDtypeStruct((n, d), F32), jax.ShapeDtypeStruct((n, d), BF16)],
        compiler_params=_cparams(("parallel",)),
        name="embed_ln",
    )(x2d, g.reshape(1, d), b.reshape(1, d))


def _in_proj_kernel(a_ref, wt_ref, o_ref, wb_ref):
    @pl.when(pl.program_id(1) == 0)
    def _():
        wb_ref[...] = wt_ref[...].astype(BF16)

    o_ref[...] = lax.dot_general(a_ref[...], wb_ref[...], _NT, preferred_element_type=F32).astype(o_ref.dtype)


def _in_proj(hb, w_in_t, layer):
    m, k = hb.shape
    tm = 2048
    tn = IN_TILE
    return pl.pallas_call(
        _in_proj_kernel,
        grid=(MAIN_COLS // tn, m // tm),
        in_specs=[pl.BlockSpec((tm, k), lambda j, i: (i, 0)),
                  pl.BlockSpec((None, tn, k), lambda j, i: (layer, j, 0))],
        out_specs=pl.BlockSpec((tm, tn), lambda j, i: (i, j)),
        out_shape=jax.ShapeDtypeStruct((m, MAIN_COLS), BF16),
        scratch_shapes=[pltpu.VMEM((tn, k), BF16)],
        compiler_params=_cparams(("parallel", "arbitrary")),
        name="in_proj",
    )(hb, w_in_t)


def _qkv_proj_kernel(a_ref, wt_hbm, o_ref, wf_ref, wb_ref, sem, *, layer, row0):
    @pl.when(pl.program_id(1) == 0)
    def _():
        rows = pl.ds(pl.multiple_of(row0 + pl.program_id(0) * wf_ref.shape[0], SUBLANES), wf_ref.shape[0])
        cp = pltpu.make_async_copy(wt_hbm.at[layer, rows, :], wf_ref, sem)
        cp.start()
        cp.wait()
        wb_ref[...] = wf_ref[...].astype(BF16)

    o_ref[...] = lax.dot_general(a_ref[...], wb_ref[...], _NT, preferred_element_type=F32)


def _qkv_proj(hb, w_in_t, layer):
    n, k = hb.shape
    tm = 2048
    tn = 1024
    ncols = ATT_Q_HEADS * ATT_HEAD_DIM + 2 * ATT_D_OUT
    return pl.pallas_call(
        functools.partial(_qkv_proj_kernel, layer=layer, row0=COL_Q),
        grid=(ncols // tn, n // tm),
        in_specs=[pl.BlockSpec((tm, k), lambda j, i: (i, 0)),
                  pl.BlockSpec(memory_space=pl.ANY)],
        out_specs=pl.BlockSpec((tm, tn), lambda j, i: (i, j)),
        out_shape=jax.ShapeDtypeStruct((n, ncols), F32),
        scratch_shapes=[pltpu.VMEM((tn, k), F32), pltpu.VMEM((tn, k), BF16), pltpu.SemaphoreType.DMA(())],
        compiler_params=_cparams(("parallel", "arbitrary")),
        name="qkv_proj",
    )(hb, w_in_t)


ATT_BATCH = 8


def _attn_kernel(slopes_ref, q0_ref, q1_ref, q2_ref, k_ref, v_ref, o_ref, kb_scr, vb_scr, res_scr, lse_scr):
    h = pl.program_id(1)
    t = o_ref.shape[0]
    blk = ATT_BLOCK
    scale = ATT_HEAD_DIM ** -0.5
    qi = lax.broadcasted_iota(jnp.int32, (blk, 2 * blk), 0)
    kj = lax.broadcasted_iota(jnp.int32, (blk, 2 * blk), 1)
    dist = blk + qi - kj
    q_refs = (q0_ref, q1_ref, q2_ref)

    def rows_of(d, r, u0, count):
        return slice(u0, u0 + count) if d == 1 else pl.ds(r + d * u0, count, stride=d)

    for g, (_, d) in enumerate(ATT_PATTERNS):
        u = t // d
        for r in range(d):
            kb_scr[g, r * u:(r + 1) * u, :] = k_ref[rows_of(d, r, 0, u), :].astype(BF16)
            vb_scr[g, r * u:(r + 1) * u, :] = v_ref[rows_of(d, r, 0, u), :].astype(BF16)

    for g, (window, d) in enumerate(ATT_PATTERNS):
        reach = window // d
        assert reach <= blk
        valid = (dist >= 0) & (dist <= reach)
        slope = slopes_ref[g, h]
        bias = jnp.where(valid, -(slope * d) * dist.astype(F32), -jnp.inf)
        bias_cur = bias[:, blk:]
        nb = (t // d) // blk
        q_ref = q_refs[g]
        for f0 in range(0, t // blk, ATT_BATCH):
            fs = list(range(f0, f0 + ATT_BATCH))
            scores, values = [], []
            for f in fs:
                r, i = divmod(f, nb)
                qb = q_ref[rows_of(d, r, i * blk, blk), :].astype(BF16)
                if i > 0:
                    rows, b = slice((f - 1) * blk, (f + 1) * blk), bias
                else:
                    rows, b = slice(f * blk, (f + 1) * blk), bias_cur
                scores.append(lax.dot_general(qb, kb_scr[g, rows, :], _NT, preferred_element_type=F32) * scale + b)
                values.append(vb_scr[g, rows, :])
            ms = [jnp.max(s, axis=-1, keepdims=True) for s in scores]
            ps = [jnp.exp(s - m) for s, m in zip(scores, ms)]
            ls = [jnp.sum(p, axis=-1, keepdims=True) for p in ps]
            pbs = [p.astype(BF16) for p in ps]
            for f, pb, vc, m, l in zip(fs, pbs, values, ms, ls):
                r, i = divmod(f, nb)
                rows = rows_of(d, r, i * blk, blk)
                res_scr[g, rows, :] = jnp.dot(pb, vc, preferred_element_type=F32) / l
                lse_scr[g, rows, :] = jnp.broadcast_to(m + jnp.log(l), (blk, LANES))
    ch = 256
    for c in range(t // ch):
        rows = slice(c * ch, (c + 1) * ch)
        lses = [lse_scr[g, rows, :] for g in range(ATT_GROUPS)]
        mx = functools.reduce(jnp.maximum, lses)
        ws = [jnp.exp(x - mx) for x in lses]
        num = sum(w * res_scr[g, rows, :] for g, w in enumerate(ws))
        o_ref[rows, :] = (num / sum(ws)).astype(o_ref.dtype)


def _attention(qkv, slopes, batch, seq):
    n = qkv.shape[0]
    hd = ATT_HEAD_DIM
    nh = ATT_KV_HEADS
    q_specs = [pl.BlockSpec((seq, hd), lambda b, h, g=g: (b, g * nh + h)) for g in range(ATT_GROUPS)]
    k_spec = pl.BlockSpec((seq, hd), lambda b, h: (b, ATT_Q_HEADS + h))
    v_spec = pl.BlockSpec((seq, hd), lambda b, h: (b, ATT_Q_HEADS + nh + h))
    return pl.pallas_call(
        _attn_kernel,
        grid=(batch, nh),
        in_specs=[pl.BlockSpec(memory_space=pltpu.SMEM)] + q_specs + [k_spec, v_spec],
        out_specs=pl.BlockSpec((seq, hd), lambda b, h: (b, h)),
        out_shape=jax.ShapeDtypeStruct((n, nh * hd), BF16),
        scratch_shapes=[pltpu.VMEM((ATT_GROUPS, seq, hd), BF16),
                        pltpu.VMEM((ATT_GROUPS, seq, hd), BF16),
                        pltpu.VMEM((ATT_GROUPS, seq, hd), F32),
                        pltpu.VMEM((ATT_GROUPS, seq, LANES), F32)],
        compiler_params=_cparams(("parallel", "parallel")),
        name="dilated_attention",
    )(slopes, qkv, qkv, qkv, qkv, qkv)


def _scan8(a, u, carry, row):
    for s in (1, 2, 4):
        a_sh = pltpu.roll(a, s, axis=0)
        u_sh = pltpu.roll(u, s, axis=0)
        m = row >= s
        u = jnp.where(m, a * u_sh + u, u)
        a = jnp.where(m, a * a_sh, a)
    return u + a * carry


def _lru_kernel(x_ref, g_ref, cw_ref, cb_ref, wr_ref, br_ref, wi_ref, bi_ref, lam_ref, o_ref,
                xpad_scr, a_scr, u_scr):
    t, c = o_ref.shape
    nb = LRU_BLOCKS
    bs = c // nb
    ch = 256
    pad = SUBLANES
    xpad_scr[0:pad, :] = jnp.zeros((pad, c), F32)
    for k in range(t // ch):
        xpad_scr[pad + k * ch:pad + (k + 1) * ch, :] = x_ref[k * ch:(k + 1) * ch, :].astype(F32)
    neg_lam = -lam_ref[...]
    sp = jnp.maximum(neg_lam, 0.0) + jnp.log1p(jnp.exp(-jnp.abs(neg_lam)))
    for k in range(t // ch):
        base = k * ch
        xc = cb_ref[...] + sum(
            cw_ref[j:j + 1, :] * xpad_scr[base + pad - (LRU_CONV - 1) + j:base + pad - (LRU_CONV - 1) + j + ch, :]
            for j in range(LRU_CONV))
        xcb = xc.astype(BF16)
        for n in range(nb):
            cols = slice(n * bs, (n + 1) * bs)
            xn = xcb[:, cols]
            r = jax.nn.sigmoid(jnp.dot(xn, wr_ref[n], preferred_element_type=F32) + br_ref[:, cols])
            ig = jax.nn.sigmoid(jnp.dot(xn, wi_ref[n], preferred_element_type=F32) + bi_ref[:, cols])
            log_a = -LRU_C * r * sp[:, cols]
            th = jnp.tanh(log_a)
            a_scr[base:base + ch, cols] = jnp.exp(log_a)
            v = -2.0 * th / (1.0 - th)
            root = jnp.where(v > 0.0, v * lax.rsqrt(v), 0.0)
            u_scr[base:base + ch, cols] = root * ig * xc[:, cols]
    row = lax.broadcasted_iota(jnp.int32, (SUBLANES, c), 0)

    def step(j, carry):
        rows = pl.ds(pl.multiple_of(j * SUBLANES, SUBLANES), SUBLANES)
        h8 = _scan8(a_scr[rows, :], u_scr[rows, :], carry, row)
        u_scr[rows, :] = h8
        return jnp.broadcast_to(h8[SUBLANES - 1:SUBLANES, :], (SUBLANES, c))

    lax.fori_loop(0, t // SUBLANES, step, jnp.zeros((SUBLANES, c), F32))
    for k in range(t // ch):
        rows = slice(k * ch, (k + 1) * ch)
        o_ref[rows, :] = (u_scr[rows, :] * jax.nn.gelu(g_ref[rows, :].astype(F32))).astype(o_ref.dtype)


def _rglru(proj, cw, cb, wr, br, wi, bi, lam, batch, seq):
    n = proj.shape[0]
    c = D_RNN
    bs = c // LRU_BLOCKS
    row = lambda b: (0, 0)
    return pl.pallas_call(
        _lru_kernel,
        grid=(batch,),
        in_specs=[pl.BlockSpec((seq, c), lambda b: (b, COL_LRU_X // c)),
                  pl.BlockSpec((seq, c), lambda b: (b, COL_LRU_G // c)),
                  pl.BlockSpec((LRU_CONV, c), row),
                  pl.BlockSpec((1, c), row),
                  pl.BlockSpec((LRU_BLOCKS, bs, bs), lambda b: (0, 0, 0)),
                  pl.BlockSpec((1, c), row),
                  pl.BlockSpec((LRU_BLOCKS, bs, bs), lambda b: (0, 0, 0)),
                  pl.BlockSpec((1, c), row),
                  pl.BlockSpec((1, c), row)],
        out_specs=pl.BlockSpec((seq, c), lambda b: (b, 0)),
        out_shape=jax.ShapeDtypeStruct((n, c), BF16),
        scratch_shapes=[pltpu.VMEM((seq + SUBLANES, c), F32),
                        pltpu.VMEM((seq, c), F32),
                        pltpu.VMEM((seq, c), F32)],
        compiler_params=_cparams(("parallel",)),
        name="rglru",
    )(proj, proj, cw, cb.reshape(1, c), wr, br.reshape(1, c), wi, bi.reshape(1, c), lam.reshape(1, c))


def _ssd_kernel(z_ref, xbc_ref, dtr_ref, cw_ref, cb_ref, dtb_ref, alog_ref, dskip_ref, ng_ref, o_ref,
                xpad_scr, st_scr):
    L = o_ref.shape[0]
    di = SSM_D_INNER
    ns = SSM_D_STATE
    pad = SUBLANES
    c = pl.program_id(1)

    @pl.when(c == 0)
    def _():
        xpad_scr[0:pad, :] = jnp.zeros((pad, SSM_XBC), F32)
        st_scr[...] = jnp.zeros(st_scr.shape, F32)

    xpad_scr[pad:pad + L, :] = xbc_ref[...].astype(F32)
    xc = cb_ref[...] + sum(
        cw_ref[j:j + 1, :] * xpad_scr[pad - (SSM_CONV - 1) + j:pad - (SSM_CONV - 1) + j + L, :]
        for j in range(SSM_CONV))
    xpad_scr[0:pad, :] = xpad_scr[L:L + pad, :]
    xc = xc * jax.nn.sigmoid(xc)
    xs = xc[:, :di]

    dt_in = dtr_ref[...].astype(F32) + dtb_ref[...]
    dt = jnp.maximum(dt_in, 0.0) + jnp.log1p(jnp.exp(-jnp.abs(dt_in)))
    adt = dt * (-jnp.exp(alog_ref[...]))
    ri = lax.broadcasted_iota(jnp.int32, (L, L), 0)
    ci = lax.broadcasted_iota(jnp.int32, (L, L), 1)
    causal = ri >= ci
    acum = jnp.dot(causal.astype(F32), adt, precision=HIGHEST, preferred_element_type=F32)
    acum_t = acum.T
    a_last = acum[L - 1:L, :]
    hl = lax.broadcasted_iota(jnp.int32, (LANES, di), 0)
    cl = lax.broadcasted_iota(jnp.int32, (LANES, di), 1)
    expand = (cl // SSM_HEAD_DIM == hl).astype(F32)
    dt_c = jnp.dot(dt, expand, precision=HIGHEST, preferred_element_type=F32)
    ea_c = jnp.dot(jnp.exp(acum), expand, precision=HIGHEST, preferred_element_type=F32)
    ds_c = jnp.dot(jnp.exp(a_last - acum), expand, precision=HIGHEST, preferred_element_type=F32)
    xdt = xs * dt_c
    xdt_b = xdt.astype(BF16)
    xw_b = (xdt * ds_c).astype(BF16)
    lane = lax.broadcasted_iota(jnp.int32, (L, LANES), 1)
    lo = lane < SSM_HEAD_DIM
    heads_per_group = SSM_HEADS // SSM_GROUPS
    ys = []
    for g in range(SSM_GROUPS):
        bm = xc[:, di + g * ns:di + (g + 1) * ns]
        cm = xc[:, di + SSM_GROUPS * ns + g * ns:di + SSM_GROUPS * ns + (g + 1) * ns]
        bm_b = bm.astype(BF16)
        cm_b = cm.astype(BF16)
        bm_t = bm.T.astype(BF16)
        cb = lax.dot_general(cm_b, bm_b, _NT, preferred_element_type=F32)
        for jp in range(heads_per_group // 2):
            j = g * (heads_per_group // 2) + jp
            cols = slice(j * LANES, (j + 1) * LANES)
            ms = []
            for hh in (2 * j, 2 * j + 1):
                seg = acum[:, hh:hh + 1] - acum_t[hh:hh + 1, :]
                decay = jnp.exp(jnp.where(causal, seg, -jnp.inf))
                ms.append((cb * decay).astype(BF16))
            mcat = jnp.concatenate(ms, axis=1)
            xp = xdt_b[:, cols]
            zero = jnp.zeros_like(xp)
            xcat = jnp.concatenate([jnp.where(lo, xp, zero), jnp.where(lo, zero, xp)], axis=0)
            y_diag = jnp.dot(mcat, xcat, preferred_element_type=F32)
            ent = st_scr[j]
            y_off = jnp.dot(cm_b, ent.astype(BF16), preferred_element_type=F32) * ea_c[:, cols]
            st_new = jnp.dot(bm_t, xw_b[:, cols], preferred_element_type=F32)
            st_scr[j] = st_new + ea_c[L - 1:L, cols] * ent
            ys.append(y_diag + y_off)
    y = jnp.concatenate(ys, axis=1) + dskip_ref[...] * xs
    zf = z_ref[...].astype(F32)
    y = y * (zf * jax.nn.sigmoid(zf))
    gw = di // SSM_GROUPS
    outs = []
    for g in range(SSM_GROUPS):
        yg = y[:, g * gw:(g + 1) * gw]
        outs.append(yg * lax.rsqrt(jnp.mean(yg * yg, axis=-1, keepdims=True) + RMS_EPS))
    o_ref[...] = (jnp.concatenate(outs, axis=1) * ng_ref[...]).astype(o_ref.dtype)


def _ssd(proj, cw, cb, dt_bias, a_log, d_skip, norm_g, batch, seq):
    n = proj.shape[0]
    L = SSM_CHUNK
    nc = seq // L
    di = SSM_D_INNER
    pad_h = LANES - SSM_HEADS
    dtb = jnp.pad(dt_bias, (0, pad_h)).reshape(1, LANES)
    alog = jnp.pad(a_log, (0, pad_h)).reshape(1, LANES)
    dskip = jnp.repeat(d_skip, SSM_HEAD_DIM).reshape(1, di)
    const = lambda b, c: (0, 0)
    return pl.pallas_call(
        _ssd_kernel,
        grid=(batch, nc),
        in_specs=[pl.BlockSpec((L, di), lambda b, c: (b * nc + c, COL_SSM_Z // di)),
                  pl.BlockSpec((L, SSM_XBC), lambda b, c: (b * nc + c, COL_XBC // SSM_XBC)),
                  pl.BlockSpec((L, LANES), lambda b, c: (b * nc + c, COL_DT // LANES)),
                  pl.BlockSpec((SSM_CONV, SSM_XBC), const),
                  pl.BlockSpec((1, SSM_XBC), const),
                  pl.BlockSpec((1, LANES), const),
                  pl.BlockSpec((1, LANES), const),
                  pl.BlockSpec((1, di), const),
                  pl.BlockSpec((1, di), const)],
        out_specs=pl.BlockSpec((L, di), lambda b, c: (b * nc + c, 0)),
        out_shape=jax.ShapeDtypeStruct((n, di), BF16),
        scratch_shapes=[pltpu.VMEM((L + SUBLANES, SSM_XBC), F32),
                        pltpu.VMEM((SSM_HEADS // 2, SSM_D_STATE, LANES), F32)],
        compiler_params=_cparams(("parallel", "arbitrary")),
        name="ssd",
    )(proj, proj, proj, cw, cb.reshape(1, SSM_XBC), dtb, alog, dskip, norm_g.reshape(1, di))


def _merge_kernel(g_ref, bg_ref, yl_ref, ys_ref, ya_ref, wl_ref, ws_ref, wa_ref, wo_ref,
                  h_ref, lg_ref, lb_ref, hn_ref, hrows_ref):
    d = h_ref.shape[1]
    merged = None
    for i, (y_ref, w_ref) in enumerate(((yl_ref, wl_ref), (ys_ref, ws_ref), (ya_ref, wa_ref))):
        gate = jax.nn.sigmoid(g_ref[:, i * d:(i + 1) * d].astype(F32) + bg_ref[:, i * d:(i + 1) * d])
        term = gate * jnp.dot(y_ref[...], w_ref[...], preferred_element_type=F32)
        merged = term if merged is None else merged + term
    mix = jnp.dot(merged.astype(BF16), wo_ref[...], preferred_element_type=F32)
    hn = _layer_norm(DEEPNORM_ALPHA * h_ref[...] + mix, lg_ref[...], lb_ref[...])
    hn_ref[...] = hn
    half = d // 2
    _store_token_major(hrows_ref, _pack_bf16_pair(hn[:, :half], hn[:, half:]))


def _merge_outproj_ln(proj, b_gate, y_lru, y_ssm, y_att, wl, ws, wa, wo, h, ln_g, ln_b):
    n, d = h.shape
    tm = 256
    gw = N_BRANCH * d
    kb = y_lru.shape[1]
    const = lambda i: (0, 0)
    once = pl.Buffered(1)
    return pl.pallas_call(
        _merge_kernel,
        grid=(n // tm,),
        in_specs=[pl.BlockSpec((tm, gw), lambda i: (i, 0)),
                  pl.BlockSpec((1, gw), const),
                  pl.BlockSpec((tm, kb), lambda i: (i, 0)),
                  pl.BlockSpec((tm, kb), lambda i: (i, 0)),
                  pl.BlockSpec((tm, kb), lambda i: (i, 0)),
                  pl.BlockSpec((kb, d), const, pipeline_mode=once),
                  pl.BlockSpec((kb, d), const, pipeline_mode=once),
                  pl.BlockSpec((kb, d), const, pipeline_mode=once),
                  pl.BlockSpec((d, d), const, pipeline_mode=once),
                  pl.BlockSpec((tm, d), lambda i: (i, 0)),
                  pl.BlockSpec((1, d), const),
                  pl.BlockSpec((1, d), const)],
        out_specs=[pl.BlockSpec((tm, d), lambda i: (i, 0)),
                   pl.BlockSpec((tm * (d // 2 // LANES), LANES), lambda i: (i, 0))],
        out_shape=[jax.ShapeDtypeStruct((n, d), F32),
                   jax.ShapeDtypeStruct((n * (d // 2 // LANES), LANES), jnp.uint32)],
        compiler_params=_cparams(("parallel",)),
        name="merge_outproj_ln",
    )(proj, b_gate.reshape(1, gw), y_lru, y_ssm, y_att, wl, ws, wa, wo, h,
      ln_g.reshape(1, d), ln_b.reshape(1, d))


def _seg_reduce(v, lane, op):
    for s in (1, 2, 4):
        up = pltpu.roll(v, LANES - s, axis=1)
        dn = pltpu.roll(v, s, axis=1)
        v = op(v, jnp.where((lane & s) == 0, up, dn))
    return v


def _router_kernel(h_ref, w_ref, b_ref, idx_ref, wt_ref, cnt_ref):
    tm = h_ref.shape[0]
    h_hi, h_lo = _bf16_terms(h_ref[...], 2)
    w_hi, w_lo = _bf16_terms(w_ref[...], 2)
    logits = (jnp.dot(h_hi, w_hi, preferred_element_type=F32) + jnp.dot(h_hi, w_lo, preferred_element_type=F32)
              + jnp.dot(h_lo, w_hi, preferred_element_type=F32))
    scores = jax.nn.sigmoid(logits)
    lane = lax.broadcasted_iota(jnp.int32, (tm, LANES), 1)
    lane_f = lane.astype(F32)
    real = lane < N_EXPERTS
    neg = -jnp.inf
    choice = jnp.where(real, scores + b_ref[...], neg)
    per_group = N_EXPERTS // N_EXPERT_GROUPS
    assert per_group == 8
    m1 = _seg_reduce(choice, lane, jnp.maximum)
    first = _seg_reduce(jnp.where(choice == m1, lane_f, float(LANES)), lane, jnp.minimum)
    m2 = _seg_reduce(jnp.where(lane_f == first, neg, choice), lane, jnp.maximum)
    gs = m1 + m2
    gidx = lane // per_group
    n_slots = LANES // per_group
    beaten = jnp.zeros((tm, LANES), jnp.int32)
    for k in range(1, n_slots):
        other = pltpu.roll(gs, per_group * k, axis=1)
        og = (gidx - k) & (n_slots - 1)
        wins = (other > gs) | ((other == gs) & (og < gidx))
        beaten = beaten + wins.astype(jnp.int32)
    masked = jnp.where((beaten < TOPK_GROUPS) & real, choice, neg)
    sel_i = jnp.zeros((tm, LANES), F32)
    sel_w = jnp.zeros((tm, LANES), F32)
    picked = jnp.zeros((tm, LANES), F32)
    for k in range(TOP_K):
        m = jnp.max(masked, axis=1, keepdims=True)
        am = jnp.min(jnp.where(masked == m, lane_f, float(LANES)), axis=1, keepdims=True)
        hit = lane_f == am
        wk = jnp.sum(jnp.where(hit, scores, 0.0), axis=1, keepdims=True)
        sel_i = jnp.where(lane == k, am, sel_i)
        sel_w = jnp.where(lane == k, wk, sel_w)
        picked = picked + jnp.where(hit, 1.0, 0.0)
        masked = jnp.where(hit, neg, masked)
    wsum = jnp.sum(sel_w, axis=1, keepdims=True)
    idx_ref[...] = sel_i.astype(jnp.int32)
    wt_ref[...] = sel_w / wsum * ROUTED_SCALE
    cnt_ref[...] = jnp.broadcast_to(jnp.sum(picked, axis=0, keepdims=True), cnt_ref.shape)


def _router(h, router_w, router_bias):
    n, d = h.shape
    tm = 512
    pad_e = LANES - N_EXPERTS
    w = jnp.pad(router_w, ((0, 0), (0, pad_e)))
    b = jnp.pad(router_bias, (0, pad_e)).reshape(1, LANES)
    return pl.pallas_call(
        _router_kernel,
        grid=(n // tm,),
        in_specs=[pl.BlockSpec((tm, d), lambda i: (i, 0)),
                  pl.BlockSpec((d, LANES), lambda i: (0, 0)),
                  pl.BlockSpec((1, LANES), lambda i: (0, 0))],
        out_specs=[pl.BlockSpec((tm, LANES), lambda i: (i, 0)),
                   pl.BlockSpec((tm, LANES), lambda i: (i, 0)),
                   pl.BlockSpec((SUBLANES, LANES), lambda i: (i, 0))],
        out_shape=[jax.ShapeDtypeStruct((n, LANES), jnp.int32), jax.ShapeDtypeStruct((n, LANES), F32),
                   jax.ShapeDtypeStruct((n // tm * SUBLANES, LANES), F32)],
        compiler_params=_cparams(("parallel",)),
        name="moe_router",
    )(h, w, b)


def _pack_bf16_pair(lo, hi):
    lo_bits = lax.bitcast_convert_type(lo.astype(BF16).astype(F32), jnp.uint32)
    hi_bits = lax.bitcast_convert_type(hi.astype(BF16).astype(F32), jnp.uint32)
    return (hi_bits & jnp.uint32(0xFFFF0000)) | (lo_bits >> 16)


def _unpack_bf16_pair(w):
    lo = lax.bitcast_convert_type(w << 16, F32)
    hi = lax.bitcast_convert_type(w & jnp.uint32(0xFFFF0000), F32)
    return lo, hi


def _expert_kernel(be_ref, nv_ref, first_ref, wslot_ref, nexte_ref, tab_ref,
                   h_hbm, w1_hbm, w3_hbm, w2_hbm, slots_hbm,
                   xbuf, ybuf, wf1, wf3, wf2, w1_ref, w3_ref, w2_ref, gsem, ssem, wsem, *, layer):
    i = pl.program_id(0)
    n_valid = nv_ref[0]
    rows = tab_ref.shape[2] // 4

    def weight_copies(e, ws):
        return (pltpu.make_async_copy(w1_hbm.at[layer, e], wf1.at[ws], wsem.at[ws]),
                pltpu.make_async_copy(w3_hbm.at[layer, e], wf3.at[ws], wsem.at[ws]),
                pltpu.make_async_copy(w2_hbm.at[layer, e], wf2.at[ws], wsem.at[ws]))

    @pl.when((i == 0) & (n_valid > 0))
    def _():
        for c in weight_copies(be_ref[0], 0):
            c.start(priority=1)

    for ws in range(2):
        @pl.when((i < n_valid) & (first_ref[i] == 1) & (wslot_ref[i] == ws))
        def _(ws=ws):
            for c in weight_copies(0, ws):
                c.wait()

            @pl.when(nexte_ref[i] >= 0)
            def _():
                for c in weight_copies(nexte_ref[i], 1 - ws):
                    c.start(priority=1)

            w1_ref[...] = wf1[ws].astype(BF16)
            w3_ref[...] = wf3[ws].astype(BF16)
            w2_ref[...] = wf2[ws].astype(BF16)

    kx = xbuf.shape[1] // rows
    ky = ybuf.shape[1] // rows
    n_real = slots_hbm.shape[0] - 2 * rows * ky
    depth = xbuf.shape[0]
    slot = lax.rem(i, depth)

    def gather(ahead_blocks, s, r):
        tok = tab_ref[0, 0, ahead_blocks * rows + r]
        return pltpu.make_async_copy(h_hbm.at[pl.ds(pl.multiple_of(tok * kx, kx), kx), :],
                                     xbuf.at[s, pl.ds(pl.multiple_of(r * kx, kx), kx), :], gsem.at[s])

    def scatter(s, r):
        dst = tab_ref[0, 0, 3 * rows + r]
        return pltpu.make_async_copy(ybuf.at[s, pl.ds(pl.multiple_of(r * ky, ky), ky), :],
                                     slots_hbm.at[pl.ds(pl.multiple_of(dst * ky, ky), ky), :], ssem.at[s])

    def wait_gathers(s):
        pltpu.make_async_copy(h_hbm.at[pl.ds(0, rows * kx), :], xbuf.at[s], gsem.at[s]).wait()

    def wait_scatters(s):
        pltpu.make_async_copy(ybuf.at[s], slots_hbm.at[pl.ds(0, rows * ky), :], ssem.at[s]).wait()

    @pl.when((i == 0) & (n_valid > 0))
    def _():
        ybuf[0] = jnp.zeros((rows * ky, LANES), jnp.uint32)
        for p in range(2):
            pltpu.make_async_copy(ybuf.at[0], slots_hbm.at[pl.ds(n_real + p * rows * ky, rows * ky), :],
                                  ssem.at[0]).start()
        for p in range(2):
            pltpu.make_async_copy(ybuf.at[0], slots_hbm.at[pl.ds(n_real + p * rows * ky, rows * ky), :],
                                  ssem.at[0]).wait()
        for blk in range(depth - 1):
            def body(r, c, blk=blk):
                gather(blk, blk, r).start()
                return c
            lax.fori_loop(0, rows, body, 0, unroll=8)

    def step(s):
        ahead = (s + depth - 1) % depth
        wait_gathers(s)
        x_lo, x_hi = _unpack_bf16_pair(_load_token_major(xbuf.at[s], rows, kx))
        x = jnp.concatenate([x_lo, x_hi], axis=1).astype(BF16)
        for r in range(rows):
            gather(depth - 1, ahead, r).start(priority=r % 2)
        a = jnp.dot(x, w1_ref[...], preferred_element_type=F32)
        b = jnp.dot(x, w3_ref[...], preferred_element_type=F32)
        hb = (a * jax.nn.sigmoid(a) * b).astype(BF16)
        half = w2_ref.shape[1] // 2
        cw = 4 * LANES
        for c in range(half // cw):
            lo = jnp.dot(hb, w2_ref[:, c * cw:(c + 1) * cw], preferred_element_type=F32)
            hi = jnp.dot(hb, w2_ref[:, half + c * cw:half + (c + 1) * cw], preferred_element_type=F32)
            packed = _pack_bf16_pair(lo, hi)
            for j in range(cw // LANES):
                ybuf[s, pl.ds(c * (cw // LANES) + j, rows, stride=ky), :] = packed[:, j * LANES:(j + 1) * LANES]

        @pl.when(i >= 1)
        def _():
            wait_scatters(ahead)

        for r in range(rows):
            scatter(s, r).start(priority=r % 2)

        @pl.when(i + 1 == n_valid)
        def _():
            wait_scatters(s)
            for k in range(1, depth):
                wait_gathers((s + k) % depth)

    for s in range(depth):
        @pl.when((i < n_valid) & (slot == s))
        def _(s=s):
            step(s)


def _routed_experts(h_rows, plan, w1, w3, w2, layer):
    block_e, n_valid, first, wslot, next_e, row_src, row_dst = plan
    d = w1.shape[2]
    kx = ky = d // 2 // LANES
    n = h_rows.shape[0] // kx
    nblk = block_e.shape[0]
    rows = MOE_ROWS
    f = w1.shape[3]
    shifted = [jnp.concatenate([row_src[k:], jnp.broadcast_to(row_src[-1:], (k, rows))]) for k in range(3)]
    table = jnp.concatenate(shifted + [row_dst], axis=1).reshape(nblk, 1, 4 * rows)
    grid_spec = pltpu.PrefetchScalarGridSpec(
        num_scalar_prefetch=5,
        grid=(nblk,),
        in_specs=[pl.BlockSpec((1, 1, 4 * rows), lambda i, *_: (i, 0, 0), memory_space=pltpu.SMEM),
                  pl.BlockSpec(memory_space=pl.ANY),
                  pl.BlockSpec(memory_space=pl.ANY),
                  pl.BlockSpec(memory_space=pl.ANY),
                  pl.BlockSpec(memory_space=pl.ANY)],
        out_specs=pl.BlockSpec(memory_space=pl.ANY),
        scratch_shapes=[pltpu.VMEM((3, rows * kx, LANES), jnp.uint32),
                        pltpu.VMEM((3, rows * ky, LANES), jnp.uint32),
                        pltpu.VMEM((2, d, f), F32),
                        pltpu.VMEM((2, d, f), F32),
                        pltpu.VMEM((2, f, d), F32),
                        pltpu.VMEM((d, f), BF16),
                        pltpu.VMEM((d, f), BF16),
                        pltpu.VMEM((f, d), BF16),
                        pltpu.SemaphoreType.DMA((3,)),
                        pltpu.SemaphoreType.DMA((3,)),
                        pltpu.SemaphoreType.DMA((2,))],
    )
    return pl.pallas_call(
        functools.partial(_expert_kernel, layer=layer),
        grid_spec=grid_spec,
        out_shape=jax.ShapeDtypeStruct(((TOP_K * n + 2 * rows) * ky, LANES), jnp.uint32),
        compiler_params=_cparams(("arbitrary",)),
        name="routed_experts",
    )(block_e, n_valid, first, wslot, next_e, table, h_rows, w1, w3, w2)


def _dispatch_plan(idx, counts, n):
    rows = MOE_ROWS
    e = N_EXPERTS
    a_total = n * TOP_K
    nblk = (a_total + e * (rows - 1)) // rows
    packed = idx.reshape(a_total) * a_total + jnp.arange(a_total, dtype=jnp.int32)
    order = jnp.sort(packed) % a_total
    ar = jnp.arange(e, dtype=jnp.int32)
    blocks_e = (counts + rows - 1) // rows
    blk_end = jnp.cumsum(blocks_e)
    blk_start = blk_end - blocks_e
    start = jnp.cumsum(counts) - counts
    run_of = jnp.cumsum((counts > 0).astype(jnp.int32)) - 1
    later = (ar[None, :] > ar[:, None]) & (counts[None, :] > 0)
    next_of = jnp.min(jnp.where(later, ar[None, :], e), axis=1)
    next_of = jnp.where(next_of < e, next_of, -1)
    bi = jnp.arange(nblk, dtype=jnp.int32)
    n_valid = blk_end[-1:].astype(jnp.int32)
    live = bi < n_valid[0]
    block_e = jnp.minimum(jnp.sum((bi[:, None] >= blk_end[None, :]).astype(jnp.int32), axis=1), e - 1)
    onehot = block_e[:, None] == ar[None, :]
    pick = lambda table: jnp.sum(jnp.where(onehot, table[None, :], 0), axis=1)
    r = jnp.arange(rows, dtype=jnp.int32)[None, :]
    j = (bi - pick(blk_start))[:, None] * rows + r
    valid = (j < pick(counts)[:, None]) & live[:, None]
    a = order[jnp.clip(pick(start)[:, None] + j, 0, a_total - 1)]
    tok = a // TOP_K
    slot = a % TOP_K
    row_src = jnp.where(valid, tok, 0).astype(jnp.int32)
    dump = TOP_K * n + (bi % 2)[:, None] * rows + r
    row_dst = jnp.where(valid, slot * n + tok, dump).astype(jnp.int32)
    first = (live & (bi == pick(blk_start))).astype(jnp.int32)
    wslot = (pick(run_of) % 2).astype(jnp.int32)
    next_e = jnp.where(live, pick(next_of), -1).astype(jnp.int32)
    return block_e, n_valid, first, wslot, next_e, row_src, row_dst


def _combine_kernel(*refs):
    slot_refs = refs[:TOP_K]
    wt_ref, h_ref, w1_ref, w3_ref, w2_ref, lg_ref, lb_ref, hn_ref, hbn_ref = refs[TOP_K:]
    tm, d = h_ref.shape
    ky = d // 2 // LANES
    lo = None
    hi = None
    for k, s_ref in enumerate(slot_refs):
        l, u = _unpack_bf16_pair(_load_token_major(s_ref, tm, ky))
        wk = wt_ref[:, k:k + 1]
        lo = wk * l if lo is None else lo + wk * l
        hi = wk * u if hi is None else hi + wk * u
    routed = jnp.concatenate([lo, hi], axis=1)
    x = h_ref[...].astype(BF16)
    a = jnp.dot(x, w1_ref[...], preferred_element_type=F32)
    b = jnp.dot(x, w3_ref[...], preferred_element_type=F32)
    shared = jnp.dot((a * jax.nn.sigmoid(a) * b).astype(BF16), w2_ref[...], preferred_element_type=F32)
    hn = _layer_norm(DEEPNORM_ALPHA * h_ref[...] + routed + shared, lg_ref[...], lb_ref[...])
    hn_ref[...] = hn
    hbn_ref[...] = hn.astype(BF16)


def _combine_shared_ln(slots, wts, h, ws1, ws3, ws2, ln_g, ln_b):
    n, d = h.shape
    tm = 256
    ky = d // 2 // LANES
    f = ws1.shape[1]
    nt = n // tm
    const = lambda i: (0, 0)
    slot_specs = [pl.BlockSpec((tm * ky, LANES), lambda i, k=k: (k * nt + i, 0)) for k in range(TOP_K)]
    return pl.pallas_call(
        _combine_kernel,
        grid=(nt,),
        in_specs=slot_specs + [pl.BlockSpec((tm, LANES), lambda i: (i, 0)),
                               pl.BlockSpec((tm, d), lambda i: (i, 0)),
                               pl.BlockSpec((d, f), const),
                               pl.BlockSpec((d, f), const),
                               pl.BlockSpec((f, d), const),
                               pl.BlockSpec((1, d), const),
                               pl.BlockSpec((1, d), const)],
        out_specs=[pl.BlockSpec((tm, d), lambda i: (i, 0)),
                   pl.BlockSpec((tm, d), lambda i: (i, 0))],
        out_shape=[jax.ShapeDtypeStruct((n, d), F32), jax.ShapeDtypeStruct((n, d), BF16)],
        compiler_params=_cparams(("parallel",)),
        name="combine_shared_ln",
    )(*([slots] * TOP_K), wts, h, ws1, ws3, ws2, ln_g.reshape(1, d), ln_b.reshape(1, d))


def kernel(x, emb_ln_g, emb_ln_b, w_in, b_gate, lru_conv_w, lru_conv_b, lru_wr, lru_br, lru_wi, lru_bi, lru_lambda, ssm_conv_w, ssm_conv_b, ssm_dt_bias, ssm_a_log, ssm_d, ssm_norm_g, w_proj_lru, w_proj_ssm, w_proj_att, w_out, ln1_g, ln1_b, router_w, router_bias, w1, w3, w2, ws1, ws3, ws2, ln2_g, ln2_b):
    batch, seq, d = x.shape
    n = batch * seq
    slopes = (2.0 ** (-8.0 * jnp.arange(1, ATT_Q_HEADS + 1, dtype=F32) / ATT_Q_HEADS)).reshape(ATT_GROUPS, ATT_KV_HEADS)
    h, hb = _embed_ln(x.reshape(n, d), emb_ln_g, emb_ln_b)
    w_in_t = jnp.swapaxes(w_in, 1, 2)
    for l in range(DEPTH):
        proj = _in_proj(hb, w_in_t, l)
        qkv = _qkv_proj(hb, w_in_t, l)
        y_lru = _rglru(proj, lru_conv_w[l], lru_conv_b[l], lru_wr[l].astype(BF16), lru_br[l],
                       lru_wi[l].astype(BF16), lru_bi[l], lru_lambda[l], batch, seq)
        y_ssm = _ssd(proj, ssm_conv_w[l], ssm_conv_b[l], ssm_dt_bias[l], ssm_a_log[l], ssm_d[l],
                     ssm_norm_g[l], batch, seq)
        y_att = _attention(qkv, slopes, batch, seq)
        h, h_rows = _merge_outproj_ln(proj, b_gate[l], y_lru, y_ssm, y_att,
                                          w_proj_lru[l].astype(BF16), w_proj_ssm[l].astype(BF16),
                                          w_proj_att[l].astype(BF16), w_out[l].astype(BF16), h, ln1_g[l], ln1_b[l])
        idx, wts, tile_counts = _router(h, router_w[l], router_bias[l])
        counts = jnp.sum(tile_counts.reshape(-1, SUBLANES, LANES)[:, 0, :N_EXPERTS], axis=0).astype(jnp.int32)
        plan = _dispatch_plan(idx[:, :TOP_K], counts, n)
        slots = _routed_experts(h_rows, plan, w1, w3, w2, l)
        h, hb = _combine_shared_ln(slots, wts, h, ws1[l].astype(BF16), ws3[l].astype(BF16),
                                   ws2[l].astype(BF16), ln2_g[l], ln2_b[l])
    return h.reshape(batch, seq, d)
```

```python
import functools

import jax
import jax.numpy as jnp
from jax import lax
from jax.experimental import pallas as pl
from jax.experimental.pallas import tpu as pltpu

F32 = jnp.float32
BF16 = jnp.bfloat16
HIGHEST = lax.Precision.HIGHEST

D_MODEL = 2048
DEPTH = 2
D_RNN = 1024
LRU_BLOCKS = 8
LRU_CONV = 4
LRU_C = 8.0
SSM_D_INNER = 1024
SSM_HEAD_DIM = 64
SSM_HEADS = SSM_D_INNER // SSM_HEAD_DIM
SSM_GROUPS = 2
SSM_D_STATE = 128
SSM_CONV = 4
SSM_CHUNK = 128
SSM_XBC = SSM_D_INNER + 2 * SSM_GROUPS * SSM_D_STATE
ATT_HEAD_DIM = 128
ATT_KV_HEADS = 8
ATT_PATTERNS = ((128, 1), (512, 4), (2048, 16))
ATT_GROUPS = len(ATT_PATTERNS)
ATT_Q_HEADS = ATT_GROUPS * ATT_KV_HEADS
ATT_BLOCK = 128
ATT_D_OUT = ATT_KV_HEADS * ATT_HEAD_DIM
N_BRANCH = 3
N_EXPERTS = 64
EXPERT_DIM = 512
TOP_K = 8
N_EXPERT_GROUPS = 8
TOPK_GROUPS = 4
ROUTED_SCALE = 2.5
DEEPNORM_ALPHA = (2 * DEPTH) ** 0.25
LN_EPS = 1e-5
RMS_EPS = 1e-6

LANES = 128
SUBLANES = 8
MXU_WIDTH = 256
VMEM_LIMIT_BYTES = 56 * 1024 * 1024

IN_TILE = 1024
COL_LRU_X = N_BRANCH * D_MODEL
COL_LRU_G = COL_LRU_X + D_RNN
COL_SSM_Z = COL_LRU_G + D_RNN
COL_XBC = COL_SSM_Z + SSM_D_INNER
COL_DT = COL_XBC + SSM_XBC
COL_Q = COL_DT + SSM_HEADS
MAIN_COLS = -(-COL_Q // IN_TILE) * IN_TILE

MOE_ROWS = 256


def _cparams(sem):
    return pltpu.CompilerParams(dimension_semantics=sem, vmem_limit_bytes=VMEM_LIMIT_BYTES)


_NT = (((1,), (1,)), ((), ()))


def _bf16_terms(x, n):
    terms = []
    for _ in range(n):
        t = x.astype(BF16)
        terms.append(t)
        x = x - t.astype(F32)
    return terms


def _layer_norm(x, g, b):
    mu = jnp.mean(x, axis=-1, keepdims=True)
    xc = x - mu
    var = jnp.mean(xc * xc, axis=-1, keepdims=True)
    return xc * lax.rsqrt(var + LN_EPS) * g + b


def _store_token_major(ref, val):
    rows, w = val.shape
    k = w // LANES
    for j in range(k):
        ref[pl.ds(j, rows, stride=k), :] = val[:, j * LANES:(j + 1) * LANES]


def _load_token_major(ref, rows, k):
    return jnp.concatenate([ref[pl.ds(j, rows, stride=k), :] for j in range(k)], axis=1)


def _ln_kernel(x_ref, g_ref, b_ref, h_ref, hb_ref):
    y = _layer_norm(x_ref[...], g_ref[...], b_ref[...])
    h_ref[...] = y
    hb_ref[...] = y.astype(BF16)


def _embed_ln(x2d, g, b):
    n, d = x2d.shape
    tm = 512
    return pl.pallas_call(
        _ln_kernel,
        grid=(n // tm,),
        in_specs=[pl.BlockSpec((tm, d), lambda i: (i, 0)),
                  pl.BlockSpec((1, d), lambda i: (0, 0)),
                  pl.BlockSpec((1, d), lambda i: (0, 0))],
        out_specs=[pl.BlockSpec((tm, d), lambda i: (i, 0)),
                   pl.BlockSpec((tm, d), lambda i: (i, 0))],
        out_shape=[jax.ShapeDtypeStruct((n, d), F32), jax.ShapeDtypeStruct((n, d), BF16)],
        compiler_params=_cparams(("parallel",)),
        name="embed_ln",
    )(x2d, g.reshape(1, d), b.reshape(1, d))


def _in_proj_kernel(a_ref, wt_ref, o_ref, wb_ref):
    @pl.when(pl.program_id(1) == 0)
    def _():
        wb_ref[...] = wt_ref[...].astype(BF16)

    o_ref[...] = lax.dot_general(a_ref[...], wb_ref[...], _NT, preferred_element_type=F32).astype(o_ref.dtype)


def _in_proj(hb, w_in_t, layer):
    m, k = hb.shape
    tm = 2048
    tn = IN_TILE
    return pl.pallas_call(
        _in_proj_kernel,
        grid=(MAIN_COLS // tn, m // tm),
        in_specs=[pl.BlockSpec((tm, k), lambda j, i: (i, 0)),
                  pl.BlockSpec((None, tn, k), lambda j, i: (layer, j, 0))],
        out_specs=pl.BlockSpec((tm, tn), lambda j, i: (i, j)),
        out_shape=jax.ShapeDtypeStruct((m, MAIN_COLS), BF16),
        scratch_shapes=[pltpu.VMEM((tn, k), BF16)],
        compiler_params=_cparams(("parallel", "arbitrary")),
        name="in_proj",
    )(hb, w_in_t)


def _qkv_proj_kernel(a_ref, wt_hbm, o_ref, wf_ref, wb_ref, sem, *, layer, row0):
    @pl.when(pl.program_id(1) == 0)
    def _():
        rows = pl.ds(pl.multiple_of(row0 + pl.program_id(0) * wf_ref.shape[0], SUBLANES), wf_ref.shape[0])
        cp = pltpu.make_async_copy(wt_hbm.at[layer, rows, :], wf_ref, sem)
        cp.start()
        cp.wait()
        wb_ref[...] = wf_ref[...].astype(BF16)

    o_ref[...] = lax.dot_general(a_ref[...], wb_ref[...], _NT, preferred_element_type=F32)


def _qkv_proj(hb, w_in_t, layer):
    n, k = hb.shape
    tm = 2048
    tn = 1024
    ncols = ATT_Q_HEADS * ATT_HEAD_DIM + 2 * ATT_D_OUT
    return pl.pallas_call(
        functools.partial(_qkv_proj_kernel, layer=layer, row0=COL_Q),
        grid=(ncols // tn, n // tm),
        in_specs=[pl.BlockSpec((tm, k), lambda j, i: (i, 0)),
                  pl.BlockSpec(memory_space=pl.ANY)],
        out_specs=pl.BlockSpec((tm, tn), lambda j, i: (i, j)),
        out_shape=jax.ShapeDtypeStruct((n, ncols), F32),
        scratch_shapes=[pltpu.VMEM((tn, k), F32), pltpu.VMEM((tn, k), BF16), pltpu.SemaphoreType.DMA(())],
        compiler_params=_cparams(("parallel", "arbitrary")),
        name="qkv_proj",
    )(hb, w_in_t)


ATT_BATCH = 8


def _attn_kernel(slopes_ref, q0_ref, q1_ref, q2_ref, k_ref, v_ref, o_ref, kb_scr, vb_scr, res_scr, lse_scr):
    h = pl.program_id(1)
    t = o_ref.shape[0]
    blk = ATT_BLOCK
    scale = ATT_HEAD_DIM ** -0.5
    qi = lax.broadcasted_iota(jnp.int32, (blk, 2 * blk), 0)
    kj = lax.broadcasted_iota(jnp.int32, (blk, 2 * blk), 1)
    dist = blk + qi - kj
    q_refs = (q0_ref, q1_ref, q2_ref)

    def rows_of(d, r, u0, count):
        return slice(u0, u0 + count) if d == 1 else pl.ds(r + d * u0, count, stride=d)

    for g, (_, d) in enumerate(ATT_PATTERNS):
        u = t // d
        for r in range(d):
            kb_scr[g, r * u:(r + 1) * u, :] = k_ref[rows_of(d, r, 0, u), :].astype(BF16)
            vb_scr[g, r * u:(r + 1) * u, :] = v_ref[rows_of(d, r, 0, u), :].astype(BF16)

    for g, (window, d) in enumerate(ATT_PATTERNS):
        reach = window // d
        assert reach <= blk
        valid = (dist >= 0) & (dist <= reach)
        slope = slopes_ref[g, h]
        bias = jnp.where(valid, -(slope * d) * dist.astype(F32), -jnp.inf)
        bias_cur = bias[:, blk:]
        nb = (t // d) // blk
        q_ref = q_refs[g]
        for f0 in range(0, t // blk, ATT_BATCH):
            fs = list(range(f0, f0 + ATT_BATCH))
            scores, values = [], []
            for f in fs:
                r, i = divmod(f, nb)
                qb = q_ref[rows_of(d, r, i * blk, blk), :].astype(BF16)
                if i > 0:
                    rows, b = slice((f - 1) * blk, (f + 1) * blk), bias
                else:
                    rows, b = slice(f * blk, (f + 1) * blk), bias_cur
                scores.append(lax.dot_general(qb, kb_scr[g, rows, :], _NT, preferred_element_type=F32) * scale + b)
                values.append(vb_scr[g, rows, :])
            ms = [jnp.max(s, axis=-1, keepdims=True) for s in scores]
            ps = [jnp.exp(s - m) for s, m in zip(scores, ms)]
            ls = [jnp.sum(p, axis=-1, keepdims=True) for p in ps]
            pbs = [p.astype(BF16) for p in ps]
            for f, pb, vc, m, l in zip(fs, pbs, values, ms, ls):
                r, i = divmod(f, nb)
                rows = rows_of(d, r, i * blk, blk)
                res_scr[g, rows, :] = jnp.dot(pb, vc, preferred_element_type=F32) / l
                lse_scr[g, rows, :] = jnp.broadcast_to(m + jnp.log(l), (blk, LANES))
    ch = 256
    for c in range(t // ch):
        rows = slice(c * ch, (c + 1) * ch)
        lses = [lse_scr[g, rows, :] for g in range(ATT_GROUPS)]
        mx = functools.reduce(jnp.maximum, lses)
        ws = [jnp.exp(x - mx) for x in lses]
        num = sum(w * res_scr[g, rows, :] for g, w in enumerate(ws))
        o_ref[rows, :] = (num / sum(ws)).astype(o_ref.dtype)


def _attention(qkv, slopes, batch, seq):
    n = qkv.shape[0]
    hd = ATT_HEAD_DIM
    nh = ATT_KV_HEADS
    q_specs = [pl.BlockSpec((seq, hd), lambda b, h, g=g: (b, g * nh + h)) for g in range(ATT_GROUPS)]
    k_spec = pl.BlockSpec((seq, hd), lambda b, h: (b, ATT_Q_HEADS + h))
    v_spec = pl.BlockSpec((seq, hd), lambda b, h: (b, ATT_Q_HEADS + nh + h))
    return pl.pallas_call(
        _attn_kernel,
        grid=(batch, nh),
        in_specs=[pl.BlockSpec(memory_space=pltpu.SMEM)] + q_specs + [k_spec, v_spec],
        out_specs=pl.BlockSpec((seq, hd), lambda b, h: (b, h)),
        out_shape=jax.ShapeDtypeStruct((n, nh * hd), BF16),
        scratch_shapes=[pltpu.VMEM((ATT_GROUPS, seq, hd), BF16),
                        pltpu.VMEM((ATT_GROUPS, seq, hd), BF16),
                        pltpu.VMEM((ATT_GROUPS, seq, hd), F32),
                        pltpu.VMEM((ATT_GROUPS, seq, LANES), F32)],
        compiler_params=_cparams(("parallel", "parallel")),
        name="dilated_attention",
    )(slopes, qkv, qkv, qkv, qkv, qkv)


def _scan8(a, u, carry, row):
    for s in (1, 2, 4):
        a_sh = pltpu.roll(a, s, axis=0)
        u_sh = pltpu.roll(u, s, axis=0)
        m = row >= s
        u = jnp.where(m, a * u_sh + u, u)
        a = jnp.where(m, a * a_sh, a)
    return u + a * carry


def _lru_kernel(x_ref, g_ref, cw_ref, cb_ref, wr_ref, br_ref, wi_ref, bi_ref, lam_ref, o_ref,
                xpad_scr, a_scr, u_scr):
    t, c = o_ref.shape
    nb = LRU_BLOCKS
    bs = c // nb
    ch = 256
    pad = SUBLANES
    xpad_scr[0:pad, :] = jnp.zeros((pad, c), F32)
    for k in range(t // ch):
        xpad_scr[pad + k * ch:pad + (k + 1) * ch, :] = x_ref[k * ch:(k + 1) * ch, :].astype(F32)
    neg_lam = -lam_ref[...]
    sp = jnp.maximum(neg_lam, 0.0) + jnp.log1p(jnp.exp(-jnp.abs(neg_lam)))
    for k in range(t // ch):
        base = k * ch
        xc = cb_ref[...] + sum(
            cw_ref[j:j + 1, :] * xpad_scr[base + pad - (LRU_CONV - 1) + j:base + pad - (LRU_CONV - 1) + j + ch, :]
            for j in range(LRU_CONV))
        xcb = xc.astype(BF16)
        for n in range(nb):
            cols = slice(n * bs, (n + 1) * bs)
            xn = xcb[:, cols]
            r = jax.nn.sigmoid(jnp.dot(xn, wr_ref[n], preferred_element_type=F32) + br_ref[:, cols])
            ig = jax.nn.sigmoid(jnp.dot(xn, wi_ref[n], preferred_element_type=F32) + bi_ref[:, cols])
            log_a = -LRU_C * r * sp[:, cols]
            th = jnp.tanh(log_a)
            a_scr[base:base + ch, cols] = jnp.exp(log_a)
            v = -2.0 * th / (1.0 - th)
            root = jnp.where(v > 0.0, v * lax.rsqrt(v), 0.0)
            u_scr[base:base + ch, cols] = root * ig * xc[:, cols]
    row = lax.broadcasted_iota(jnp.int32, (SUBLANES, c), 0)

    def step(j, carry):
        rows = pl.ds(pl.multiple_of(j * SUBLANES, SUBLANES), SUBLANES)
        h8 = _scan8(a_scr[rows, :], u_scr[rows, :], carry, row)
        u_scr[rows, :] = h8
        return jnp.broadcast_to(h8[SUBLANES - 1:SUBLANES, :], (SUBLANES, c))

    lax.fori_loop(0, t // SUBLANES, step, jnp.zeros((SUBLANES, c), F32))
    for k in range(t // ch):
        rows = slice(k * ch, (k + 1) * ch)
        o_ref[rows, :] = (u_scr[rows, :] * jax.nn.gelu(g_ref[rows, :].astype(F32))).astype(o_ref.dtype)


def _rglru(proj, cw, cb, wr, br, wi, bi, lam, batch, seq):
    n = proj.shape[0]
    c = D_RNN
    bs = c // LRU_BLOCKS
    row = lambda b: (0, 0)
    return pl.pallas_call(
        _lru_kernel,
        grid=(batch,),
        in_specs=[pl.BlockSpec((seq, c), lambda b: (b, COL_LRU_X // c)),
                  pl.BlockSpec((seq, c), lambda b: (b, COL_LRU_G // c)),
                  pl.BlockSpec((LRU_CONV, c), row),
                  pl.BlockSpec((1, c), row),
                  pl.BlockSpec((LRU_BLOCKS, bs, bs), lambda b: (0, 0, 0)),
                  pl.BlockSpec((1, c), row),
                  pl.BlockSpec((LRU_BLOCKS, bs, bs), lambda b: (0, 0, 0)),
                  pl.BlockSpec((1, c), row),
                  pl.BlockSpec((1, c), row)],
        out_specs=pl.BlockSpec((seq, c), lambda b: (b, 0)),
        out_shape=jax.ShapeDtypeStruct((n, c), BF16),
        scratch_shapes=[pltpu.VMEM((seq + SUBLANES, c), F32),
                        pltpu.VMEM((seq, c), F32),
                        pltpu.VMEM((seq, c), F32)],
        compiler_params=_cparams(("parallel",)),
        name="rglru",
    )(proj, proj, cw, cb.reshape(1, c), wr, br.reshape(1, c), wi, bi.reshape(1, c), lam.reshape(1, c))


def _ssd_kernel(z_ref, xbc_ref, dtr_ref, cw_ref, cb_ref, dtb_ref, alog_ref, dskip_ref, ng_ref, o_ref,
                xpad_scr, st_scr):
    L = o_ref.shape[0]
    di = SSM_D_INNER
    ns = SSM_D_STATE
    pad = SUBLANES
    c = pl.program_id(1)

    @pl.when(c == 0)
    def _():
        xpad_scr[0:pad, :] = jnp.zeros((pad, SSM_XBC), F32)
        st_scr[...] = jnp.zeros(st_scr.shape, F32)

    xpad_scr[pad:pad + L, :] = xbc_ref[...].astype(F32)
    xc = cb_ref[...] + sum(
        cw_ref[j:j + 1, :] * xpad_scr[pad - (SSM_CONV - 1) + j:pad - (SSM_CONV - 1) + j + L, :]
        for j in range(SSM_CONV))
    xpad_scr[0:pad, :] = xpad_scr[L:L + pad, :]
    xc = xc * jax.nn.sigmoid(xc)
    xs = xc[:, :di]

    dt_in = dtr_ref[...].astype(F32) + dtb_ref[...]
    dt = jnp.maximum(dt_in, 0.0) + jnp.log1p(jnp.exp(-jnp.abs(dt_in)))
    adt = dt * (-jnp.exp(alog_ref[...]))
    ri = lax.broadcasted_iota(jnp.int32, (L, L), 0)
    ci = lax.broadcasted_iota(jnp.int32, (L, L), 1)
    causal = ri >= ci
    acum = jnp.dot(causal.astype(F32), adt, precision=HIGHEST, preferred_element_type=F32)
    acum_t = acum.T
    a_last = acum[L - 1:L, :]
    hl = lax.broadcasted_iota(jnp.int32, (LANES, di), 0)
    cl = lax.broadcasted_iota(jnp.int32, (LANES, di), 1)
    expand = (cl // SSM_HEAD_DIM == hl).astype(F32)
    dt_c = jnp.dot(dt, expand, precision=HIGHEST, preferred_element_type=F32)
    ea_c = jnp.dot(jnp.exp(acum), expand, precision=HIGHEST, preferred_element_type=F32)
    ds_c = jnp.dot(jnp.exp(a_last - acum), expand, precision=HIGHEST, preferred_element_type=F32)
    xdt = xs * dt_c
    xdt_b = xdt.astype(BF16)
    xw_b = (xdt * ds_c).astype(BF16)
    lane = lax.broadcasted_iota(jnp.int32, (L, LANES), 1)
    lo = lane < SSM_HEAD_DIM
    heads_per_group = SSM_HEADS // SSM_GROUPS
    ys = []
    for g in range(SSM_GROUPS):
        bm = xc[:, di + g * ns:di + (g + 1) * ns]
        cm = xc[:, di + SSM_GROUPS * ns + g * ns:di + SSM_GROUPS * ns + (g + 1) * ns]
        bm_b = bm.astype(BF16)
        cm_b = cm.astype(BF16)
        bm_t = bm.T.astype(BF16)
        cb = lax.dot_general(cm_b, bm_b, _NT, preferred_element_type=F32)
        for jp in range(heads_per_group // 2):
            j = g * (heads_per_group // 2) + jp
            cols = slice(j * LANES, (j + 1) * LANES)
            ms = []
            for hh in (2 * j, 2 * j + 1):
                seg = acum[:, hh:hh + 1] - acum_t[hh:hh + 1, :]
                decay = jnp.exp(jnp.where(causal, seg, -jnp.inf))
                ms.append((cb * decay).astype(BF16))
            mcat = jnp.concatenate(ms, axis=1)
            xp = xdt_b[:, cols]
            zero = jnp.zeros_like(xp)
            xcat = jnp.concatenate([jnp.where(lo, xp, zero), jnp.where(lo, zero, xp)], axis=0)
            y_diag = jnp.dot(mcat, xcat, preferred_element_type=F32)
            ent = st_scr[j]
            y_off = jnp.dot(cm_b, ent.astype(BF16), preferred_element_type=F32) * ea_c[:, cols]
            st_new = jnp.dot(bm_t, xw_b[:, cols], preferred_element_type=F32)
            st_scr[j] = st_new + ea_c[L - 1:L, cols] * ent
            ys.append(y_diag + y_off)
    y = jnp.concatenate(ys, axis=1) + dskip_ref[...] * xs
    zf = z_ref[...].astype(F32)
    y = y * (zf * jax.nn.sigmoid(zf))
    gw = di // SSM_GROUPS
    outs = []
    for g in range(SSM_GROUPS):
        yg = y[:, g * gw:(g + 1) * gw]
        outs.append(yg * lax.rsqrt(jnp.mean(yg * yg, axis=-1, keepdims=True) + RMS_EPS))
    o_ref[...] = (jnp.concatenate(outs, axis=1) * ng_ref[...]).astype(o_ref.dtype)


def _ssd(proj, cw, cb, dt_bias, a_log, d_skip, norm_g, batch, seq):
    n = proj.shape[0]
    L = SSM_CHUNK
    nc = seq // L
    di = SSM_D_INNER
    pad_h = LANES - SSM_HEADS
    dtb = jnp.pad(dt_bias, (0, pad_h)).reshape(1, LANES)
    alog = jnp.pad(a_log, (0, pad_h)).reshape(1, LANES)
    dskip = jnp.repeat(d_skip, SSM_HEAD_DIM).reshape(1, di)
    const = lambda b, c: (0, 0)
    return pl.pallas_call(
        _ssd_kernel,
        grid=(batch, nc),
        in_specs=[pl.BlockSpec((L, di), lambda b, c: (b * nc + c, COL_SSM_Z // di)),
                  pl.BlockSpec((L, SSM_XBC), lambda b, c: (b * nc + c, COL_XBC // SSM_XBC)),
                  pl.BlockSpec((L, LANES), lambda b, c: (b * nc + c, COL_DT // LANES)),
                  pl.BlockSpec((SSM_CONV, SSM_XBC), const),
                  pl.BlockSpec((1, SSM_XBC), const),
                  pl.BlockSpec((1, LANES), const),
                  pl.BlockSpec((1, LANES), const),
                  pl.BlockSpec((1, di), const),
                  pl.BlockSpec((1, di), const)],
        out_specs=pl.BlockSpec((L, di), lambda b, c: (b * nc + c, 0)),
        out_shape=jax.ShapeDtypeStruct((n, di), BF16),
        scratch_shapes=[pltpu.VMEM((L + SUBLANES, SSM_XBC), F32),
                        pltpu.VMEM((SSM_HEADS // 2, SSM_D_STATE, LANES), F32)],
        compiler_params=_cparams(("parallel", "arbitrary")),
        name="ssd",
    )(proj, proj, proj, cw, cb.reshape(1, SSM_XBC), dtb, alog, dskip, norm_g.reshape(1, di))


def _merge_kernel(g_ref, bg_ref, yl_ref, ys_ref, ya_ref, wl_ref, ws_ref, wa_ref, wo_ref,
                  h_ref, lg_ref, lb_ref, hn_ref, hrows_ref):
    d = h_ref.shape[1]
    merged = None
    for i, (y_ref, w_ref) in enumerate(((yl_ref, wl_ref), (ys_ref, ws_ref), (ya_ref, wa_ref))):
        gate = jax.nn.sigmoid(g_ref[:, i * d:(i + 1) * d].astype(F32) + bg_ref[:, i * d:(i + 1) * d])
        term = gate * jnp.dot(y_ref[...], w_ref[...], preferred_element_type=F32)
        merged = term if merged is None else merged + term
    mix = jnp.dot(merged.astype(BF16), wo_ref[...], preferred_element_type=F32)
    hn = _layer_norm(DEEPNORM_ALPHA * h_ref[...] + mix, lg_ref[...], lb_ref[...])
    hn_ref[...] = hn
    half = d // 2
    _store_token_major(hrows_ref, _pack_bf16_pair(hn[:, :half], hn[:, half:]))


def _merge_outproj_ln(proj, b_gate, y_lru, y_ssm, y_att, wl, ws, wa, wo, h, ln_g, ln_b):
    n, d = h.shape
    tm = 256
    gw = N_BRANCH * d
    kb = y_lru.shape[1]
    const = lambda i: (0, 0)
    once = pl.Buffered(1)
    return pl.pallas_call(
        _merge_kernel,
        grid=(n // tm,),
        in_specs=[pl.BlockSpec((tm, gw), lambda i: (i, 0)),
                  pl.BlockSpec((1, gw), const),
                  pl.BlockSpec((tm, kb), lambda i: (i, 0)),
                  pl.BlockSpec((tm, kb), lambda i: (i, 0)),
                  pl.BlockSpec((tm, kb), lambda i: (i, 0)),
                  pl.BlockSpec((kb, d), const, pipeline_mode=once),
                  pl.BlockSpec((kb, d), const, pipeline_mode=once),
                  pl.BlockSpec((kb, d), const, pipeline_mode=once),
                  pl.BlockSpec((d, d), const, pipeline_mode=once),
                  pl.BlockSpec((tm, d), lambda i: (i, 0)),
                  pl.BlockSpec((1, d), const),
                  pl.BlockSpec((1, d), const)],
        out_specs=[pl.BlockSpec((tm, d), lambda i: (i, 0)),
                   pl.BlockSpec((tm * (d // 2 // LANES), LANES), lambda i: (i, 0))],
        out_shape=[jax.ShapeDtypeStruct((n, d), F32),
                   jax.ShapeDtypeStruct((n * (d // 2 // LANES), LANES), jnp.uint32)],
        compiler_params=_cparams(("parallel",)),
        name="merge_outproj_ln",
    )(proj, b_gate.reshape(1, gw), y_lru, y_ssm, y_att, wl, ws, wa, wo, h,
      ln_g.reshape(1, d), ln_b.reshape(1, d))


def _seg_reduce(v, lane, op):
    for s in (1, 2, 4):
        up = pltpu.roll(v, LANES - s, axis=1)
        dn = pltpu.roll(v, s, axis=1)
        v = op(v, jnp.where((lane & s) == 0, up, dn))
    return v


def _router_kernel(h_ref, w_ref, b_ref, idx_ref, wt_ref, cnt_ref):
    tm = h_ref.shape[0]
    h_hi, h_lo = _bf16_terms(h_ref[...], 2)
    w_hi, w_lo = _bf16_terms(w_ref[...], 2)
    logits = (jnp.dot(h_hi, w_hi, preferred_element_type=F32) + jnp.dot(h_hi, w_lo, preferred_element_type=F32)
              + jnp.dot(h_lo, w_hi, preferred_element_type=F32))
    scores = jax.nn.sigmoid(logits)
    lane = lax.broadcasted_iota(jnp.int32, (tm, LANES), 1)
    lane_f = lane.astype(F32)
    real = lane < N_EXPERTS
    neg = -jnp.inf
    choice = jnp.where(real, scores + b_ref[...], neg)
    per_group = N_EXPERTS // N_EXPERT_GROUPS
    assert per_group == 8
    m1 = _seg_reduce(choice, lane, jnp.maximum)
    first = _seg_reduce(jnp.where(choice == m1, lane_f, float(LANES)), lane, jnp.minimum)
    m2 = _seg_reduce(jnp.where(lane_f == first, neg, choice), lane, jnp.maximum)
    gs = m1 + m2
    gidx = lane // per_group
    n_slots = LANES // per_group
    beaten = jnp.zeros((tm, LANES), jnp.int32)
    for k in range(1, n_slots):
        other = pltpu.roll(gs, per_group * k, axis=1)
        og = (gidx - k) & (n_slots - 1)
        wins = (other > gs) | ((other == gs) & (og < gidx))
        beaten = beaten + wins.astype(jnp.int32)
    masked = jnp.where((beaten < TOPK_GROUPS) & real, choice, neg)
    sel_i = jnp.zeros((tm, LANES), F32)
    sel_w = jnp.zeros((tm, LANES), F32)
    picked = jnp.zeros((tm, LANES), F32)
    for k in range(TOP_K):
        m = jnp.max(masked, axis=1, keepdims=True)
        am = jnp.min(jnp.where(masked == m, lane_f, float(LANES)), axis=1, keepdims=True)
        hit = lane_f == am
        wk = jnp.sum(jnp.where(hit, scores, 0.0), axis=1, keepdims=True)
        sel_i = jnp.where(lane == k, am, sel_i)
        sel_w = jnp.where(lane == k, wk, sel_w)
        picked = picked + jnp.where(hit, 1.0, 0.0)
        masked = jnp.where(hit, neg, masked)
    wsum = jnp.sum(sel_w, axis=1, keepdims=True)
    idx_ref[...] = sel_i.astype(jnp.int32)
    wt_ref[...] = sel_w / wsum * ROUTED_SCALE
    cnt_ref[...] = jnp.broadcast_to(jnp.sum(picked, axis=0, keepdims=True), cnt_ref.shape)


def _router(h, router_w, router_bias):
    n, d = h.shape
    tm = 512
    pad_e = LANES - N_EXPERTS
    w = jnp.pad(router_w, ((0, 0), (0, pad_e)))
    b = jnp.pad(router_bias, (0, pad_e)).reshape(1, LANES)
    return pl.pallas_call(
        _router_kernel,
        grid=(n // tm,),
        in_specs=[pl.BlockSpec((tm, d), lambda i: (i, 0)),
                  pl.BlockSpec((d, LANES), lambda i: (0, 0)),
                  pl.BlockSpec((1, LANES), lambda i: (0, 0))],
        out_specs=[pl.BlockSpec((tm, LANES), lambda i: (i, 0)),
                   pl.BlockSpec((tm, LANES), lambda i: (i, 0)),
                   pl.BlockSpec((SUBLANES, LANES), lambda i: (i, 0))],
        out_shape=[jax.ShapeDtypeStruct((n, LANES), jnp.int32), jax.ShapeDtypeStruct((n, LANES), F32),
                   jax.ShapeDtypeStruct((n // tm * SUBLANES, LANES), F32)],
        compiler_params=_cparams(("parallel",)),
        name="moe_router",
    )(h, w, b)


def _pack_bf16_pair(lo, hi):
    lo_bits = lax.bitcast_convert_type(lo.astype(BF16).astype(F32), jnp.uint32)
    hi_bits = lax.bitcast_convert_type(hi.astype(BF16).astype(F32), jnp.uint32)
    return (hi_bits & jnp.uint32(0xFFFF0000)) | (lo_bits >> 16)


def _unpack_bf16_pair(w):
    lo = lax.bitcast_convert_type(w << 16, F32)
    hi = lax.bitcast_convert_type(w & jnp.uint32(0xFFFF0000), F32)
    return lo, hi


def _expert_kernel(be_ref, nv_ref, first_ref, wslot_ref, nexte_ref, tab_ref,
                   h_hbm, w1_hbm, w3_hbm, w2_hbm, slots_hbm,
                   xbuf, ybuf, wf1, wf3, wf2, w1_ref, w3_ref, w2_ref, gsem, ssem, wsem, *, layer):
    i = pl.program_id(0)
    n_valid = nv_ref[0]
    rows = tab_ref.shape[2] // 4

    def weight_copies(e, ws):
        return (pltpu.make_async_copy(w1_hbm.at[layer, e], wf1.at[ws], wsem.at[ws]),
                pltpu.make_async_copy(w3_hbm.at[layer, e], wf3.at[ws], wsem.at[ws]),
                pltpu.make_async_copy(w2_hbm.at[layer, e], wf2.at[ws], wsem.at[ws]))

    @pl.when((i == 0) & (n_valid > 0))
    def _():
        for c in weight_copies(be_ref[0], 0):
            c.start(priority=1)

    for ws in range(2):
        @pl.when((i < n_valid) & (first_ref[i] == 1) & (wslot_ref[i] == ws))
        def _(ws=ws):
            for c in weight_copies(0, ws):
                c.wait()

            @pl.when(nexte_ref[i] >= 0)
            def _():
                for c in weight_copies(nexte_ref[i], 1 - ws):
                    c.start(priority=1)

            w1_ref[...] = wf1[ws].astype(BF16)
            w3_ref[...] = wf3[ws].astype(BF16)
            w2_ref[...] = wf2[ws].astype(BF16)

    kx = xbuf.shape[1] // rows
    ky = ybuf.shape[1] // rows
    n_real = slots_hbm.shape[0] - 2 * rows * ky
    depth = xbuf.shape[0]
    slot = lax.rem(i, depth)

    def gather(ahead_blocks, s, r, offset=0):
        tok = tab_ref[0, 0, ahead_blocks * rows + r] + offset
        return pltpu.make_async_copy(h_hbm.at[pl.ds(pl.multiple_of(tok * kx, kx), kx), :],
                                     xbuf.at[s, pl.ds(pl.multiple_of(r * kx, kx), kx), :], gsem.at[s])

    def scatter(s, r):
        dst = tab_ref[0, 0, 3 * rows + r]
        return pltpu.make_async_copy(ybuf.at[s, pl.ds(pl.multiple_of(r * ky, ky), ky), :],
                                     slots_hbm.at[pl.ds(pl.multiple_of(dst * ky, ky), ky), :], ssem.at[s])

    def wait_gathers(s):
        pltpu.make_async_copy(h_hbm.at[pl.ds(0, rows * kx), :], xbuf.at[s], gsem.at[s]).wait()

    def wait_scatters(s):
        pltpu.make_async_copy(ybuf.at[s], slots_hbm.at[pl.ds(0, rows * ky), :], ssem.at[s]).wait()

    @pl.when((i == 0) & (n_valid > 0))
    def _():
        ybuf[0] = jnp.zeros((rows * ky, LANES), jnp.uint32)
        for p in range(2):
            pltpu.make_async_copy(ybuf.at[0], slots_hbm.at[pl.ds(n_real + p * rows * ky, rows * ky), :],
                                  ssem.at[0]).start()
        for p in range(2):
            pltpu.make_async_copy(ybuf.at[0], slots_hbm.at[pl.ds(n_real + p * rows * ky, rows * ky), :],
                                  ssem.at[0]).wait()
        for blk in range(depth - 1):
            def body(r, c, blk=blk):
                gather(blk, blk, r).start()
                return c
            lax.fori_loop(0, rows, body, 0, unroll=8)

    def step(s):
        ahead = (s + depth - 1) % depth
        wait_gathers(s)
        x_lo, x_hi = _unpack_bf16_pair(_load_token_major(xbuf.at[s], rows, kx))
        x = jnp.concatenate([x_lo, x_hi], axis=1).astype(BF16)
        def issue_after(val, lo, hi):
            bits = lax.bitcast_convert_type(val[0:1, 0:1], jnp.uint32)
            zero = ((bits >> 16) >> 16)[0, 0].astype(jnp.int32)
            for r in range(lo, hi):
                gather(depth - 1, ahead, r, zero).start(priority=r % 2)

        nchunk = w1_ref.shape[1] // MXU_WIDTH
        per = rows // (2 * nchunk)
        hbs = []
        for c in range(nchunk):
            cols = slice(c * MXU_WIDTH, (c + 1) * MXU_WIDTH)
            a = jnp.dot(x, w1_ref[:, cols], preferred_element_type=F32)
            issue_after(a, 2 * c * per, (2 * c + 1) * per)
            b = jnp.dot(x, w3_ref[:, cols], preferred_element_type=F32)
            issue_after(b, (2 * c + 1) * per, (2 * c + 2) * per)
            hbs.append((a * jax.nn.sigmoid(a) * b).astype(BF16))
        hb = jnp.concatenate(hbs, axis=1)
        half = w2_ref.shape[1] // 2
        cw = 4 * LANES
        for c in range(half // cw):
            lo = jnp.dot(hb, w2_ref[:, c * cw:(c + 1) * cw], preferred_element_type=F32)
            hi = jnp.dot(hb, w2_ref[:, half + c * cw:half + (c + 1) * cw], preferred_element_type=F32)
            packed = _pack_bf16_pair(lo, hi)
            for j in range(cw // LANES):
                ybuf[s, pl.ds(c * (cw // LANES) + j, rows, stride=ky), :] = packed[:, j * LANES:(j + 1) * LANES]

        @pl.when(i >= 1)
        def _():
            wait_scatters(ahead)

        for r in range(rows):
            scatter(s, r).start(priority=r % 2)

        @pl.when(i + 1 == n_valid)
        def _():
            wait_scatters(s)
            for k in range(1, depth):
                wait_gathers((s + k) % depth)

    for s in range(depth):
        @pl.when((i < n_valid) & (slot == s))
        def _(s=s):
            step(s)


def _routed_experts(h_rows, plan, w1, w3, w2, layer):
    block_e, n_valid, first, wslot, next_e, row_src, row_dst = plan
    d = w1.shape[2]
    kx = ky = d // 2 // LANES
    n = h_rows.shape[0] // kx
    nblk = block_e.shape[0]
    rows = MOE_ROWS
    f = w1.shape[3]
    shifted = [jnp.concatenate([row_src[k:], jnp.broadcast_to(row_src[-1:], (k, rows))]) for k in range(3)]
    table = jnp.concatenate(shifted + [row_dst], axis=1).reshape(nblk, 1, 4 * rows)
    grid_spec = pltpu.PrefetchScalarGridSpec(
        num_scalar_prefetch=5,
        grid=(nblk,),
        in_specs=[pl.BlockSpec((1, 1, 4 * rows), lambda i, *_: (i, 0, 0), memory_space=pltpu.SMEM),
                  pl.BlockSpec(memory_space=pl.ANY),
                  pl.BlockSpec(memory_space=pl.ANY),
                  pl.BlockSpec(memory_space=pl.ANY),
                  pl.BlockSpec(memory_space=pl.ANY)],
        out_specs=pl.BlockSpec(memory_space=pl.ANY),
        scratch_shapes=[pltpu.VMEM((3, rows * kx, LANES), jnp.uint32),
                        pltpu.VMEM((3, rows * ky, LANES), jnp.uint32),
                        pltpu.VMEM((2, d, f), F32),
                        pltpu.VMEM((2, d, f), F32),
                        pltpu.VMEM((2, f, d), F32),
                        pltpu.VMEM((d, f), BF16),
                        pltpu.VMEM((d, f), BF16),
                        pltpu.VMEM((f, d), BF16),
                        pltpu.SemaphoreType.DMA((3,)),
                        pltpu.SemaphoreType.DMA((3,)),
                        pltpu.SemaphoreType.DMA((2,))],
    )
    return pl.pallas_call(
        functools.partial(_expert_kernel, layer=layer),
        grid_spec=grid_spec,
        out_shape=jax.ShapeDtypeStruct(((TOP_K * n + 2 * rows) * ky, LANES), jnp.uint32),
        compiler_params=_cparams(("arbitrary",)),
        name="routed_experts",
    )(block_e, n_valid, first, wslot, next_e, table, h_rows, w1, w3, w2)


def _dispatch_plan(idx, counts, n):
    rows = MOE_ROWS
    e = N_EXPERTS
    a_total = n * TOP_K
    nblk = (a_total + e * (rows - 1)) // rows
    packed = idx.reshape(a_total) * a_total + jnp.arange(a_total, dtype=jnp.int32)
    order = jnp.sort(packed) % a_total
    ar = jnp.arange(e, dtype=jnp.int32)
    blocks_e = (counts + rows - 1) // rows
    blk_end = jnp.cumsum(blocks_e)
    blk_start = blk_end - blocks_e
    start = jnp.cumsum(counts) - counts
    run_of = jnp.cumsum((counts > 0).astype(jnp.int32)) - 1
    later = (ar[None, :] > ar[:, None]) & (counts[None, :] > 0)
    next_of = jnp.min(jnp.where(later, ar[None, :], e), axis=1)
    next_of = jnp.where(next_of < e, next_of, -1)
    bi = jnp.arange(nblk, dtype=jnp.int32)
    n_valid = blk_end[-1:].astype(jnp.int32)
    live = bi < n_valid[0]
    block_e = jnp.minimum(jnp.sum((bi[:, None] >= blk_end[None, :]).astype(jnp.int32), axis=1), e - 1)
    onehot = block_e[:, None] == ar[None, :]
    pick = lambda table: jnp.sum(jnp.where(onehot, table[None, :], 0), axis=1)
    r = jnp.arange(rows, dtype=jnp.int32)[None, :]
    j = (bi - pick(blk_start))[:, None] * rows + r
    valid = (j < pick(counts)[:, None]) & live[:, None]
    a = order[jnp.clip(pick(start)[:, None] + j, 0, a_total - 1)]
    tok = a // TOP_K
    slot = a % TOP_K
    row_src = jnp.where(valid, tok, 0).astype(jnp.int32)
    dump = TOP_K * n + (bi % 2)[:, None] * rows + r
    row_dst = jnp.where(valid, slot * n + tok, dump).astype(jnp.int32)
    first = (live & (bi == pick(blk_start))).astype(jnp.int32)
    wslot = (pick(run_of) % 2).astype(jnp.int32)
    next_e = jnp.where(live, pick(next_of), -1).astype(jnp.int32)
    return block_e, n_valid, first, wslot, next_e, row_src, row_dst


def _combine_kernel(*refs):
    slot_refs = refs[:TOP_K]
    wt_ref, h_ref, w1_ref, w3_ref, w2_ref, lg_ref, lb_ref, hn_ref, hbn_ref = refs[TOP_K:]
    tm, d = h_ref.shape
    ky = d // 2 // LANES
    lo = None
    hi = None
    for k, s_ref in enumerate(slot_refs):
        l, u = _unpack_bf16_pair(_load_token_major(s_ref, tm, ky))
        wk = wt_ref[:, k:k + 1]
        lo = wk * l if lo is None else lo + wk * l
        hi = wk * u if hi is None else hi + wk * u
    routed = jnp.concatenate([lo, hi], axis=1)
    x = h_ref[...].astype(BF16)
    a = jnp.dot(x, w1_ref[...], preferred_element_type=F32)
    b = jnp.dot(x, w3_ref[...], preferred_element_type=F32)
    shared = jnp.dot((a * jax.nn.sigmoid(a) * b).astype(BF16), w2_ref[...], preferred_element_type=F32)
    hn = _layer_norm(DEEPNORM_ALPHA * h_ref[...] + routed + shared, lg_ref[...], lb_ref[...])
    hn_ref[...] = hn
    hbn_ref[...] = hn.astype(BF16)


def _combine_shared_ln(slots, wts, h, ws1, ws3, ws2, ln_g, ln_b):
    n, d = h.shape
    tm = 256
    ky = d // 2 // LANES
    f = ws1.shape[1]
    nt = n // tm
    const = lambda i: (0, 0)
    slot_specs = [pl.BlockSpec((tm * ky, LANES), lambda i, k=k: (k * nt + i, 0)) for k in range(TOP_K)]
    return pl.pallas_call(
        _combine_kernel,
        grid=(nt,),
        in_specs=slot_specs + [pl.BlockSpec((tm, LANES), lambda i: (i, 0)),
                               pl.BlockSpec((tm, d), lambda i: (i, 0)),
                               pl.BlockSpec((d, f), const),
                               pl.BlockSpec((d, f), const),
                               pl.BlockSpec((f, d), const),
                               pl.BlockSpec((1, d), const),
                               pl.BlockSpec((1, d), const)],
        out_specs=[pl.BlockSpec((tm, d), lambda i: (i, 0)),
                   pl.BlockSpec((tm, d), lambda i: (i, 0))],
        out_shape=[jax.ShapeDtypeStruct((n, d), F32), jax.ShapeDtypeStruct((n, d), BF16)],
        compiler_params=_cparams(("parallel",)),
        name="combine_shared_ln",
    )(*([slots] * TOP_K), wts, h, ws1, ws3, ws2, ln_g.reshape(1, d), ln_b.reshape(1, d))


def kernel(x, emb_ln_g, emb_ln_b, w_in, b_gate, lru_conv_w, lru_conv_b, lru_wr, lru_br, lru_wi, lru_bi, lru_lambda, ssm_conv_w, ssm_conv_b, ssm_dt_bias, ssm_a_log, ssm_d, ssm_norm_g, w_proj_lru, w_proj_ssm, w_proj_att, w_out, ln1_g, ln1_b, router_w, router_bias, w1, w3, w2, ws1, ws3, ws2, ln2_g, ln2_b):
    batch, seq, d = x.shape
    n = batch * seq
    slopes = (2.0 ** (-8.0 * jnp.arange(1, ATT_Q_HEADS + 1, dtype=F32) / ATT_Q_HEADS)).reshape(ATT_GROUPS, ATT_KV_HEADS)
    h, hb = _embed_ln(x.reshape(n, d), emb_ln_g, emb_ln_b)
    w_in_t = jnp.swapaxes(w_in, 1, 2)
    for l in range(DEPTH):
        proj = _in_proj(hb, w_in_t, l)
        qkv = _qkv_proj(hb, w_in_t, l)
        y_lru = _rglru(proj, lru_conv_w[l], lru_conv_b[l], lru_wr[l].astype(BF16), lru_br[l],
                       lru_wi[l].astype(BF16), lru_bi[l], lru_lambda[l], batch, seq)
        y_ssm = _ssd(proj, ssm_conv_w[l], ssm_conv_b[l], ssm_dt_bias[l], ssm_a_log[l], ssm_d[l],
                     ssm_norm_g[l], batch, seq)
        y_att = _attention(qkv, slopes, batch, seq)
        h, h_rows = _merge_outproj_ln(proj, b_gate[l], y_lru, y_ssm, y_att,
                                          w_proj_lru[l].astype(BF16), w_proj_ssm[l].astype(BF16),
                                          w_proj_att[l].astype(BF16), w_out[l].astype(BF16), h, ln1_g[l], ln1_b[l])
        idx, wts, tile_counts = _router(h, router_w[l], router_bias[l])
        counts = jnp.sum(tile_counts.reshape(-1, SUBLANES, LANES)[:, 0, :N_EXPERTS], axis=0).astype(jnp.int32)
        plan = _dispatch_plan(idx[:, :TOP_K], counts, n)
        slots = _routed_experts(h_rows, plan, w1, w3, w2, l)
        h, hb = _combine_shared_ln(slots, wts, h, ws1[l].astype(BF16), ws3[l].astype(BF16),
                                   ws2[l].astype(BF16), ln2_g[l], ln2_b[l])
    return h.reshape(batch, seq, d)
```
